```python
import jax, jax.numpy as jnp
from jax import lax
import numpy as np

D_MODEL = 1024
BATCH = 8
SEQ = 4096
DEPTH = 2
DEC_BATCH = 8
DEC_SEQ = 32
PAST_LEN = 1024

CHUNK = 64
D_MIX = D_MODEL
A_WIDTH = D_MIX // 2
A_HEADS = 8
A_HD = A_WIDTH // A_HEADS
A_CONV = 4
RG_C = 8.0
B_WIDTH = D_MIX // 4
B_CONV = 3
C_WIDTH = D_MIX // 4
C_HEADS = 4
C_HD = C_WIDTH // C_HEADS
MLP_CHUNK = 128
N_MEM = 256
X_HEADS = 4
X_HD = D_MODEL // X_HEADS
D_FF = ((8 * D_MODEL // 3 + 127) // 128) * 128
FFN_CONV = 3
EPS = 1e-6
IN_COLS = 2 * A_WIDTH + 3 * B_WIDTH + 2 * C_WIDTH
SPLITS = (A_WIDTH, 2 * A_WIDTH, 2 * A_WIDTH + B_WIDTH, 2 * A_WIDTH + 2 * B_WIDTH,
          2 * A_WIDTH + 3 * B_WIDTH, 2 * A_WIDTH + 3 * B_WIDTH + C_WIDTH)

kernel_name = "hybrid_stream_rglru_shortconv_chunkmlp_step"


def rms_norm(x, g):
    x32 = x.astype(jnp.float32)
    y = x32 * lax.rsqrt(jnp.mean(x32 * x32, axis=-1, keepdims=True) + EPS)
    return (y * g.astype(jnp.float32)).astype(x.dtype)


def group_rms_norm(x, g):
    bsz, t, _ = x.shape
    xg = x.reshape(bsz, t, C_HEADS, C_HD).astype(jnp.float32)
    y = xg * lax.rsqrt(jnp.mean(xg * xg, axis=-1, keepdims=True) + EPS)
    return (y.reshape(bsz, t, C_WIDTH) * g.astype(jnp.float32)).astype(x.dtype)


def causal_dwconv(x, prev, w):
    width = w.shape[0]
    t = x.shape[1]
    xp = jnp.concatenate([prev.astype(x.dtype), x], axis=1)
    y = xp[:, 0:t] * w[0]
    for k in range(1, width):
        y = y + xp[:, k:k + t] * w[k]
    return y, xp[:, t:]


def rg_lru(x, h0, w_r, b_r, w_i, b_i, lam):
    bsz, t, _ = x.shape
    f32 = jnp.float32
    x32 = x.astype(f32)
    xh = x32.reshape(bsz, t, A_HEADS, A_HD)
    r = jax.nn.sigmoid(jnp.einsum('bthi,hij->bthj', xh, w_r.astype(f32)).reshape(bsz, t, A_WIDTH) + b_r.astype(f32))
    gi = jax.nn.sigmoid(jnp.einsum('bthi,hij->bthj', xh, w_i.astype(f32)).reshape(bsz, t, A_WIDTH) + b_i.astype(f32))
    log_a = -RG_C * r * jax.nn.softplus(-lam.astype(f32))
    a = jnp.exp(log_a)
    b = jnp.sqrt(-jnp.expm1(2.0 * log_a)) * (gi * x32)
    b = b.at[:, 0].add(a[:, 0] * h0.astype(f32))

    def combine(left, right):
        a_l, b_l = left
        a_r, b_rr = right
        return a_l * a_r, a_r * b_l + b_rr

    _, h = lax.associative_scan(combine, (a, b), axis=1)
    return h.astype(x.dtype), h[:, -1].astype(x.dtype)


def chunk_spatial_gate(u, v, w_s, b_s):
    bsz, t, _ = v.shape
    L = min(t, MLP_CHUNK)
    n = t // L
    mask = jnp.tril(jnp.ones((L, L), dtype=bool))
    w = jnp.where(mask, w_s[:, :L, :L], 0)
    vc = v.reshape(bsz, n, L, C_HEADS, C_HD)
    bias = jnp.transpose(b_s[:, :L])[None, None, :, :, None]
    mixed = jnp.einsum('hts,bnshc->bnthc', w, vc) + bias
    return u * mixed.reshape(bsz, t, C_WIDTH)


def memory_kv(mem, w_k, w_v):
    bsz, m, _ = mem.shape
    k = (mem @ w_k).reshape(bsz, m, X_HEADS, X_HD)
    v = (mem @ w_v).reshape(bsz, m, X_HEADS, X_HD)
    return k, v


def cross_attend(xn, k, v, w_q, w_o):
    bsz, t, _ = xn.shape
    f32 = jnp.float32
    q = (xn @ w_q).reshape(bsz, t, X_HEADS, X_HD)
    s = jnp.einsum('bthd,bmhd->bhtm', q.astype(f32), k.astype(f32)) * (X_HD ** -0.5)
    p = jax.nn.softmax(s, axis=-1)
    o = jnp.einsum('bhtm,bmhd->bthd', p, v.astype(f32)).astype(xn.dtype)
    return o.reshape(bsz, t, D_MODEL) @ w_o


def trunk_layer(x, mem_k, mem_v, conv_a_prev, h_prev, conv_b_prev, ffn_prev, lp):
    xn = rms_norm(x, lp['g_mix'])
    z = xn @ lp['w_in']
    xa, ga, xb, gb, gc, uc, vc = jnp.split(z, SPLITS, axis=-1)
    xa, conv_a_new = causal_dwconv(xa, conv_a_prev, lp['conv_a_w'])
    xa = xa + lp['conv_a_b']
    ha, h_new = rg_lru(xa, h_prev, lp['w_rg'], lp['b_rg'], lp['w_ig'], lp['b_ig'], lp['lam'])
    y_a = jax.nn.gelu(ga) * ha
    zb, conv_b_new = causal_dwconv(gc * xb, conv_b_prev, lp['conv_b_w'])
    y_b = gb * zb
    uc = jax.nn.gelu(uc)
    vc = group_rms_norm(jax.nn.gelu(vc), lp['g_v'])
    y_c = chunk_spatial_gate(uc, vc, lp['w_s'], lp['b_s'])
    x = x + jnp.concatenate([y_a, y_b, y_c], axis=-1) @ lp['w_out']
    x = x + cross_attend(rms_norm(x, lp['g_x']), mem_k, mem_v, lp['w_q'], lp['w_o'])
    gu = rms_norm(x, lp['g_ffn']) @ lp['w_up']
    g, u = jnp.split(gu, 2, axis=-1)
    g, ffn_new = causal_dwconv(g, ffn_prev, lp['conv_f_w'])
    x = x + (jax.nn.silu(g) * u) @ lp['w_down']
    return x, conv_a_new, h_new, conv_b_new, ffn_new, vc


def setup_inputs(seed: int = 0) -> dict:
    key = jax.random.key(seed)
    ks = iter(jax.random.split(key, 40))

    def nrm(shape, scale):
        return jax.random.normal(next(ks), shape, jnp.float32) * scale

    def gain(shape):
        return 1.0 + nrm(shape, 0.05)

    u = jax.random.uniform(next(ks), (DEPTH, A_WIDTH), jnp.float32, minval=0.9, maxval=0.999)
    a_base = u ** (1.0 / RG_C)
    lam = jnp.log(a_base) - jnp.log1p(-a_base)
    return {
        "x_prompt": nrm((BATCH, SEQ, D_MODEL), 1.0),
        "x_sample": nrm((DEC_BATCH, DEC_SEQ, D_MODEL), 1.0),
        "mem_prompt": nrm((BATCH, N_MEM, D_MODEL), 1.0),
        "cache_mem_k": nrm((DEPTH, DEC_BATCH, N_MEM, X_HEADS, X_HD), 1.0),
        "cache_mem_v": nrm((DEPTH, DEC_BATCH, N_MEM, X_HEADS, X_HD), 1.0),
        "state_conv_a": nrm((DEPTH, DEC_BATCH, A_CONV - 1, A_WIDTH), 1.0),
        "state_h_a": nrm((DEPTH, DEC_BATCH, A_WIDTH), 0.5),
        "state_conv_b": nrm((DEPTH, DEC_BATCH, B_CONV - 1, B_WIDTH), 1.0),
        "state_conv_ffn": nrm((DEPTH, DEC_BATCH, FFN_CONV - 1, D_FF), 1.0),
        "g_mix": gain((DEPTH, D_MODEL)),
        "w_in": nrm((DEPTH, D_MODEL, IN_COLS), D_MODEL ** -0.5),
        "conv_a_w": nrm((DEPTH, A_CONV, A_WIDTH), A_CONV ** -0.5),
        "conv_a_b": nrm((DEPTH, A_WIDTH), 0.02),
        "w_rg": nrm((DEPTH, A_HEADS, A_HD, A_HD), A_HD ** -0.5),
        "b_rg": nrm((DEPTH, A_WIDTH), 0.02),
        "w_ig": nrm((DEPTH, A_HEADS, A_HD, A_HD), A_HD ** -0.5),
        "b_ig": nrm((DEPTH, A_WIDTH), 0.02),
        "lam": lam,
        "conv_b_w": nrm((DEPTH, B_CONV, B_WIDTH), B_CONV ** -0.5),
        "g_v": gain((DEPTH, C_WIDTH)),
        "w_s": nrm((DEPTH, C_HEADS, MLP_CHUNK, MLP_CHUNK), 0.5 * MLP_CHUNK ** -0.5),
        "b_s": 1.0 + nrm((DEPTH, C_HEADS, MLP_CHUNK), 0.1),
        "w_out": nrm((DEPTH, D_MIX, D_MODEL), D_MIX ** -0.5),
        "g_x": gain((DEPTH, D_MODEL)),
        "w_q": nrm((DEPTH, D_MODEL, D_MODEL), D_MODEL ** -0.5),
        "w_k": nrm((DEPTH, D_MODEL, D_MODEL), D_MODEL ** -0.5),
        "w_v": nrm((DEPTH, D_MODEL, D_MODEL), D_MODEL ** -0.5),
        "w_o": nrm((DEPTH, D_MODEL, D_MODEL), D_MODEL ** -0.5),
        "g_ffn": gain((DEPTH, D_MODEL)),
        "w_up": nrm((DEPTH, D_MODEL, 2 * D_FF), D_MODEL ** -0.5),
        "conv_f_w": nrm((DEPTH, FFN_CONV, D_FF), FFN_CONV ** -0.5),
        "w_down": nrm((DEPTH, D_FF, D_MODEL), D_FF ** -0.5),
        "g_final": gain((D_MODEL,)),
    }


def reference(x_prompt, x_sample, mem_prompt, cache_mem_k, cache_mem_v, state_conv_a, state_h_a,
              state_conv_b, state_conv_ffn, g_mix, w_in, conv_a_w, conv_a_b, w_rg, b_rg, w_ig, b_ig,
              lam, conv_b_w, g_v, w_s, b_s, w_out, g_x, w_q, w_k, w_v, w_o, g_ffn, w_up, conv_f_w,
              w_down, g_final):
    bp = x_prompt.shape[0]
    xp, xs = x_prompt, x_sample
    p_ca, p_h, p_cb, p_cf, p_mk, p_mv = [], [], [], [], [], []
    s_ca, s_h, s_cb, s_cf, s_vc = [], [], [], [], []
    for l in range(DEPTH):
        lp = dict(g_mix=g_mix[l], w_in=w_in[l], conv_a_w=conv_a_w[l], conv_a_b=conv_a_b[l],
                  w_rg=w_rg[l], b_rg=b_rg[l], w_ig=w_ig[l], b_ig=b_ig[l], lam=lam[l],
                  conv_b_w=conv_b_w[l], g_v=g_v[l], w_s=w_s[l], b_s=b_s[l], w_out=w_out[l],
                  g_x=g_x[l], w_q=w_q[l], w_o=w_o[l], g_ffn=g_ffn[l], w_up=w_up[l],
                  conv_f_w=conv_f_w[l], w_down=w_down[l])
        mk, mv = memory_kv(mem_prompt, w_k[l], w_v[l])
        dt = xp.dtype
        xp, ca, hh, cb, cf, _ = trunk_layer(
            xp, mk, mv,
            jnp.zeros((bp, A_CONV - 1, A_WIDTH), dt), jnp.zeros((bp, A_WIDTH), dt),
            jnp.zeros((bp, B_CONV - 1, B_WIDTH), dt), jnp.zeros((bp, FFN_CONV - 1, D_FF), dt), lp)
        p_ca.append(ca); p_h.append(hh); p_cb.append(cb); p_cf.append(cf); p_mk.append(mk); p_mv.append(mv)
        xs, ca2, hh2, cb2, cf2, vc2 = trunk_layer(
            xs, cache_mem_k[l], cache_mem_v[l], state_conv_a[l], state_h_a[l],
            state_conv_b[l], state_conv_ffn[l], lp)
        s_ca.append(ca2); s_h.append(hh2); s_cb.append(cb2); s_cf.append(cf2); s_vc.append(vc2)
    y_prompt = rms_norm(xp, g_final)
    y_sample = rms_norm(xs, g_final)
    return (y_prompt, y_sample,
            jnp.stack(p_ca), jnp.stack(p_h), jnp.stack(p_cb), jnp.stack(p_cf), jnp.stack(p_mk), jnp.stack(p_mv),
            jnp.stack(s_ca), jnp.stack(s_h), jnp.stack(s_cb), jnp.stack(s_cf), jnp.stack(s_vc))
```

```python
import functools

import jax
import jax.numpy as jnp
from jax import lax
from jax.experimental import pallas as pl
from jax.experimental.pallas import tpu as pltpu

D_MODEL = 1024
DEPTH = 2
A_WIDTH = 512
A_HEADS = 8
A_HD = 64
A_CONV = 4
RG_C = 8.0
B_WIDTH = 256
B_CONV = 3
C_WIDTH = 256
C_HEADS = 4
C_HD = 64
MLP_CHUNK = 128
N_MEM = 256
X_HEADS = 4
X_HD = 256
D_FF = 2816
FFN_CONV = 3
EPS = 1e-6
IN_COLS = 2 * A_WIDTH + 3 * B_WIDTH + 2 * C_WIDTH

SUBLANES = 8
LANES = 128
MXU_DIM = 256
PROMPT_TIME_TILE = 256
VMEM_LIMIT_BYTES = 56 * 1024 * 1024

F32 = jnp.float32
BF16 = jnp.bfloat16


def _dot(a, b):
    return jnp.dot(a, b, preferred_element_type=F32)


def _rms(x, g):
    ms = jnp.mean(x * x, axis=-1, keepdims=True)
    return x * lax.rsqrt(ms + EPS) * g


def _gelu(x):
    return jax.nn.gelu(x, approximate=True)


def _shifted_taps(buf_ref, n_prev, tt, w):
    base = SUBLANES - n_prev
    acc = buf_ref[:, base:base + tt, :] * w[0]
    for k in range(1, n_prev + 1):
        acc = acc + buf_ref[:, base + k:base + k + tt, :] * w[k]
    return acc


def _linear_scan(a, b, h0):
    nb, tt, c = a.shape
    groups = tt // SUBLANES
    a3 = a.reshape(nb * groups, SUBLANES, c)
    b3 = b.reshape(nb * groups, SUBLANES, c)
    row = lax.broadcasted_iota(jnp.int32, a3.shape, 1)
    d = 1
    while d < SUBLANES:
        a_prev = pltpu.roll(a3, d, 1)
        b_prev = pltpu.roll(b3, d, 1)
        keep = row >= d
        b3 = jnp.where(keep, a3 * b_prev + b3, b3)
        a3 = jnp.where(keep, a3 * a_prev, a3)
        d *= 2
    a4 = a3.reshape(nb, groups, SUBLANES, c)
    b4 = b3.reshape(nb, groups, SUBLANES, c)
    h = h0
    outs = []
    for g in range(groups):
        hg = a4[:, g] * h + b4[:, g]
        outs.append(hg)
        h = hg[:, SUBLANES - 1:SUBLANES, :]
    return jnp.concatenate(outs, axis=1), h


def _mixer_kernel(x_ref, ca0_ref, h0_ref, cb0_ref, gmix_ref, win_ref, caw_ref, cab_ref, wg_ref,
                  brg_ref, big_ref, lam_ref, cbw_ref, gv_ref, gsum_ref, ws_ref, bsf_ref, wout_ref,
                  xo_ref, cao_ref, ho_ref, cbo_ref, *rest, nb, tt, chunk, emit_vc):
    if emit_vc:
        vco_ref, xa_buf, pb_buf, h_carry, ycat = rest
    else:
        xa_buf, pb_buf, h_carry, ycat = rest
    rows = nb * tt

    @pl.when(pl.program_id(1) == 0)
    def _():
        xa_buf[:, SUBLANES - 3:SUBLANES, :] = ca0_ref[...]
        pb_buf[:, SUBLANES - 2:SUBLANES, :] = cb0_ref[...]
        h_carry[...] = h0_ref[...]

    x = x_ref[...].reshape(rows, D_MODEL)
    xn = _rms(x, gmix_ref[...]).astype(BF16)
    z = _dot(xn, win_ref[...])
    o = 0
    xa = z[:, o:o + A_WIDTH]; o += A_WIDTH
    ga = z[:, o:o + A_WIDTH]; o += A_WIDTH
    xb = z[:, o:o + B_WIDTH]; o += B_WIDTH
    gb = z[:, o:o + B_WIDTH]; o += B_WIDTH
    gc = z[:, o:o + B_WIDTH]; o += B_WIDTH
    uc = z[:, o:o + C_WIDTH]; o += C_WIDTH
    vc = z[:, o:o + C_WIDTH]

    xa_buf[:, SUBLANES:SUBLANES + tt, :] = xa.reshape(nb, tt, A_WIDTH)
    xc = _shifted_taps(xa_buf, A_CONV - 1, tt, caw_ref[...]) + cab_ref[...]
    ca_new = xa_buf[:, SUBLANES + tt - 3:SUBLANES + tt, :]
    xa_buf[:, SUBLANES - 3:SUBLANES, :] = ca_new
    cao_ref[...] = ca_new
    xc2 = xc.reshape(rows, A_WIDTH)
    xcb = xc2.astype(BF16)
    half = A_WIDTH // 2
    gz0 = _dot(xcb[:, :half], wg_ref[0])
    gz1 = _dot(xcb[:, half:], wg_ref[1])
    r = jax.nn.sigmoid(jnp.concatenate([gz0[:, :half], gz1[:, :half]], axis=1) + brg_ref[...])
    gi = jax.nn.sigmoid(jnp.concatenate([gz0[:, half:], gz1[:, half:]], axis=1) + big_ref[...])
    neg_lam = -lam_ref[...]
    softplus = jnp.maximum(neg_lam, 0.0) + jnp.log1p(jnp.exp(-jnp.abs(neg_lam)))
    log_a = (-RG_C) * r * softplus
    a = jnp.exp(log_a)
    one_minus_a2 = (1.0 + a * a) * jnp.tanh(-log_a)
    bterm = jnp.sqrt(one_minus_a2) * (gi * xc2)
    h, h_last = _linear_scan(a.reshape(nb, tt, A_WIDTH), bterm.reshape(nb, tt, A_WIDTH), h_carry[...])
    h_carry[...] = h_last
    ho_ref[...] = h_last
    ycat[:, 0:A_WIDTH] = (_gelu(ga) * h.reshape(rows, A_WIDTH)).astype(BF16)

    pb_buf[:, SUBLANES:SUBLANES + tt, :] = (gc * xb).reshape(nb, tt, B_WIDTH)
    zb = _shifted_taps(pb_buf, B_CONV - 1, tt, cbw_ref[...])
    cb_new = pb_buf[:, SUBLANES + tt - 2:SUBLANES + tt, :]
    pb_buf[:, SUBLANES - 2:SUBLANES, :] = cb_new
    cbo_ref[...] = cb_new
    ycat[:, A_WIDTH:A_WIDTH + B_WIDTH] = (gb * zb.reshape(rows, B_WIDTH)).astype(BF16)

    u = _gelu(uc)
    v = _gelu(vc)
    sq = v * v
    sq_hi = sq.astype(BF16)
    sq_lo = (sq - sq_hi.astype(F32)).astype(BF16)
    ms = _dot(sq_hi, gsum_ref[...]) + _dot(sq_lo, gsum_ref[...])
    vn = v * lax.rsqrt(ms + EPS) * gv_ref[...]
    if emit_vc:
        vco_ref[...] = vn.reshape(nb, tt, C_WIDTH)
    vnb = vn.astype(BF16)
    tri_r = lax.broadcasted_iota(jnp.int32, (chunk, chunk), 0)
    tri_c = lax.broadcasted_iota(jnp.int32, (chunk, chunk), 1)
    wtril = [jnp.where(tri_r >= tri_c, ws_ref[hd], 0.0).astype(BF16) for hd in range(C_HEADS)]
    lane = lax.broadcasted_iota(jnp.int32, (chunk, LANES), 1)
    first_head = lane < C_HD
    bias = bsf_ref[...]
    for c0 in range(0, rows, chunk):
        pieces = []
        for pair in range(C_WIDTH // LANES):
            vp = vnb[c0:c0 + chunk, pair * LANES:(pair + 1) * LANES]
            m0 = _dot(wtril[2 * pair], vp)
            m1 = _dot(wtril[2 * pair + 1], vp)
            pieces.append(jnp.where(first_head, m0, m1))
        mixed = jnp.concatenate(pieces, axis=1) + bias
        ycat[c0:c0 + chunk, A_WIDTH + B_WIDTH:] = (u[c0:c0 + chunk] * mixed).astype(BF16)

    out = x + _dot(ycat[...], wout_ref[...])
    xo_ref[...] = out.reshape(nb, tt, D_MODEL)


def _attn_kernel(x_ref, k_ref, v_ref, gx_ref, wq_ref, wo_ref, xo_ref, kb, vb, ocat, *, nb, tt):
    rows = nb * tt

    @pl.when(pl.program_id(1) == 0)
    def _():
        kb[...] = k_ref[...].astype(BF16)
        vb[...] = v_ref[...].astype(BF16)

    x = x_ref[...].reshape(rows, D_MODEL)
    xn = _rms(x, gx_ref[...]).astype(BF16)
    q = (_dot(xn, wq_ref[...]) * (X_HD ** -0.5)).astype(BF16)
    for b in range(nb):
        for hd in range(X_HEADS):
            cols = slice(hd * X_HD, (hd + 1) * X_HD)
            qh = q[b * tt:(b + 1) * tt, cols]
            s = lax.dot_general(qh, kb[b, :, cols], (((1,), (1,)), ((), ())),
                                preferred_element_type=F32)
            e = jnp.exp(s - jnp.max(s, axis=-1, keepdims=True))
            denom = jnp.sum(e, axis=-1, keepdims=True)
            oh = _dot(e.astype(BF16), vb[b, :, cols]) / denom
            ocat[b * tt:(b + 1) * tt, cols] = oh.astype(BF16)
    out = x + _dot(ocat[...], wo_ref[...])
    xo_ref[...] = out.reshape(nb, tt, D_MODEL)


def _ffn_kernel(x_ref, cf0_ref, gffn_ref, wup_ref, cfw_ref, wdown_ref, gfin_ref, xo_ref, cfo_ref,
                g_buf, *, nb, tt, final_norm):
    rows = nb * tt

    @pl.when(pl.program_id(1) == 0)
    def _():
        g_buf[:, SUBLANES - 2:SUBLANES, :] = cf0_ref[...]

    x = x_ref[...].reshape(rows, D_MODEL)
    xn = _rms(x, gffn_ref[...]).astype(BF16)
    gu = _dot(xn, wup_ref[...])
    g_buf[:, SUBLANES:SUBLANES + tt, :] = gu[:, :D_FF].reshape(nb, tt, D_FF)
    gconv = _shifted_taps(g_buf, FFN_CONV - 1, tt, cfw_ref[...]).reshape(rows, D_FF)
    cf_new = g_buf[:, SUBLANES + tt - 2:SUBLANES + tt, :]
    g_buf[:, SUBLANES - 2:SUBLANES, :] = cf_new
    cfo_ref[...] = cf_new
    act = (jax.nn.silu(gconv) * gu[:, D_FF:]).astype(BF16)
    out = x + _dot(act, wdown_ref[...])
    if final_norm:
        out = _rms(out, gfin_ref[...])
    xo_ref[...] = out.reshape(nb, tt, D_MODEL)


def _kv_kernel(mem_ref, wk_ref, wv_ref, ko_ref, vo_ref):
    m = mem_ref[0].astype(BF16)
    ko_ref[0] = _dot(m, wk_ref[...])
    vo_ref[0] = _dot(m, wv_ref[...])


def _whole():
    return pl.BlockSpec(memory_space=pltpu.MemorySpace.VMEM)


def _params():
    return pltpu.CompilerParams(dimension_semantics=("arbitrary", "arbitrary"),
                                vmem_limit_bytes=VMEM_LIMIT_BYTES)


def _tile_specs(nb, tt):
    x_spec = pl.BlockSpec((nb, tt, D_MODEL), lambda b, t: (b, t, 0))

    def state_spec(n, width):
        return pl.BlockSpec((nb, n, width), lambda b, t: (b, 0, 0))

    return x_spec, state_spec


def _mixer_call(x, ca0, h0, cb0, p, nb, tt, chunk, emit_vc):
    bsz, seq, _ = x.shape
    x_spec, state_spec = _tile_specs(nb, tt)
    rows = nb * tt
    out_shape = [jax.ShapeDtypeStruct(x.shape, F32),
                 jax.ShapeDtypeStruct((bsz, A_CONV - 1, A_WIDTH), F32),
                 jax.ShapeDtypeStruct((bsz, 1, A_WIDTH), F32),
                 jax.ShapeDtypeStruct((bsz, B_CONV - 1, B_WIDTH), F32)]
    out_specs = [x_spec, state_spec(A_CONV - 1, A_WIDTH), state_spec(1, A_WIDTH),
                 state_spec(B_CONV - 1, B_WIDTH)]
    if emit_vc:
        out_shape.append(jax.ShapeDtypeStruct((bsz, seq, C_WIDTH), F32))
        out_specs.append(pl.BlockSpec((nb, tt, C_WIDTH), lambda b, t: (b, t, 0)))
    kern = functools.partial(_mixer_kernel, nb=nb, tt=tt, chunk=chunk, emit_vc=emit_vc)
    return pl.pallas_call(
        kern,
        grid=(bsz // nb, seq // tt),
        in_specs=[x_spec, state_spec(A_CONV - 1, A_WIDTH), state_spec(1, A_WIDTH),
                  state_spec(B_CONV - 1, B_WIDTH)] + [_whole()] * 14,
        out_specs=out_specs,
        out_shape=out_shape,
        scratch_shapes=[pltpu.VMEM((nb, SUBLANES + tt, A_WIDTH), F32),
                        pltpu.VMEM((nb, SUBLANES + tt, B_WIDTH), F32),
                        pltpu.VMEM((nb, 1, A_WIDTH), F32),
                        pltpu.VMEM((rows, D_MODEL), BF16)],
        compiler_params=_params(),
        name="mixer",
    )(x, ca0, h0, cb0, p["g_mix"], p["w_in"], p["conv_a_w"], p["conv_a_b"], p["w_gates"],
      p["b_rg"], p["b_ig"], p["lam"], p["conv_b_w"], p["g_v"], p["group_mean"], p["w_s"],
      p["b_s_full"], p["w_out"])


def _attn_call(x, k, v, p, nb, tt):
    bsz, seq, _ = x.shape
    x_spec, _ = _tile_specs(nb, tt)
    kv_spec = pl.BlockSpec((nb, N_MEM, D_MODEL), lambda b, t: (b, 0, 0))
    kern = functools.partial(_attn_kernel, nb=nb, tt=tt)
    return pl.pallas_call(
        kern,
        grid=(bsz // nb, seq // tt),
        in_specs=[x_spec, kv_spec, kv_spec, _whole(), _whole(), _whole()],
        out_specs=x_spec,
        out_shape=jax.ShapeDtypeStruct(x.shape, F32),
        scratch_shapes=[pltpu.VMEM((nb, N_MEM, D_MODEL), BF16),
                        pltpu.VMEM((nb, N_MEM, D_MODEL), BF16),
                        pltpu.VMEM((nb * tt, D_MODEL), BF16)],
        compiler_params=_params(),
        name="attn",
    )(x, k, v, p["g_x"], p["w_q"], p["w_o"])


def _ffn_call(x, cf0, p, g_final, nb, tt, final_norm):
    bsz, seq, _ = x.shape
    x_spec, state_spec = _tile_specs(nb, tt)
    kern = functools.partial(_ffn_kernel, nb=nb, tt=tt, final_norm=final_norm)
    return pl.pallas_call(
        kern,
        grid=(bsz // nb, seq // tt),
        in_specs=[x_spec, state_spec(FFN_CONV - 1, D_FF)] + [_whole()] * 5,
        out_specs=[x_spec, state_spec(FFN_CONV - 1, D_FF)],
        out_shape=[jax.ShapeDtypeStruct(x.shape, F32),
                   jax.ShapeDtypeStruct((bsz, FFN_CONV - 1, D_FF), F32)],
        scratch_shapes=[pltpu.VMEM((nb, SUBLANES + tt, D_FF), F32)],
        compiler_params=_params(),
        name="ffn",
    )(x, cf0, p["g_ffn"], p["w_up"], p["conv_f_w"], p["w_down"], g_final)


def _kv_call(mem, w_k, w_v):
    bsz = mem.shape[0]
    spec = pl.BlockSpec((1, N_MEM, D_MODEL), lambda b: (b, 0, 0))
    shape = jax.ShapeDtypeStruct((bsz, N_MEM, D_MODEL), F32)
    return pl.pallas_call(
        _kv_kernel,
        grid=(bsz,),
        in_specs=[spec, _whole(), _whole()],
        out_specs=[spec, spec],
        out_shape=[shape, shape],
        compiler_params=pltpu.CompilerParams(dimension_semantics=("arbitrary",),
                                             vmem_limit_bytes=VMEM_LIMIT_BYTES),
        name="memory_kv",
    )(mem, w_k, w_v)


def _block_diag(blocks):
    n, r, c = blocks.shape
    eye = jnp.eye(n, dtype=blocks.dtype)
    return (blocks[:, :, None, :] * eye[:, None, :, None]).reshape(n * r, n * c)


def _layer_params(l, chunk, g_mix, w_in, conv_a_w, conv_a_b, w_rg, b_rg, w_ig, b_ig, lam, conv_b_w,
                  g_v, w_s, b_s, w_out, g_x, w_q, w_o, g_ffn, w_up, conv_f_w, w_down):
    per_half = A_HEADS // 2
    gates = [jnp.concatenate([_block_diag(w_rg[l, j * per_half:(j + 1) * per_half]),
                              _block_diag(w_ig[l, j * per_half:(j + 1) * per_half])], axis=1)
             for j in range(2)]
    group = jnp.arange(C_WIDTH) // C_HD
    return dict(
        g_mix=g_mix[l][None], w_in=w_in[l].astype(BF16),
        conv_a_w=conv_a_w[l][:, None, :], conv_a_b=conv_a_b[l][None],
        w_gates=jnp.stack(gates).astype(BF16), b_rg=b_rg[l][None], b_ig=b_ig[l][None],
        lam=lam[l][None], conv_b_w=conv_b_w[l][:, None, :], g_v=g_v[l][None],
        group_mean=((group[:, None] == group[None, :]).astype(F32) / C_HD).astype(BF16),
        w_s=w_s[l][:, :chunk, :chunk],
        b_s_full=jnp.repeat(jnp.transpose(b_s[l][:, :chunk]), C_HD, axis=1),
        w_out=w_out[l].astype(BF16),
        g_x=g_x[l][None], w_q=w_q[l].astype(BF16), w_o=w_o[l].astype(BF16),
        g_ffn=g_ffn[l][None], w_up=w_up[l].astype(BF16), conv_f_w=conv_f_w[l][:, None, :],
        w_down=w_down[l].astype(BF16))


def _tiles(bsz, seq):
    if seq <= MLP_CHUNK:
        return bsz, seq
    return 1, PROMPT_TIME_TILE


def _layer(x, k, v, ca0, h0, cb0, cf0, p, g_final, final_norm, emit_vc):
    bsz, seq, _ = x.shape
    nb, tt = _tiles(bsz, seq)
    mix = _mixer_call(x, ca0, h0, cb0, p, nb, tt, min(seq, MLP_CHUNK), emit_vc)
    x = _attn_call(mix[0], k, v, p, nb, tt)
    x, cf = _ffn_call(x, cf0, p, g_final, nb, tt, final_norm)
    return x, mix[1], mix[2][:, 0], mix[3], cf, (mix[4] if emit_vc else None)


def kernel(x_prompt, x_sample, mem_prompt, cache_mem_k, cache_mem_v, state_conv_a, state_h_a, state_conv_b, state_conv_ffn, g_mix, w_in, conv_a_w, conv_a_b, w_rg, b_rg, w_ig, b_ig, lam, conv_b_w, g_v, w_s, b_s, w_out, g_x, w_q, w_k, w_v, w_o, g_ffn, w_up, conv_f_w, w_down, g_final):
    bp, seq_p, _ = x_prompt.shape
    bs, seq_s, _ = x_sample.shape
    weights = (g_mix, w_in, conv_a_w, conv_a_b, w_rg, b_rg, w_ig, b_ig, lam, conv_b_w, g_v, w_s, b_s,
               w_out, g_x, w_q, w_o, g_ffn, w_up, conv_f_w, w_down)
    gfin = g_final[None]
    xp, xs = x_prompt, x_sample
    outs_p = [[] for _ in range(6)]
    outs_s = [[] for _ in range(5)]
    for l in range(DEPTH):
        last = l == DEPTH - 1
        pp = _layer_params(l, min(seq_p, MLP_CHUNK), *weights)
        ps = _layer_params(l, min(seq_s, MLP_CHUNK), *weights)
        mk, mv = _kv_call(mem_prompt, w_k[l].astype(BF16), w_v[l].astype(BF16))
        xp, ca, hh, cb, cf, _ = _layer(
            xp, mk, mv,
            jnp.zeros((bp, A_CONV - 1, A_WIDTH), F32), jnp.zeros((bp, 1, A_WIDTH), F32),
            jnp.zeros((bp, B_CONV - 1, B_WIDTH), F32), jnp.zeros((bp, FFN_CONV - 1, D_FF), F32),
            pp, gfin, last, False)
        for acc, val in zip(outs_p, (ca, hh, cb, cf, mk.reshape(bp, N_MEM, X_HEADS, X_HD),
                                     mv.reshape(bp, N_MEM, X_HEADS, X_HD))):
            acc.append(val)
        xs, ca, hh, cb, cf, vc = _layer(
            xs, cache_mem_k[l].reshape(bs, N_MEM, D_MODEL), cache_mem_v[l].reshape(bs, N_MEM, D_MODEL),
            state_conv_a[l], state_h_a[l][:, None, :], state_conv_b[l], state_conv_ffn[l],
            ps, gfin, last, True)
        for acc, val in zip(outs_s, (ca, hh, cb, cf, vc)):
            acc.append(val)
    return (xp, xs) + tuple(jnp.stack(a) for a in outs_p) + tuple(jnp.stack(a) for a in outs_s)
```

```python
import functools

import jax
import jax.numpy as jnp
from jax import lax
from jax.experimental import pallas as pl
from jax.experimental.pallas import tpu as pltpu

D_MODEL = 1024
DEPTH = 2
A_WIDTH = 512
A_HEADS = 8
A_HD = 64
A_CONV = 4
RG_C = 8.0
B_WIDTH = 256
B_CONV = 3
C_WIDTH = 256
C_HEADS = 4
C_HD = 64
MLP_CHUNK = 128
N_MEM = 256
X_HEADS = 4
X_HD = 256
D_FF = 2816
FFN_CONV = 3
EPS = 1e-6
IN_COLS = 2 * A_WIDTH + 3 * B_WIDTH + 2 * C_WIDTH

SUBLANES = 8
LANES = 128
PROMPT_TIME_TILE = 256
PROMPT_BATCH_TILE = 4
VMEM_LIMIT_BYTES = 56 * 1024 * 1024

F32 = jnp.float32
BF16 = jnp.bfloat16


def _dot(a, b):
    return jnp.dot(a, b, preferred_element_type=F32)


def _rms(x, g):
    ms = jnp.mean(x * x, axis=-1, keepdims=True)
    return x * lax.rsqrt(ms + EPS) * g


def _gelu(x):
    return jax.nn.gelu(x, approximate=True)


def _sublane_iota(shape):
    return lax.broadcasted_iota(jnp.int32, shape, len(shape) - 2)


def _causal_taps(cur, tail_ref, w):
    n_prev = tail_ref.shape[1]
    seg = cur.shape[1]
    last = cur[:, seg - n_prev:]
    wrapped = pltpu.roll(jnp.where(_sublane_iota(last.shape) == SUBLANES - 1, tail_ref[...], last), 1, 2)
    tail_ref[...] = last
    xp = jnp.concatenate([wrapped, cur], axis=1)
    acc = xp[:, 0:seg] * w[0]
    for j in range(1, n_prev + 1):
        acc = acc + xp[:, j:j + seg] * w[j]
    return acc


def _segment_scan(a, b, h_ref):
    seg = a.shape[1]
    local, decay = b[:, 0], a[:, 0]
    locals_, decays = [local], [decay]
    for s in range(1, seg):
        local = a[:, s] * local + b[:, s]
        decay = a[:, s] * decay
        locals_.append(local)
        decays.append(decay)
    h_in = h_ref[:, SUBLANES - 1:SUBLANES, :]
    row = _sublane_iota(local.shape)
    seg_a, seg_b = decay, local
    d = 1
    while d < SUBLANES:
        a_prev = pltpu.roll(seg_a, d, 1)
        b_prev = pltpu.roll(seg_b, d, 1)
        keep = row >= d
        seg_b = jnp.where(keep, seg_a * b_prev + seg_b, seg_b)
        seg_a = jnp.where(keep, seg_a * a_prev, seg_a)
        d *= 2
    h_end = seg_a * h_in + seg_b
    h_ref[...] = h_end
    h_start = jnp.where(row == 0, h_in, pltpu.roll(h_end, 1, 1))
    return jnp.stack([locals_[s] + decays[s] * h_start for s in range(seg)], axis=1)


def _mixer_kernel(x_ref, ca0_ref, h0_ref, cb0_ref, gmix_ref, win_ref, caw_ref, cab_ref, wg_ref,
                  brg_ref, big_ref, lam_ref, cbw_ref, gv_ref, gsum_ref, ws_ref, bsf_ref, wout_ref,
                  xo_ref, cao_ref, ho_ref, cbo_ref, *rest, nb, gb, tt, chunk, emit_vc):
    if emit_vc:
        vco_ref, ca_tail, cb_tail, h_state, wmix, ycat = rest
    else:
        ca_tail, cb_tail, h_state, wmix, ycat = rest
    rows = gb * tt
    seg = tt // SUBLANES

    @pl.when(pl.program_id(1) == 0)
    def _():
        ca_tail[...] = ca0_ref[...]
        cb_tail[...] = cb0_ref[...]
        h_state[...] = h0_ref[...]
        ti = lax.broadcasted_iota(jnp.int32, (tt, tt), 0)
        tj = lax.broadcasted_iota(jnp.int32, (tt, tt), 1)
        time_i = (ti % SUBLANES) * seg + ti // SUBLANES
        time_j = (tj % SUBLANES) * seg + tj // SUBLANES
        allowed = (time_i >= time_j) & (time_i // chunk == time_j // chunk)
        for hd in range(C_HEADS):
            wmix[hd] = jnp.where(allowed, ws_ref[hd], 0.0).astype(BF16)

    def project_in(g):
        x = x_ref[g * gb:(g + 1) * gb].reshape(rows, D_MODEL)
        return _dot(_rms(x, gmix_ref[...]).astype(BF16), win_ref[...])

    col = {}
    o = 0
    for name, width in (("xa", A_WIDTH), ("ga", A_WIDTH), ("xb", B_WIDTH), ("gb", B_WIDTH),
                        ("gc", B_WIDTH), ("uc", C_WIDTH), ("vc", C_WIDTH)):
        col[name] = slice(o, o + width)
        o += width
    half = A_WIDTH // 2

    def mix_matmuls(g, z):
        seqs = slice(g * gb, (g + 1) * gb)
        xc = _causal_taps(z[:, col["xa"]].reshape(gb, seg, SUBLANES, A_WIDTH), ca_tail.at[seqs], caw_ref[...])
        xc2 = (xc + cab_ref[...]).reshape(rows, A_WIDTH)
        xcb = xc2.astype(BF16)
        gz0 = _dot(xcb[:, :half], wg_ref[0])
        gz1 = _dot(xcb[:, half:], wg_ref[1])
        v = _gelu(z[:, col["vc"]])
        sq = v * v
        sq_hi = sq.astype(BF16)
        sq_lo = (sq - sq_hi.astype(F32)).astype(BF16)
        ms = _dot(sq_hi, gsum_ref[...]) + _dot(sq_lo, gsum_ref[...])
        vn = v * lax.rsqrt(ms + EPS) * gv_ref[...]
        if emit_vc:
            vco_ref[seqs] = vn.reshape(gb, tt, C_WIDTH)
        vnb = vn.astype(BF16)
        first_head = lax.broadcasted_iota(jnp.int32, (tt, LANES), 1) < C_HD
        mixed = []
        for b in range(gb):
            pieces = []
            for pair in range(C_WIDTH // LANES):
                vp = vnb[b * tt:(b + 1) * tt, pair * LANES:(pair + 1) * LANES]
                pieces.append(jnp.where(first_head, _dot(wmix[2 * pair], vp), _dot(wmix[2 * pair + 1], vp)))
            mixed.append(jnp.concatenate(pieces, axis=1) + bsf_ref[...])
        return xc2, gz0, gz1, jnp.concatenate(mixed, axis=0)

    def mix_elementwise(g, z, xc2, gz0, gz1, mixed):
        seqs = slice(g * gb, (g + 1) * gb)
        y = ycat.at[g * rows:(g + 1) * rows]
        r = jax.nn.sigmoid(jnp.concatenate([gz0[:, :half], gz1[:, :half]], axis=1) + brg_ref[...])
        gi = jax.nn.sigmoid(jnp.concatenate([gz0[:, half:], gz1[:, half:]], axis=1) + big_ref[...])
        neg_lam = -lam_ref[...]
        softplus = jnp.maximum(neg_lam, 0.0) + jnp.log1p(jnp.exp(-jnp.abs(neg_lam)))
        log_a = (-RG_C) * r * softplus
        a = jnp.exp(log_a)
        one_minus_a2 = (1.0 + a * a) * jnp.tanh(-log_a)
        bterm = jnp.sqrt(one_minus_a2) * (gi * xc2)
        h = _segment_scan(a.reshape(gb, seg, SUBLANES, A_WIDTH),
                          bterm.reshape(gb, seg, SUBLANES, A_WIDTH), h_state.at[seqs])
        y[:, 0:A_WIDTH] = (_gelu(z[:, col["ga"]]) * h.reshape(rows, A_WIDTH)).astype(BF16)
        zb = _causal_taps((z[:, col["gc"]] * z[:, col["xb"]]).reshape(gb, seg, SUBLANES, B_WIDTH),
                          cb_tail.at[seqs], cbw_ref[...])
        y[:, A_WIDTH:A_WIDTH + B_WIDTH] = (z[:, col["gb"]] * zb.reshape(rows, B_WIDTH)).astype(BF16)
        y[:, A_WIDTH + B_WIDTH:] = (_gelu(z[:, col["uc"]]) * mixed).astype(BF16)

    def project_out(g):
        x = x_ref[g * gb:(g + 1) * gb].reshape(rows, D_MODEL)
        out = x + _dot(ycat[g * rows:(g + 1) * rows], wout_ref[...])
        xo_ref[g * gb:(g + 1) * gb] = out.reshape(gb, tt, D_MODEL)

    groups = nb // gb
    z = project_in(0)
    small = mix_matmuls(0, z)
    for g in range(groups):
        z_next = project_in(g + 1) if g + 1 < groups else None
        mix_elementwise(g, z, *small)
        if z_next is not None:
            small = mix_matmuls(g + 1, z_next)
        project_out(g)
        z = z_next
    cao_ref[...] = ca_tail[...]
    ho_ref[...] = h_state[...]
    cbo_ref[...] = cb_tail[...]


def _attn_kernel(x_ref, k_ref, v_ref, gx_ref, wq_ref, wo_ref, xo_ref, kb, vb, ocat, *, nb, tt):
    rows = nb * tt

    @pl.when(pl.program_id(1) == 0)
    def _():
        kb[...] = k_ref[...].astype(BF16)
        vb[...] = v_ref[...].astype(BF16)

    x = x_ref[...].reshape(rows, D_MODEL)
    xn = _rms(x, gx_ref[...]).astype(BF16)
    q = (_dot(xn, wq_ref[...]) * (X_HD ** -0.5)).astype(BF16)
    for b in range(nb):
        for hd in range(X_HEADS):
            cols = slice(hd * X_HD, (hd + 1) * X_HD)
            qh = q[b * tt:(b + 1) * tt, cols]
            s = lax.dot_general(qh, kb[b, :, cols], (((1,), (1,)), ((), ())),
                                preferred_element_type=F32)
            e = jnp.exp(s - jnp.max(s, axis=-1, keepdims=True))
            denom = jnp.sum(e, axis=-1, keepdims=True)
            oh = _dot(e.astype(BF16), vb[b, :, cols]) / denom
            ocat[b * tt:(b + 1) * tt, cols] = oh.astype(BF16)
    out = x + _dot(ocat[...], wo_ref[...])
    xo_ref[...] = out.reshape(nb, tt, D_MODEL)


def _ffn_kernel(x_ref, cf0_ref, gffn_ref, wup_ref, cfw_ref, wdown_ref, gfin_ref, xo_ref, cfo_ref,
                cf_tail, *, nb, tt, final_norm):
    rows = nb * tt
    seg = tt // SUBLANES

    @pl.when(pl.program_id(1) == 0)
    def _():
        cf_tail[...] = cf0_ref[...]

    x = x_ref[...].reshape(rows, D_MODEL)
    xn = _rms(x, gffn_ref[...]).astype(BF16)
    gu = _dot(xn, wup_ref[...])
    gconv = _causal_taps(gu[:, :D_FF].reshape(nb, seg, SUBLANES, D_FF), cf_tail, cfw_ref[...])
    cfo_ref[...] = cf_tail[...]
    act = (jax.nn.silu(gconv.reshape(rows, D_FF)) * gu[:, D_FF:]).astype(BF16)
    out = x + _dot(act, wdown_ref[...])
    if final_norm:
        out = _rms(out, gfin_ref[...])
    xo_ref[...] = out.reshape(nb, tt, D_MODEL)


def _kv_kernel(mem_ref, wk_ref, wv_ref, ko_ref, vo_ref):
    m = mem_ref[0].astype(BF16)
    ko_ref[0] = _dot(m, wk_ref[...])
    vo_ref[0] = _dot(m, wv_ref[...])


def _whole():
    return pl.BlockSpec(memory_space=pltpu.MemorySpace.VMEM)


def _params():
    return pltpu.CompilerParams(dimension_semantics=("arbitrary", "arbitrary"),
                                vmem_limit_bytes=VMEM_LIMIT_BYTES)


def _tile_specs(nb, tt):
    x_spec = pl.BlockSpec((nb, tt, D_MODEL), lambda b, t: (b, t, 0))

    def state_spec(n, width):
        return pl.BlockSpec((nb, n, SUBLANES, width), lambda b, t: (b, 0, 0, 0))

    return x_spec, state_spec


def _mixer_call(x, ca0, h0, cb0, p, nb, tt, chunk, emit_vc):
    bsz, seq, _ = x.shape
    x_spec, state_spec = _tile_specs(nb, tt)
    h_spec = pl.BlockSpec((nb, SUBLANES, A_WIDTH), lambda b, t: (b, 0, 0))
    out_shape = [jax.ShapeDtypeStruct(x.shape, F32),
                 jax.ShapeDtypeStruct(ca0.shape, F32),
                 jax.ShapeDtypeStruct(h0.shape, F32),
                 jax.ShapeDtypeStruct(cb0.shape, F32)]
    out_specs = [x_spec, state_spec(A_CONV - 1, A_WIDTH), h_spec, state_spec(B_CONV - 1, B_WIDTH)]
    if emit_vc:
        out_shape.append(jax.ShapeDtypeStruct((bsz, seq, C_WIDTH), F32))
        out_specs.append(pl.BlockSpec((nb, tt, C_WIDTH), lambda b, t: (b, t, 0)))
    gb = nb if seq == tt else 1
    kern = functools.partial(_mixer_kernel, nb=nb, gb=gb, tt=tt, chunk=chunk, emit_vc=emit_vc)
    return pl.pallas_call(
        kern,
        grid=(bsz // nb, seq // tt),
        in_specs=[x_spec, state_spec(A_CONV - 1, A_WIDTH), h_spec,
                  state_spec(B_CONV - 1, B_WIDTH)] + [_whole()] * 14,
        out_specs=out_specs,
        out_shape=out_shape,
        scratch_shapes=[pltpu.VMEM((nb, A_CONV - 1, SUBLANES, A_WIDTH), F32),
                        pltpu.VMEM((nb, B_CONV - 1, SUBLANES, B_WIDTH), F32),
                        pltpu.VMEM((nb, SUBLANES, A_WIDTH), F32),
                        pltpu.VMEM((C_HEADS, tt, tt), BF16),
                        pltpu.VMEM((nb * tt, D_MODEL), BF16)],
        compiler_params=_params(),
        name="mixer",
    )(x, ca0, h0, cb0, p["g_mix"], p["w_in"], p["conv_a_w"], p["conv_a_b"], p["w_gates"],
      p["b_rg"], p["b_ig"], p["lam"], p["conv_b_w"], p["g_v"], p["group_mean"], p["w_s"],
      p["b_s_full"], p["w_out"])


def _attn_call(x, k, v, p, nb, tt):
    bsz, seq, _ = x.shape
    x_spec, _ = _tile_specs(nb, tt)
    kv_spec = pl.BlockSpec((nb, N_MEM, D_MODEL), lambda b, t: (b, 0, 0))
    kern = functools.partial(_attn_kernel, nb=nb, tt=tt)
    return pl.pallas_call(
        kern,
        grid=(bsz // nb, seq // tt),
        in_specs=[x_spec, kv_spec, kv_spec, _whole(), _whole(), _whole()],
        out_specs=x_spec,
        out_shape=jax.ShapeDtypeStruct(x.shape, F32),
        scratch_shapes=[pltpu.VMEM((nb, N_MEM, D_MODEL), BF16),
                        pltpu.VMEM((nb, N_MEM, D_MODEL), BF16),
                        pltpu.VMEM((nb * tt, D_MODEL), BF16)],
        compiler_params=_params(),
        name="attn",
    )(x, k, v, p["g_x"], p["w_q"], p["w_o"])


def _ffn_call(x, cf0, p, g_final, nb, tt, final_norm):
    bsz, seq, _ = x.shape
    x_spec, state_spec = _tile_specs(nb, tt)
    kern = functools.partial(_ffn_kernel, nb=nb, tt=tt, final_norm=final_norm)
    return pl.pallas_call(
        kern,
        grid=(bsz // nb, seq // tt),
        in_specs=[x_spec, state_spec(FFN_CONV - 1, D_FF)] + [_whole()] * 5,
        out_specs=[x_spec, state_spec(FFN_CONV - 1, D_FF)],
        out_shape=[jax.ShapeDtypeStruct(x.shape, F32), jax.ShapeDtypeStruct(cf0.shape, F32)],
        scratch_shapes=[pltpu.VMEM((nb, FFN_CONV - 1, SUBLANES, D_FF), F32)],
        compiler_params=_params(),
        name="ffn",
    )(x, cf0, p["g_ffn"], p["w_up"], p["conv_f_w"], p["w_down"], g_final)


def _kv_call(mem, w_k, w_v):
    bsz = mem.shape[0]
    spec = pl.BlockSpec((1, N_MEM, D_MODEL), lambda b: (b, 0, 0))
    shape = jax.ShapeDtypeStruct((bsz, N_MEM, D_MODEL), F32)
    return pl.pallas_call(
        _kv_kernel,
        grid=(bsz,),
        in_specs=[spec, _whole(), _whole()],
        out_specs=[spec, spec],
        out_shape=[shape, shape],
        compiler_params=pltpu.CompilerParams(dimension_semantics=("arbitrary",),
                                             vmem_limit_bytes=VMEM_LIMIT_BYTES),
        name="memory_kv",
    )(mem, w_k, w_v)


def _block_diag(blocks):
    n, r, c = blocks.shape
    eye = jnp.eye(n, dtype=blocks.dtype)
    return (blocks[:, :, None, :] * eye[:, None, :, None]).reshape(n * r, n * c)


def _tile_times(tt):
    i = jnp.arange(tt)
    return (i % SUBLANES) * (tt // SUBLANES) + i // SUBLANES


def _to_segments(x, tt):
    bsz, seq, c = x.shape
    return x.reshape(bsz, seq // tt, SUBLANES, tt // SUBLANES, c).swapaxes(2, 3).reshape(bsz, seq, c)


def _from_segments(x, tt):
    bsz, seq, c = x.shape
    return x.reshape(bsz, seq // tt, tt // SUBLANES, SUBLANES, c).swapaxes(2, 3).reshape(bsz, seq, c)


def _on_sublanes(state):
    return jnp.broadcast_to(state[:, :, None, :], state.shape[:2] + (SUBLANES, state.shape[2]))


def _layer_params(l, tt, chunk, g_mix, w_in, conv_a_w, conv_a_b, w_rg, b_rg, w_ig, b_ig, lam,
                  conv_b_w, g_v, w_s, b_s, w_out, g_x, w_q, w_o, g_ffn, w_up, conv_f_w, w_down):
    per_half = A_HEADS // 2
    gates = [jnp.concatenate([_block_diag(w_rg[l, j * per_half:(j + 1) * per_half]),
                              _block_diag(w_ig[l, j * per_half:(j + 1) * per_half])], axis=1)
             for j in range(2)]
    group = jnp.arange(C_WIDTH) // C_HD
    pos = _tile_times(tt) % chunk
    return dict(
        g_mix=g_mix[l][None], w_in=w_in[l].astype(BF16),
        conv_a_w=conv_a_w[l][:, None, :], conv_a_b=conv_a_b[l][None],
        w_gates=jnp.stack(gates).astype(BF16), b_rg=b_rg[l][None], b_ig=b_ig[l][None],
        lam=lam[l][None], conv_b_w=conv_b_w[l][:, None, :], g_v=g_v[l][None],
        group_mean=((group[:, None] == group[None, :]).astype(F32) / C_HD).astype(BF16),
        w_s=w_s[l][:, pos][:, :, pos],
        b_s_full=jnp.repeat(jnp.transpose(b_s[l][:, pos]), C_HD, axis=1),
        w_out=w_out[l].astype(BF16),
        g_x=g_x[l][None], w_q=w_q[l].astype(BF16), w_o=w_o[l].astype(BF16),
        g_ffn=g_ffn[l][None], w_up=w_up[l].astype(BF16), conv_f_w=conv_f_w[l][:, None, :],
        w_down=w_down[l].astype(BF16))


def _tiles(bsz, seq):
    if seq <= MLP_CHUNK:
        return bsz, seq
    return PROMPT_BATCH_TILE, PROMPT_TIME_TILE


def _layer(x, k, v, ca0, h0, cb0, cf0, p, g_final, final_norm, emit_vc, nb, tt, chunk):
    mix = _mixer_call(x, ca0, h0, cb0, p, nb, tt, chunk, emit_vc)
    x = _attn_call(mix[0], k, v, p, nb, tt)
    x, cf = _ffn_call(x, cf0, p, g_final, nb, tt, final_norm)
    last = SUBLANES - 1
    return (x, mix[1][:, :, last], mix[2][:, last], mix[3][:, :, last], cf[:, :, last],
            (mix[4] if emit_vc else None))


def kernel(x_prompt, x_sample, mem_prompt, cache_mem_k, cache_mem_v, state_conv_a, state_h_a, state_conv_b, state_conv_ffn, g_mix, w_in, conv_a_w, conv_a_b, w_rg, b_rg, w_ig, b_ig, lam, conv_b_w, g_v, w_s, b_s, w_out, g_x, w_q, w_k, w_v, w_o, g_ffn, w_up, conv_f_w, w_down, g_final):
    bp, seq_p, _ = x_prompt.shape
    bs, seq_s, _ = x_sample.shape
    weights = (g_mix, w_in, conv_a_w, conv_a_b, w_rg, b_rg, w_ig, b_ig, lam, conv_b_w, g_v, w_s, b_s,
               w_out, g_x, w_q, w_o, g_ffn, w_up, conv_f_w, w_down)
    gfin = g_final[None]
    nb_p, tt_p = _tiles(bp, seq_p)
    nb_s, tt_s = _tiles(bs, seq_s)
    chunk_p, chunk_s = min(seq_p, MLP_CHUNK), min(seq_s, MLP_CHUNK)
    xp, xs = _to_segments(x_prompt, tt_p), _to_segments(x_sample, tt_s)
    outs_p = [[] for _ in range(6)]
    outs_s = [[] for _ in range(5)]
    for l in range(DEPTH):
        last = l == DEPTH - 1
        pp = _layer_params(l, tt_p, chunk_p, *weights)
        ps = _layer_params(l, tt_s, chunk_s, *weights)
        mk, mv = _kv_call(mem_prompt, w_k[l].astype(BF16), w_v[l].astype(BF16))
        xp, ca, hh, cb, cf, _ = _layer(
            xp, mk, mv,
            jnp.zeros((bp, A_CONV - 1, SUBLANES, A_WIDTH), F32), jnp.zeros((bp, SUBLANES, A_WIDTH), F32),
            jnp.zeros((bp, B_CONV - 1, SUBLANES, B_WIDTH), F32),
            jnp.zeros((bp, FFN_CONV - 1, SUBLANES, D_FF), F32),
            pp, gfin, last, False, nb_p, tt_p, chunk_p)
        for acc, val in zip(outs_p, (ca, hh, cb, cf, mk.reshape(bp, N_MEM, X_HEADS, X_HD),
                                     mv.reshape(bp, N_MEM, X_HEADS, X_HD))):
            acc.append(val)
        xs, ca, hh, cb, cf, vc = _layer(
            xs, cache_mem_k[l].reshape(bs, N_MEM, D_MODEL), cache_mem_v[l].reshape(bs, N_MEM, D_MODEL),
            _on_sublanes(state_conv_a[l]), _on_sublanes(state_h_a[l][:, None, :])[:, 0],
            _on_sublanes(state_conv_b[l]), _on_sublanes(state_conv_ffn[l]),
            ps, gfin, last, True, nb_s, tt_s, chunk_s)
        for acc, val in zip(outs_s, (ca, hh, cb, cf, _from_segments(vc, tt_s))):
            acc.append(val)
    return ((_from_segments(xp, tt_p), _from_segments(xs, tt_s))
            + tuple(jnp.stack(a) for a in outs_p) + tuple(jnp.stack(a) for a in outs_s))
```

```python
import functools

import jax
import jax.numpy as jnp
from jax import lax
from jax.experimental import pallas as pl
from jax.experimental.pallas import tpu as pltpu

D_MODEL = 1024
DEPTH = 2
A_WIDTH = 512
A_HEADS = 8
A_HD = 64
A_CONV = 4
RG_C = 8.0
B_WIDTH = 256
B_CONV = 3
C_WIDTH = 256
C_HEADS = 4
C_HD = 64
MLP_CHUNK = 128
N_MEM = 256
X_HEADS = 4
X_HD = 256
D_FF = 2816
FFN_CONV = 3
EPS = 1e-6
IN_COLS = 2 * A_WIDTH + 3 * B_WIDTH + 2 * C_WIDTH

SUBLANES = 8
LANES = 128
PROMPT_TIME_TILE = 256
PROMPT_BATCH_TILE = 4
VMEM_LIMIT_BYTES = 56 * 1024 * 1024

F32 = jnp.float32
BF16 = jnp.bfloat16


def _dot(a, b):
    return jnp.dot(a, b, preferred_element_type=F32)


def _rms(x, g):
    ms = jnp.mean(x * x, axis=-1, keepdims=True)
    return x * lax.rsqrt(ms + EPS) * g


def _gelu(x):
    return jax.nn.gelu(x, approximate=True)


def _sublane_iota(shape):
    return lax.broadcasted_iota(jnp.int32, shape, len(shape) - 2)


def _causal_taps(cur, tail_ref, w):
    n_prev = tail_ref.shape[1]
    seg = cur.shape[1]
    last = cur[:, seg - n_prev:]
    wrapped = pltpu.roll(jnp.where(_sublane_iota(last.shape) == SUBLANES - 1, tail_ref[...], last), 1, 2)
    tail_ref[...] = last
    xp = jnp.concatenate([wrapped, cur], axis=1)
    acc = xp[:, 0:seg] * w[0]
    for j in range(1, n_prev + 1):
        acc = acc + xp[:, j:j + seg] * w[j]
    return acc


def _segment_scan(a, b, h_ref):
    seg = a.shape[1]
    local, decay = b[:, 0], a[:, 0]
    locals_, decays = [local], [decay]
    for s in range(1, seg):
        local = a[:, s] * local + b[:, s]
        decay = a[:, s] * decay
        locals_.append(local)
        decays.append(decay)
    h_in = h_ref[:, SUBLANES - 1:SUBLANES, :]
    row = _sublane_iota(local.shape)
    seg_a, seg_b = decay, local
    d = 1
    while d < SUBLANES:
        a_prev = pltpu.roll(seg_a, d, 1)
        b_prev = pltpu.roll(seg_b, d, 1)
        keep = row >= d
        seg_b = jnp.where(keep, seg_a * b_prev + seg_b, seg_b)
        seg_a = jnp.where(keep, seg_a * a_prev, seg_a)
        d *= 2
    h_end = seg_a * h_in + seg_b
    h_ref[...] = h_end
    h_start = jnp.where(row == 0, h_in, pltpu.roll(h_end, 1, 1))
    return jnp.stack([locals_[s] + decays[s] * h_start for s in range(seg)], axis=1)


def _mixer_kernel(x_ref, ca0_ref, h0_ref, cb0_ref, gmix_ref, win_ref, caw_ref, cab_ref, wg_ref,
                  brg_ref, big_ref, lam_ref, cbw_ref, gv_ref, gsum_ref, ws_ref, bsf_ref, wout_ref,
                  xo_ref, cao_ref, ho_ref, cbo_ref, *rest, nb, gb, tt, chunk, emit_vc):
    if emit_vc:
        vco_ref, ca_tail, cb_tail, h_state, wmix, ycat = rest
    else:
        ca_tail, cb_tail, h_state, wmix, ycat = rest
    rows = gb * tt
    seg = tt // SUBLANES

    @pl.when(pl.program_id(1) == 0)
    def _():
        ca_tail[...] = ca0_ref[...]
        cb_tail[...] = cb0_ref[...]
        h_state[...] = h0_ref[...]
        ti = lax.broadcasted_iota(jnp.int32, (tt, tt), 0)
        tj = lax.broadcasted_iota(jnp.int32, (tt, tt), 1)
        time_i = (ti % SUBLANES) * seg + ti // SUBLANES
        time_j = (tj % SUBLANES) * seg + tj // SUBLANES
        allowed = (time_i >= time_j) & (time_i // chunk == time_j // chunk)
        for hd in range(C_HEADS):
            wmix[hd] = jnp.where(allowed, ws_ref[hd], 0.0).astype(BF16)

    def project_in(g):
        x = x_ref[g * gb:(g + 1) * gb].reshape(rows, D_MODEL)
        return _dot(_rms(x, gmix_ref[...]).astype(BF16), win_ref[...])

    col = {}
    o = 0
    for name, width in (("xa", A_WIDTH), ("ga", A_WIDTH), ("xb", B_WIDTH), ("gb", B_WIDTH),
                        ("gc", B_WIDTH), ("uc", C_WIDTH), ("vc", C_WIDTH)):
        col[name] = slice(o, o + width)
        o += width
    half = A_WIDTH // 2

    def mix_matmuls(g, z):
        seqs = slice(g * gb, (g + 1) * gb)
        xc = _causal_taps(z[:, col["xa"]].reshape(gb, seg, SUBLANES, A_WIDTH), ca_tail.at[seqs], caw_ref[...])
        xc2 = (xc + cab_ref[...]).reshape(rows, A_WIDTH)
        xcb = xc2.astype(BF16)
        gz0 = _dot(xcb[:, :half], wg_ref[0])
        gz1 = _dot(xcb[:, half:], wg_ref[1])
        v = _gelu(z[:, col["vc"]])
        sq = v * v
        sq_hi = sq.astype(BF16)
        sq_lo = (sq - sq_hi.astype(F32)).astype(BF16)
        ms = _dot(sq_hi, gsum_ref[...]) + _dot(sq_lo, gsum_ref[...])
        vn = v * lax.rsqrt(ms + EPS) * gv_ref[...]
        if emit_vc:
            vco_ref[seqs] = vn.reshape(gb, tt, C_WIDTH)
        vnb = vn.astype(BF16)
        first_head = lax.broadcasted_iota(jnp.int32, (tt, LANES), 1) < C_HD
        mixed = []
        for b in range(gb):
            pieces = []
            for pair in range(C_WIDTH // LANES):
                vp = vnb[b * tt:(b + 1) * tt, pair * LANES:(pair + 1) * LANES]
                pieces.append(jnp.where(first_head, _dot(wmix[2 * pair], vp), _dot(wmix[2 * pair + 1], vp)))
            mixed.append(jnp.concatenate(pieces, axis=1) + bsf_ref[...])
        return xc2, gz0, gz1, jnp.concatenate(mixed, axis=0)

    def mix_elementwise(g, z, xc2, gz0, gz1, mixed):
        seqs = slice(g * gb, (g + 1) * gb)
        y = ycat.at[g * rows:(g + 1) * rows]
        r = jax.nn.sigmoid(jnp.concatenate([gz0[:, :half], gz1[:, :half]], axis=1) + brg_ref[...])
        gi = jax.nn.sigmoid(jnp.concatenate([gz0[:, half:], gz1[:, half:]], axis=1) + big_ref[...])
        neg_lam = -lam_ref[...]
        softplus = jnp.maximum(neg_lam, 0.0) + jnp.log1p(jnp.exp(-jnp.abs(neg_lam)))
        log_a = (-RG_C) * r * softplus
        a = jnp.exp(log_a)
        one_minus_a2 = (1.0 + a * a) * jnp.tanh(-log_a)
        bterm = jnp.sqrt(one_minus_a2) * (gi * xc2)
        h = _segment_scan(a.reshape(gb, seg, SUBLANES, A_WIDTH),
                          bterm.reshape(gb, seg, SUBLANES, A_WIDTH), h_state.at[seqs])
        y[:, 0:A_WIDTH] = (_gelu(z[:, col["ga"]]) * h.reshape(rows, A_WIDTH)).astype(BF16)
        zb = _causal_taps((z[:, col["gc"]] * z[:, col["xb"]]).reshape(gb, seg, SUBLANES, B_WIDTH),
                          cb_tail.at[seqs], cbw_ref[...])
        y[:, A_WIDTH:A_WIDTH + B_WIDTH] = (z[:, col["gb"]] * zb.reshape(rows, B_WIDTH)).astype(BF16)
        y[:, A_WIDTH + B_WIDTH:] = (_gelu(z[:, col["uc"]]) * mixed).astype(BF16)

    def project_out(g):
        x = x_ref[g * gb:(g + 1) * gb].reshape(rows, D_MODEL)
        out = x + _dot(ycat[g * rows:(g + 1) * rows], wout_ref[...])
        xo_ref[g * gb:(g + 1) * gb] = out.reshape(gb, tt, D_MODEL)

    groups = nb // gb
    z = project_in(0)
    small = mix_matmuls(0, z)
    for g in range(groups):
        z_next = project_in(g + 1) if g + 1 < groups else None
        mix_elementwise(g, z, *small)
        if z_next is not None:
            small = mix_matmuls(g + 1, z_next)
        project_out(g)
        z = z_next
    cao_ref[...] = ca_tail[...]
    ho_ref[...] = h_state[...]
    cbo_ref[...] = cb_tail[...]


def _attn_kernel(x_ref, kb, vb, gx_ref, wq_ref, wo_ref, xo_ref, ocat, *, nb, tt):
    rows = nb * tt
    x = x_ref[...].reshape(rows, D_MODEL)
    xn = _rms(x, gx_ref[...]).astype(BF16)
    q = (_dot(xn, wq_ref[...]) * (X_HD ** -0.5)).astype(BF16)
    for b in range(nb):
        for hd in range(X_HEADS):
            cols = slice(hd * X_HD, (hd + 1) * X_HD)
            qh = q[b * tt:(b + 1) * tt, cols]
            s = lax.dot_general(qh, kb[b, :, cols], (((1,), (1,)), ((), ())),
                                preferred_element_type=F32)
            e = jnp.exp(s - jnp.max(s, axis=-1, keepdims=True))
            denom = jnp.sum(e, axis=-1, keepdims=True)
            oh = _dot(e.astype(BF16), vb[b, :, cols]) / denom
            ocat[b * tt:(b + 1) * tt, cols] = oh.astype(BF16)
    out = x + _dot(ocat[...], wo_ref[...])
    xo_ref[...] = out.reshape(nb, tt, D_MODEL)


def _ffn_kernel(x_ref, cf0_ref, gffn_ref, wup_ref, cfw_ref, wdown_ref, gfin_ref, xo_ref, cfo_ref,
                cf_tail, *, nb, tt, final_norm):
    rows = nb * tt
    seg = tt // SUBLANES

    @pl.when(pl.program_id(1) == 0)
    def _():
        cf_tail[...] = cf0_ref[...]

    x = x_ref[...].reshape(rows, D_MODEL)
    xn = _rms(x, gffn_ref[...]).astype(BF16)
    gu = _dot(xn, wup_ref[...])
    gconv = _causal_taps(gu[:, :D_FF].reshape(nb, seg, SUBLANES, D_FF), cf_tail, cfw_ref[...])
    cfo_ref[...] = cf_tail[...]
    act = (jax.nn.silu(gconv.reshape(rows, D_FF)) * gu[:, D_FF:]).astype(BF16)
    out = x + _dot(act, wdown_ref[...])
    if final_norm:
        out = _rms(out, gfin_ref[...])
    xo_ref[...] = out.reshape(nb, tt, D_MODEL)


def _kv_kernel(mem_ref, wk_ref, wv_ref, ko_ref, vo_ref, kb_ref, vb_ref):
    m = mem_ref[0].astype(BF16)
    for w_ref, o_ref, b_ref in ((wk_ref, ko_ref, kb_ref), (wv_ref, vo_ref, vb_ref)):
        kv = _dot(m, w_ref[...])
        b_ref[...] = kv.astype(BF16)
        for hd in range(X_HEADS):
            o_ref[:, hd, :] = kv[:, hd * X_HD:(hd + 1) * X_HD]


def _whole():
    return pl.BlockSpec(memory_space=pltpu.MemorySpace.VMEM)


def _of_layer(arr, l):
    zeros = (0,) * (arr.ndim - 1)
    return pl.BlockSpec((None,) + arr.shape[1:], lambda b, t: (l,) + zeros, pipeline_mode=pl.Buffered(1))


def _params():
    return pltpu.CompilerParams(dimension_semantics=("arbitrary", "arbitrary"),
                                vmem_limit_bytes=VMEM_LIMIT_BYTES)


def _tile_specs(nb, tt):
    x_spec = pl.BlockSpec((nb, tt, D_MODEL), lambda b, t: (b, t, 0))

    def state_spec(n, width):
        return pl.BlockSpec((nb, n, SUBLANES, width), lambda b, t: (b, 0, 0, 0))

    return x_spec, state_spec


def _mixer_call(x, ca0, h0, cb0, p, l, nb, tt, chunk, emit_vc):
    bsz, seq, _ = x.shape
    layer_params = [p[name] for name in ("g_mix", "w_in", "conv_a_w", "conv_a_b", "w_gates", "b_rg", "b_ig",
                                         "lam", "conv_b_w", "g_v")]
    mix_params = [p[name] for name in ("w_s", "b_s_full", "w_out")]
    x_spec, state_spec = _tile_specs(nb, tt)
    h_spec = pl.BlockSpec((nb, SUBLANES, A_WIDTH), lambda b, t: (b, 0, 0))
    out_shape = [jax.ShapeDtypeStruct(x.shape, F32),
                 jax.ShapeDtypeStruct(ca0.shape, F32),
                 jax.ShapeDtypeStruct(h0.shape, F32),
                 jax.ShapeDtypeStruct(cb0.shape, F32)]
    out_specs = [x_spec, state_spec(A_CONV - 1, A_WIDTH), h_spec, state_spec(B_CONV - 1, B_WIDTH)]
    if emit_vc:
        out_shape.append(jax.ShapeDtypeStruct((bsz, seq, C_WIDTH), F32))
        out_specs.append(pl.BlockSpec((nb, tt, C_WIDTH), lambda b, t: (b, t, 0)))
    gb = nb if seq == tt else 1
    kern = functools.partial(_mixer_kernel, nb=nb, gb=gb, tt=tt, chunk=chunk, emit_vc=emit_vc)
    return pl.pallas_call(
        kern,
        grid=(bsz // nb, seq // tt),
        in_specs=([x_spec, state_spec(A_CONV - 1, A_WIDTH), h_spec, state_spec(B_CONV - 1, B_WIDTH)]
                  + [_of_layer(a, l) for a in layer_params] + [_whole()]
                  + [_of_layer(a, l) for a in mix_params]),
        out_specs=out_specs,
        out_shape=out_shape,
        scratch_shapes=[pltpu.VMEM((nb, A_CONV - 1, SUBLANES, A_WIDTH), F32),
                        pltpu.VMEM((nb, B_CONV - 1, SUBLANES, B_WIDTH), F32),
                        pltpu.VMEM((nb, SUBLANES, A_WIDTH), F32),
                        pltpu.VMEM((C_HEADS, tt, tt), BF16),
                        pltpu.VMEM((nb * tt, D_MODEL), BF16)],
        compiler_params=_params(),
        name="mixer",
    )(x, ca0, h0, cb0, *layer_params, p["group_mean"], *mix_params)


def _attn_call(x, kb, vb, p, l, nb, tt):
    bsz, seq, _ = x.shape
    x_spec, _ = _tile_specs(nb, tt)
    kv_spec = pl.BlockSpec((None, nb, N_MEM, D_MODEL), lambda b, t: (l, b, 0, 0))
    weights = [p["g_x"], p["w_q"], p["w_o"]]
    kern = functools.partial(_attn_kernel, nb=nb, tt=tt)
    return pl.pallas_call(
        kern,
        grid=(bsz // nb, seq // tt),
        in_specs=[x_spec, kv_spec, kv_spec] + [_of_layer(a, l) for a in weights],
        out_specs=x_spec,
        out_shape=jax.ShapeDtypeStruct(x.shape, F32),
        scratch_shapes=[pltpu.VMEM((nb * tt, D_MODEL), BF16)],
        compiler_params=_params(),
        name="attn",
    )(x, kb, vb, *weights)


def _ffn_call(x, cf0, p, l, nb, tt, final_norm):
    bsz, seq, _ = x.shape
    x_spec, state_spec = _tile_specs(nb, tt)
    weights = [p["g_ffn"], p["w_up"], p["conv_f_w"], p["w_down"]]
    kern = functools.partial(_ffn_kernel, nb=nb, tt=tt, final_norm=final_norm)
    return pl.pallas_call(
        kern,
        grid=(bsz // nb, seq // tt),
        in_specs=[x_spec, state_spec(FFN_CONV - 1, D_FF)] + [_of_layer(a, l) for a in weights] + [_whole()],
        out_specs=[x_spec, state_spec(FFN_CONV - 1, D_FF)],
        out_shape=[jax.ShapeDtypeStruct(x.shape, F32), jax.ShapeDtypeStruct(cf0.shape, F32)],
        scratch_shapes=[pltpu.VMEM((nb, FFN_CONV - 1, SUBLANES, D_FF), F32)],
        compiler_params=_params(),
        name="ffn",
    )(x, cf0, *weights, p["g_final"])


def _kv_call(mem, w_k, w_v):
    bsz = mem.shape[0]
    depth = w_k.shape[0]
    w_spec = pl.BlockSpec((None, D_MODEL, D_MODEL), lambda l, b: (l, 0, 0))
    out5 = pl.BlockSpec((None, None, N_MEM, X_HEADS, X_HD), lambda l, b: (l, b, 0, 0, 0))
    out4 = pl.BlockSpec((None, None, N_MEM, D_MODEL), lambda l, b: (l, b, 0, 0))
    shape5 = jax.ShapeDtypeStruct((depth, bsz, N_MEM, X_HEADS, X_HD), F32)
    shape4 = jax.ShapeDtypeStruct((depth, bsz, N_MEM, D_MODEL), BF16)
    return pl.pallas_call(
        _kv_kernel,
        grid=(depth, bsz),
        in_specs=[pl.BlockSpec((1, N_MEM, D_MODEL), lambda l, b: (b, 0, 0)), w_spec, w_spec],
        out_specs=[out5, out5, out4, out4],
        out_shape=[shape5, shape5, shape4, shape4],
        compiler_params=_params(),
        name="memory_kv",
    )(mem, w_k, w_v)


def _block_diag(blocks):
    n, r, c = blocks.shape[-3:]
    eye = jnp.eye(n, dtype=blocks.dtype)
    return (blocks[..., :, :, None, :] * eye[:, None, :, None]).reshape(blocks.shape[:-3] + (n * r, n * c))


def _tile_times(tt):
    i = jnp.arange(tt)
    return (i % SUBLANES) * (tt // SUBLANES) + i // SUBLANES


def _to_segments(x, tt):
    bsz, seq, c = x.shape
    return x.reshape(bsz, seq // tt, SUBLANES, tt // SUBLANES, c).swapaxes(2, 3).reshape(bsz, seq, c)


def _from_segments(x, tt):
    bsz, seq, c = x.shape
    return x.reshape(bsz, seq // tt, tt // SUBLANES, SUBLANES, c).swapaxes(2, 3).reshape(bsz, seq, c)


def _on_sublanes(state):
    return jnp.broadcast_to(state[..., None, :], state.shape[:-1] + (SUBLANES, state.shape[-1]))


def _shared_params(g_mix, w_in, conv_a_w, conv_a_b, w_rg, b_rg, w_ig, b_ig, lam, conv_b_w, g_v, w_out,
                   g_x, w_q, w_o, g_ffn, w_up, conv_f_w, w_down, g_final):
    per_half = A_HEADS // 2
    gates = [jnp.concatenate([_block_diag(w_rg[:, j * per_half:(j + 1) * per_half]),
                              _block_diag(w_ig[:, j * per_half:(j + 1) * per_half])], axis=-1)
             for j in range(2)]
    group = jnp.arange(C_WIDTH) // C_HD

    def row(a):
        return a[:, None, :]

    def taps(a):
        return a[:, :, None, :]

    return dict(
        g_mix=row(g_mix), w_in=w_in.astype(BF16), conv_a_w=taps(conv_a_w), conv_a_b=row(conv_a_b),
        w_gates=jnp.stack(gates, axis=1).astype(BF16), b_rg=row(b_rg), b_ig=row(b_ig), lam=row(lam),
        conv_b_w=taps(conv_b_w), g_v=row(g_v),
        group_mean=((group[:, None] == group[None, :]).astype(F32) / C_HD).astype(BF16),
        w_out=w_out.astype(BF16), g_x=row(g_x), w_q=w_q.astype(BF16), w_o=w_o.astype(BF16),
        g_ffn=row(g_ffn), w_up=w_up.astype(BF16), conv_f_w=taps(conv_f_w), w_down=w_down.astype(BF16),
        g_final=g_final[None])


def _tile_params(tt, chunk, w_s, b_s):
    pos = _tile_times(tt) % chunk
    return dict(w_s=w_s[:, :, pos][:, :, :, pos],
                b_s_full=jnp.repeat(jnp.swapaxes(b_s[:, :, pos], 1, 2), C_HD, axis=2))


def _tiles(bsz, seq):
    if seq <= MLP_CHUNK:
        return bsz, seq
    return PROMPT_BATCH_TILE, PROMPT_TIME_TILE


def _trunk(x, kb, vb, states, p, emit_vc):
    bsz, seq, _ = x.shape
    nb, tt = _tiles(bsz, seq)
    chunk = min(seq, MLP_CHUNK)
    x = _to_segments(x, tt)
    new_states = [[] for _ in range(4)]
    vcs = []
    for l in range(DEPTH):
        ca0, h0, cb0, cf0 = (s[l] for s in states)
        mix = _mixer_call(x, ca0, h0, cb0, p, l, nb, tt, chunk, emit_vc)
        x = _attn_call(mix[0], kb, vb, p, l, nb, tt)
        x, cf = _ffn_call(x, cf0, p, l, nb, tt, l == DEPTH - 1)
        for acc, val in zip(new_states, (mix[1], mix[2], mix[3], cf)):
            acc.append(val)
        if emit_vc:
            vcs.append(mix[4])
    new_states = tuple(jnp.stack(s)[..., SUBLANES - 1, :] for s in new_states)
    vc = _from_segments(jnp.concatenate(vcs, axis=0), tt).reshape(DEPTH, bsz, seq, C_WIDTH) if emit_vc else None
    return _from_segments(x, tt), new_states, vc


def kernel(x_prompt, x_sample, mem_prompt, cache_mem_k, cache_mem_v, state_conv_a, state_h_a, state_conv_b, state_conv_ffn, g_mix, w_in, conv_a_w, conv_a_b, w_rg, b_rg, w_ig, b_ig, lam, conv_b_w, g_v, w_s, b_s, w_out, g_x, w_q, w_k, w_v, w_o, g_ffn, w_up, conv_f_w, w_down, g_final):
    bp, seq_p, _ = x_prompt.shape
    bs, seq_s, _ = x_sample.shape
    shared = _shared_params(g_mix, w_in, conv_a_w, conv_a_b, w_rg, b_rg, w_ig, b_ig, lam, conv_b_w, g_v,
                            w_out, g_x, w_q, w_o, g_ffn, w_up, conv_f_w, w_down, g_final)

    def tile_params(bsz, seq):
        return dict(shared, **_tile_params(_tiles(bsz, seq)[1], min(seq, MLP_CHUNK), w_s, b_s))

    mem_k, mem_v, kb, vb = _kv_call(mem_prompt, w_k.astype(BF16), w_v.astype(BF16))
    zero_states = tuple(jnp.zeros((DEPTH, bp) + shape + (SUBLANES, width), F32)
                        for shape, width in (((A_CONV - 1,), A_WIDTH), ((), A_WIDTH),
                                             ((B_CONV - 1,), B_WIDTH), ((FFN_CONV - 1,), D_FF)))
    y_p, states_p, _ = _trunk(x_prompt, kb, vb, zero_states, tile_params(bp, seq_p), False)
    carried = tuple(_on_sublanes(s) for s in (state_conv_a, state_h_a, state_conv_b, state_conv_ffn))
    y_s, states_s, vc_s = _trunk(
        x_sample, cache_mem_k.reshape(DEPTH, bs, N_MEM, D_MODEL).astype(BF16),
        cache_mem_v.reshape(DEPTH, bs, N_MEM, D_MODEL).astype(BF16), carried, tile_params(bs, seq_s), True)
    return (y_p, y_s) + states_p + (mem_k, mem_v) + states_s + (vc_s,)
```

```python
import functools
import math

import jax
import jax.numpy as jnp
from jax import lax
from jax.experimental import pallas as pl
from jax.experimental.pallas import tpu as pltpu

D_MODEL = 1024
DEPTH = 2
A_WIDTH = 512
A_HEADS = 8
A_HD = 64
A_CONV = 4
RG_C = 8.0
B_WIDTH = 256
B_CONV = 3
C_WIDTH = 256
C_HEADS = 4
C_HD = 64
MLP_CHUNK = 128
N_MEM = 256
X_HEADS = 4
X_HD = 256
D_FF = 2816
FFN_CONV = 3
EPS = 1e-6
IN_COLS = 2 * A_WIDTH + 3 * B_WIDTH + 2 * C_WIDTH

SUBLANES = 8
LANES = 128
PROMPT_TIME_TILE = 256
PROMPT_BATCH_TILE = 4
VMEM_LIMIT_BYTES = 56 * 1024 * 1024

LOG2_E = math.log2(math.e)
GELU_C0 = math.sqrt(2.0 / math.pi)
GELU_C1 = 0.044715

F32 = jnp.float32
BF16 = jnp.bfloat16


def _dot(a, b):
    return jnp.dot(a, b, preferred_element_type=F32)


def _rms(x, g):
    ms = jnp.mean(x * x, axis=-1, keepdims=True)
    return x * lax.rsqrt(ms + EPS) * g


def _gelu(x):
    half_x = 0.5 * x
    inner = (x * x * (GELU_C0 * GELU_C1) + GELU_C0) * x
    return half_x * jnp.tanh(inner) + half_x


def _sublane_iota(shape):
    return lax.broadcasted_iota(jnp.int32, shape, len(shape) - 2)


def _causal_taps(cur, tail_ref, w):
    n_prev = tail_ref.shape[1]
    seg = cur.shape[1]
    last = cur[:, seg - n_prev:]
    wrapped = pltpu.roll(jnp.where(_sublane_iota(last.shape) == SUBLANES - 1, tail_ref[...], last), 1, 2)
    tail_ref[...] = last
    xp = jnp.concatenate([wrapped, cur], axis=1)
    acc = xp[:, 0:seg] * w[0]
    for j in range(1, n_prev + 1):
        acc = acc + xp[:, j:j + seg] * w[j]
    return acc


def _segment_scan(a, b, h_ref):
    seg = a.shape[1]
    local, decay = b[:, 0], a[:, 0]
    locals_, decays = [local], [decay]
    for s in range(1, seg):
        local = a[:, s] * local + b[:, s]
        decay = a[:, s] * decay
        locals_.append(local)
        decays.append(decay)
    h_in = h_ref[:, SUBLANES - 1:SUBLANES, :]
    row = _sublane_iota(local.shape)
    seg_a, seg_b = decay, local
    d = 1
    while d < SUBLANES:
        a_prev = pltpu.roll(seg_a, d, 1)
        b_prev = pltpu.roll(seg_b, d, 1)
        keep = row >= d
        seg_b = jnp.where(keep, seg_a * b_prev + seg_b, seg_b)
        seg_a = jnp.where(keep, seg_a * a_prev, seg_a)
        d *= 2
    h_end = seg_a * h_in + seg_b
    h_ref[...] = h_end
    h_start = jnp.where(row == 0, h_in, pltpu.roll(h_end, 1, 1))
    return jnp.stack([locals_[s] + decays[s] * h_start for s in range(seg)], axis=1)


def _mixer_kernel(x_ref, ca0_ref, h0_ref, cb0_ref, gmix_ref, win_ref, caw_ref, cab_ref, wg_ref,
                  brg_ref, big_ref, lam_ref, cbw_ref, gv_ref, gsum_ref, ws_ref, bsf_ref, wout_ref,
                  xo_ref, cao_ref, ho_ref, cbo_ref, *rest, nb, gb, tt, chunk, emit_vc):
    if emit_vc:
        vco_ref, ca_tail, cb_tail, h_state, wmix, ycat = rest
    else:
        ca_tail, cb_tail, h_state, wmix, ycat = rest
    rows = gb * tt
    seg = tt // SUBLANES
    groups = nb // gb

    @pl.when(pl.program_id(1) == 0)
    def _():
        ca_tail[...] = ca0_ref[...]
        cb_tail[...] = cb0_ref[...]
        h_state[...] = h0_ref[...]
        ti = lax.broadcasted_iota(jnp.int32, (tt, tt), 0)
        tj = lax.broadcasted_iota(jnp.int32, (tt, tt), 1)
        time_i = (ti % SUBLANES) * seg + ti // SUBLANES
        time_j = (tj % SUBLANES) * seg + tj // SUBLANES
        allowed = (time_i >= time_j) & (time_i // chunk == time_j // chunk)
        for hd in range(C_HEADS):
            wmix[hd] = jnp.where(allowed, ws_ref[hd], 0.0).astype(BF16)

    def project_in(g):
        x = x_ref[g * gb:(g + 1) * gb].reshape(rows, D_MODEL)
        return _dot(_rms(x, gmix_ref[...]).astype(BF16), win_ref[...])

    col = {}
    o = 0
    for name, width in (("xa", A_WIDTH), ("ga", A_WIDTH), ("xb", B_WIDTH), ("gb", B_WIDTH),
                        ("gc", B_WIDTH), ("uc", C_WIDTH), ("vc", C_WIDTH)):
        col[name] = slice(o, o + width)
        o += width
    half = A_WIDTH // 2

    def mix_matmuls(g, z):
        seqs = slice(g * gb, (g + 1) * gb)
        xc = _causal_taps(z[:, col["xa"]].reshape(gb, seg, SUBLANES, A_WIDTH), ca_tail.at[seqs], caw_ref[...])
        xc2 = (xc + cab_ref[...]).reshape(rows, A_WIDTH)
        xcb = xc2.astype(BF16)
        gz0 = _dot(xcb[:, :half], wg_ref[0])
        gz1 = _dot(xcb[:, half:], wg_ref[1])
        v = _gelu(z[:, col["vc"]])
        sq = v * v
        sq_hi = sq.astype(BF16)
        sq_lo = (sq - sq_hi.astype(F32)).astype(BF16)
        ms = _dot(sq_hi, gsum_ref[...]) + _dot(sq_lo, gsum_ref[...])
        vn = v * lax.rsqrt(ms + EPS) * gv_ref[...]
        if emit_vc:
            vco_ref[seqs] = vn.reshape(gb, tt, C_WIDTH)
        vnb = vn.astype(BF16)
        first_head = lax.broadcasted_iota(jnp.int32, (tt, LANES), 1) < C_HD
        mixed = []
        for b in range(gb):
            pieces = []
            for pair in range(C_WIDTH // LANES):
                vp = vnb[b * tt:(b + 1) * tt, pair * LANES:(pair + 1) * LANES]
                pieces.append(jnp.where(first_head, _dot(wmix[2 * pair], vp), _dot(wmix[2 * pair + 1], vp)))
            mixed.append(jnp.concatenate(pieces, axis=1) + bsf_ref[...])
        return xc2, gz0, gz1, jnp.concatenate(mixed, axis=0)

    def mix_elementwise(g, z, xc2, gz0, gz1, mixed):
        seqs = slice(g * gb, (g + 1) * gb)
        y = ycat.at[g * rows:(g + 1) * rows]
        r = jax.nn.sigmoid(jnp.concatenate([gz0[:, :half], gz1[:, :half]], axis=1) + brg_ref[...])
        gi = jax.nn.sigmoid(jnp.concatenate([gz0[:, half:], gz1[:, half:]], axis=1) + big_ref[...])
        neg_lam = -lam_ref[...]
        softplus = jnp.maximum(neg_lam, 0.0) + jnp.log1p(jnp.exp(-jnp.abs(neg_lam)))
        decay_rate = RG_C * softplus
        a = jnp.exp2((-LOG2_E) * decay_rate * r)
        one_minus_a2 = (1.0 + a * a) * jnp.tanh(decay_rate * r)
        bterm = jnp.sqrt(one_minus_a2) * (gi * xc2)
        h = _segment_scan(a.reshape(gb, seg, SUBLANES, A_WIDTH),
                          bterm.reshape(gb, seg, SUBLANES, A_WIDTH), h_state.at[seqs])
        y[:, 0:A_WIDTH] = (_gelu(z[:, col["ga"]]) * h.reshape(rows, A_WIDTH)).astype(BF16)
        zb = _causal_taps((z[:, col["gc"]] * z[:, col["xb"]]).reshape(gb, seg, SUBLANES, B_WIDTH),
                          cb_tail.at[seqs], cbw_ref[...])
        y[:, A_WIDTH:A_WIDTH + B_WIDTH] = (z[:, col["gb"]] * zb.reshape(rows, B_WIDTH)).astype(BF16)
        y[:, A_WIDTH + B_WIDTH:] = (_gelu(z[:, col["uc"]]) * mixed).astype(BF16)

    def project_out(g):
        x = x_ref[g * gb:(g + 1) * gb].reshape(rows, D_MODEL)
        out = x + _dot(ycat[g * rows:(g + 1) * rows], wout_ref[...])
        xo_ref[g * gb:(g + 1) * gb] = out.reshape(gb, tt, D_MODEL)

    zs = {g: project_in(g) for g in range(min(2, groups))}
    small = {0: mix_matmuls(0, zs[0])}
    for g in range(groups):
        if g + 2 < groups:
            zs[g + 2] = project_in(g + 2)
        mix_elementwise(g, zs.pop(g), *small.pop(g))
        if g + 1 < groups:
            small[g + 1] = mix_matmuls(g + 1, zs[g + 1])
        project_out(g)
    cao_ref[...] = ca_tail[...]
    ho_ref[...] = h_state[...]
    cbo_ref[...] = cb_tail[...]


def _attn_kernel(x_ref, kb, vb, gx_ref, wq_ref, wo_ref, xo_ref, ocat, *, nb, tt):
    rows = nb * tt
    x = x_ref[...].reshape(rows, D_MODEL)
    xn = _rms(x, gx_ref[...]).astype(BF16)
    q = (_dot(xn, wq_ref[...]) * (X_HD ** -0.5)).astype(BF16)
    for b in range(nb):
        for hd in range(X_HEADS):
            cols = slice(hd * X_HD, (hd + 1) * X_HD)
            qh = q[b * tt:(b + 1) * tt, cols]
            s = lax.dot_general(qh, kb[b, :, cols], (((1,), (1,)), ((), ())),
                                preferred_element_type=F32)
            e = jnp.exp(s - jnp.max(s, axis=-1, keepdims=True))
            denom = jnp.sum(e, axis=-1, keepdims=True)
            oh = _dot(e.astype(BF16), vb[b, :, cols]) / denom
            ocat[b * tt:(b + 1) * tt, cols] = oh.astype(BF16)
    out = x + _dot(ocat[...], wo_ref[...])
    xo_ref[...] = out.reshape(nb, tt, D_MODEL)


def _ffn_kernel(x_ref, cf0_ref, gffn_ref, wup_ref, cfw_ref, wdown_ref, gfin_ref, xo_ref, cfo_ref,
                cf_tail, *, nb, tt, final_norm):
    rows = nb * tt
    seg = tt // SUBLANES

    @pl.when(pl.program_id(1) == 0)
    def _():
        cf_tail[...] = cf0_ref[...]

    x = x_ref[...].reshape(rows, D_MODEL)
    xn = _rms(x, gffn_ref[...]).astype(BF16)
    gu = _dot(xn, wup_ref[...])
    gconv = _causal_taps(gu[:, :D_FF].reshape(nb, seg, SUBLANES, D_FF), cf_tail, cfw_ref[...])
    cfo_ref[...] = cf_tail[...]
    act = (jax.nn.silu(gconv.reshape(rows, D_FF)) * gu[:, D_FF:]).astype(BF16)
    out = x + _dot(act, wdown_ref[...])
    if final_norm:
        out = _rms(out, gfin_ref[...])
    xo_ref[...] = out.reshape(nb, tt, D_MODEL)


def _kv_kernel(mem_ref, wk_ref, wv_ref, ko_ref, vo_ref, kb_ref, vb_ref):
    m = mem_ref[0].astype(BF16)
    for w_ref, o_ref, b_ref in ((wk_ref, ko_ref, kb_ref), (wv_ref, vo_ref, vb_ref)):
        kv = _dot(m, w_ref[...])
        b_ref[...] = kv.astype(BF16)
        for hd in range(X_HEADS):
            o_ref[:, hd, :] = kv[:, hd * X_HD:(hd + 1) * X_HD]


def _kv_cast_kernel(k_ref, v_ref, kb_ref, vb_ref):
    for src, dst in ((k_ref, kb_ref), (v_ref, vb_ref)):
        for hd in range(X_HEADS):
            dst[:, hd * X_HD:(hd + 1) * X_HD] = src[:, hd, :].astype(BF16)


def _whole():
    return pl.BlockSpec(memory_space=pltpu.MemorySpace.VMEM)


def _of_layer(arr, l):
    zeros = (0,) * (arr.ndim - 1)
    return pl.BlockSpec((None,) + arr.shape[1:], lambda b, t: (l,) + zeros, pipeline_mode=pl.Buffered(1))


def _params():
    return pltpu.CompilerParams(dimension_semantics=("arbitrary", "arbitrary"),
                                vmem_limit_bytes=VMEM_LIMIT_BYTES)


def _tile_specs(nb, tt):
    x_spec = pl.BlockSpec((nb, tt, D_MODEL), lambda b, t: (b, t, 0))

    def state_spec(n, width):
        return pl.BlockSpec((nb, n, SUBLANES, width), lambda b, t: (b, 0, 0, 0))

    return x_spec, state_spec


def _mixer_call(x, ca0, h0, cb0, p, l, nb, tt, chunk, emit_vc):
    bsz, seq, _ = x.shape
    layer_params = [p[name] for name in ("g_mix", "w_in", "conv_a_w", "conv_a_b", "w_gates", "b_rg", "b_ig",
                                         "lam", "conv_b_w", "g_v")]
    mix_params = [p[name] for name in ("w_s", "b_s_full", "w_out")]
    x_spec, state_spec = _tile_specs(nb, tt)
    h_spec = pl.BlockSpec((nb, SUBLANES, A_WIDTH), lambda b, t: (b, 0, 0))
    out_shape = [jax.ShapeDtypeStruct(x.shape, F32),
                 jax.ShapeDtypeStruct(ca0.shape, F32),
                 jax.ShapeDtypeStruct(h0.shape, F32),
                 jax.ShapeDtypeStruct(cb0.shape, F32)]
    out_specs = [x_spec, state_spec(A_CONV - 1, A_WIDTH), h_spec, state_spec(B_CONV - 1, B_WIDTH)]
    if emit_vc:
        out_shape.append(jax.ShapeDtypeStruct((bsz, seq, C_WIDTH), F32))
        out_specs.append(pl.BlockSpec((nb, tt, C_WIDTH), lambda b, t: (b, t, 0)))
    gb = nb if seq == tt else 1
    kern = functools.partial(_mixer_kernel, nb=nb, gb=gb, tt=tt, chunk=chunk, emit_vc=emit_vc)
    return pl.pallas_call(
        kern,
        grid=(bsz // nb, seq // tt),
        in_specs=([x_spec, state_spec(A_CONV - 1, A_WIDTH), h_spec, state_spec(B_CONV - 1, B_WIDTH)]
                  + [_of_layer(a, l) for a in layer_params] + [_whole()]
                  + [_of_layer(a, l) for a in mix_params]),
        out_specs=out_specs,
        out_shape=out_shape,
        scratch_shapes=[pltpu.VMEM((nb, A_CONV - 1, SUBLANES, A_WIDTH), F32),
                        pltpu.VMEM((nb, B_CONV - 1, SUBLANES, B_WIDTH), F32),
                        pltpu.VMEM((nb, SUBLANES, A_WIDTH), F32),
                        pltpu.VMEM((C_HEADS, tt, tt), BF16),
                        pltpu.VMEM((nb * tt, D_MODEL), BF16)],
        compiler_params=_params(),
        name="mixer",
    )(x, ca0, h0, cb0, *layer_params, p["group_mean"], *mix_params)


def _attn_call(x, kb, vb, p, l, nb, tt):
    bsz, seq, _ = x.shape
    x_spec, _ = _tile_specs(nb, tt)
    kv_spec = pl.BlockSpec((None, nb, N_MEM, D_MODEL), lambda b, t: (l, b, 0, 0))
    weights = [p["g_x"], p["w_q"], p["w_o"]]
    kern = functools.partial(_attn_kernel, nb=nb, tt=tt)
    return pl.pallas_call(
        kern,
        grid=(bsz // nb, seq // tt),
        in_specs=[x_spec, kv_spec, kv_spec] + [_of_layer(a, l) for a in weights],
        out_specs=x_spec,
        out_shape=jax.ShapeDtypeStruct(x.shape, F32),
        scratch_shapes=[pltpu.VMEM((nb * tt, D_MODEL), BF16)],
        compiler_params=_params(),
        name="attn",
    )(x, kb, vb, *weights)


def _ffn_call(x, cf0, p, l, nb, tt, final_norm):
    bsz, seq, _ = x.shape
    x_spec, state_spec = _tile_specs(nb, tt)
    weights = [p["g_ffn"], p["w_up"], p["conv_f_w"], p["w_down"]]
    kern = functools.partial(_ffn_kernel, nb=nb, tt=tt, final_norm=final_norm)
    return pl.pallas_call(
        kern,
        grid=(bsz // nb, seq // tt),
        in_specs=[x_spec, state_spec(FFN_CONV - 1, D_FF)] + [_of_layer(a, l) for a in weights] + [_whole()],
        out_specs=[x_spec, state_spec(FFN_CONV - 1, D_FF)],
        out_shape=[jax.ShapeDtypeStruct(x.shape, F32), jax.ShapeDtypeStruct(cf0.shape, F32)],
        scratch_shapes=[pltpu.VMEM((nb, FFN_CONV - 1, SUBLANES, D_FF), F32)],
        compiler_params=_params(),
        name="ffn",
    )(x, cf0, *weights, p["g_final"])


def _kv_call(mem, w_k, w_v):
    bsz = mem.shape[0]
    depth = w_k.shape[0]
    w_spec = pl.BlockSpec((None, D_MODEL, D_MODEL), lambda l, b: (l, 0, 0))
    out5 = pl.BlockSpec((None, None, N_MEM, X_HEADS, X_HD), lambda l, b: (l, b, 0, 0, 0))
    out4 = pl.BlockSpec((None, None, N_MEM, D_MODEL), lambda l, b: (l, b, 0, 0))
    shape5 = jax.ShapeDtypeStruct((depth, bsz, N_MEM, X_HEADS, X_HD), F32)
    shape4 = jax.ShapeDtypeStruct((depth, bsz, N_MEM, D_MODEL), BF16)
    return pl.pallas_call(
        _kv_kernel,
        grid=(depth, bsz),
        in_specs=[pl.BlockSpec((1, N_MEM, D_MODEL), lambda l, b: (b, 0, 0)), w_spec, w_spec],
        out_specs=[out5, out5, out4, out4],
        out_shape=[shape5, shape5, shape4, shape4],
        compiler_params=_params(),
        name="memory_kv",
    )(mem, w_k, w_v)


def _kv_cast_call(cache_k, cache_v):
    depth, bsz = cache_k.shape[:2]
    in_spec = pl.BlockSpec((None, None, N_MEM, X_HEADS, X_HD), lambda l, b: (l, b, 0, 0, 0))
    out_spec = pl.BlockSpec((None, None, N_MEM, D_MODEL), lambda l, b: (l, b, 0, 0))
    shape = jax.ShapeDtypeStruct((depth, bsz, N_MEM, D_MODEL), BF16)
    return pl.pallas_call(
        _kv_cast_kernel,
        grid=(depth, bsz),
        in_specs=[in_spec, in_spec],
        out_specs=[out_spec, out_spec],
        out_shape=[shape, shape],
        compiler_params=_params(),
        name="cache_kv_cast",
    )(cache_k, cache_v)


def _block_diag(blocks):
    n, r, c = blocks.shape[-3:]
    eye = jnp.eye(n, dtype=blocks.dtype)
    return (blocks[..., :, :, None, :] * eye[:, None, :, None]).reshape(blocks.shape[:-3] + (n * r, n * c))


def _tile_times(tt):
    i = jnp.arange(tt)
    return (i % SUBLANES) * (tt // SUBLANES) + i // SUBLANES


def _to_segments(x, tt):
    bsz, seq, c = x.shape
    return x.reshape(bsz, seq // tt, SUBLANES, tt // SUBLANES, c).swapaxes(2, 3).reshape(bsz, seq, c)


def _from_segments(x, tt):
    bsz, seq, c = x.shape
    return x.reshape(bsz, seq // tt, tt // SUBLANES, SUBLANES, c).swapaxes(2, 3).reshape(bsz, seq, c)


def _on_sublanes(state):
    return jnp.broadcast_to(state[..., None, :], state.shape[:-1] + (SUBLANES, state.shape[-1]))


def _shared_params(g_mix, w_in, conv_a_w, conv_a_b, w_rg, b_rg, w_ig, b_ig, lam, conv_b_w, g_v, w_out,
                   g_x, w_q, w_o, g_ffn, w_up, conv_f_w, w_down, g_final):
    per_half = A_HEADS // 2
    gates = [jnp.concatenate([_block_diag(w_rg[:, j * per_half:(j + 1) * per_half]),
                              _block_diag(w_ig[:, j * per_half:(j + 1) * per_half])], axis=-1)
             for j in range(2)]
    group = jnp.arange(C_WIDTH) // C_HD

    def row(a):
        return a[:, None, :]

    def taps(a):
        return a[:, :, None, :]

    return dict(
        g_mix=row(g_mix), w_in=w_in.astype(BF16), conv_a_w=taps(conv_a_w), conv_a_b=row(conv_a_b),
        w_gates=jnp.stack(gates, axis=1).astype(BF16), b_rg=row(b_rg), b_ig=row(b_ig), lam=row(lam),
        conv_b_w=taps(conv_b_w), g_v=row(g_v),
        group_mean=((group[:, None] == group[None, :]).astype(F32) / C_HD).astype(BF16),
        w_out=w_out.astype(BF16), g_x=row(g_x), w_q=w_q.astype(BF16), w_o=w_o.astype(BF16),
        g_ffn=row(g_ffn), w_up=w_up.astype(BF16), conv_f_w=taps(conv_f_w), w_down=w_down.astype(BF16),
        g_final=g_final[None])


def _tile_params(tt, chunk, w_s, b_s):
    pos = _tile_times(tt) % chunk
    return dict(w_s=w_s[:, :, pos][:, :, :, pos],
                b_s_full=jnp.repeat(jnp.swapaxes(b_s[:, :, pos], 1, 2), C_HD, axis=2))


def _tiles(bsz, seq):
    if seq <= MLP_CHUNK:
        return bsz, seq
    return PROMPT_BATCH_TILE, PROMPT_TIME_TILE


def _trunk(x, kb, vb, states, p, emit_vc):
    bsz, seq, _ = x.shape
    nb, tt = _tiles(bsz, seq)
    chunk = min(seq, MLP_CHUNK)
    x = _to_segments(x, tt)
    new_states = [[] for _ in range(4)]
    vcs = []
    for l in range(DEPTH):
        ca0, h0, cb0, cf0 = (s[l] for s in states)
        mix = _mixer_call(x, ca0, h0, cb0, p, l, nb, tt, chunk, emit_vc)
        x = _attn_call(mix[0], kb, vb, p, l, nb, tt)
        x, cf = _ffn_call(x, cf0, p, l, nb, tt, l == DEPTH - 1)
        for acc, val in zip(new_states, (mix[1], mix[2], mix[3], cf)):
            acc.append(val)
        if emit_vc:
            vcs.append(mix[4])
    new_states = tuple(jnp.stack(s)[..., SUBLANES - 1, :] for s in new_states)
    vc = _from_segments(jnp.concatenate(vcs, axis=0), tt).reshape(DEPTH, bsz, seq, C_WIDTH) if emit_vc else None
    return _from_segments(x, tt), new_states, vc


def kernel(x_prompt, x_sample, mem_prompt, cache_mem_k, cache_mem_v, state_conv_a, state_h_a, state_conv_b, state_conv_ffn, g_mix, w_in, conv_a_w, conv_a_b, w_rg, b_rg, w_ig, b_ig, lam, conv_b_w, g_v, w_s, b_s, w_out, g_x, w_q, w_k, w_v, w_o, g_ffn, w_up, conv_f_w, w_down, g_final):
    bp, seq_p, _ = x_prompt.shape
    bs, seq_s, _ = x_sample.shape
    shared = _shared_params(g_mix, w_in, conv_a_w, conv_a_b, w_rg, b_rg, w_ig, b_ig, lam, conv_b_w, g_v,
                            w_out, g_x, w_q, w_o, g_ffn, w_up, conv_f_w, w_down, g_final)

    def tile_params(bsz, seq):
        return dict(shared, **_tile_params(_tiles(bsz, seq)[1], min(seq, MLP_CHUNK), w_s, b_s))

    mem_k, mem_v, kb, vb = _kv_call(mem_prompt, w_k.astype(BF16), w_v.astype(BF16))
    zero_states = tuple(jnp.zeros((DEPTH, bp) + shape + (SUBLANES, width), F32)
                        for shape, width in (((A_CONV - 1,), A_WIDTH), ((), A_WIDTH),
                                             ((B_CONV - 1,), B_WIDTH), ((FFN_CONV - 1,), D_FF)))
    y_p, states_p, _ = _trunk(x_prompt, kb, vb, zero_states, tile_params(bp, seq_p), False)
    carried = tuple(_on_sublanes(s) for s in (state_conv_a, state_h_a, state_conv_b, state_conv_ffn))
    cache_kb, cache_vb = _kv_cast_call(cache_mem_k, cache_mem_v)
    y_s, states_s, vc_s = _trunk(x_sample, cache_kb, cache_vb, carried, tile_params(bs, seq_s), True)
    return (y_p, y_s) + states_p + (mem_k, mem_v) + states_s + (vc_s,)
```

```python
import functools
import math

import jax
import jax.numpy as jnp
from jax import lax
from jax.experimental import pallas as pl
from jax.experimental.pallas import tpu as pltpu

D_MODEL = 1024
DEPTH = 2
A_WIDTH = 512
A_HEADS = 8
A_HD = 64
A_CONV = 4
RG_C = 8.0
B_WIDTH = 256
B_CONV = 3
C_WIDTH = 256
C_HEADS = 4
C_HD = 64
MLP_CHUNK = 128
N_MEM = 256
X_HEADS = 4
X_HD = 256
D_FF = 2816
FFN_CONV = 3
EPS = 1e-6
IN_COLS = 2 * A_WIDTH + 3 * B_WIDTH + 2 * C_WIDTH

SUBLANES = 8
LANES = 128
PROMPT_TIME_TILE = 256
PROMPT_BATCH_TILE = 4
Z_SLOTS = 3
PROJ_PARTS = 9
VMEM_LIMIT_BYTES = 56 * 1024 * 1024

LOG2_E = math.log2(math.e)
GELU_C0 = math.sqrt(2.0 / math.pi)
GELU_C1 = 0.044715

F32 = jnp.float32
BF16 = jnp.bfloat16


def _dot(a, b):
    return jnp.dot(a, b, preferred_element_type=F32)


def _rms(x, g):
    ms = jnp.mean(x * x, axis=-1, keepdims=True)
    return x * lax.rsqrt(ms + EPS) * g


def _gelu(x):
    half_x = 0.5 * x
    inner = (x * x * (GELU_C0 * GELU_C1) + GELU_C0) * x
    return half_x * jnp.tanh(inner) + half_x


def _sublane_iota(shape):
    return lax.broadcasted_iota(jnp.int32, shape, len(shape) - 2)


def _causal_taps(cur, tail_ref, w):
    n_prev = tail_ref.shape[1]
    seg = cur.shape[1]
    last = cur[:, seg - n_prev:]
    wrapped = pltpu.roll(jnp.where(_sublane_iota(last.shape) == SUBLANES - 1, tail_ref[...], last), 1, 2)
    tail_ref[...] = last
    xp = jnp.concatenate([wrapped, cur], axis=1)
    acc = xp[:, 0:seg] * w[0]
    for j in range(1, n_prev + 1):
        acc = acc + xp[:, j:j + seg] * w[j]
    return acc


def _segment_scan(a, b, h_ref):
    seg = a.shape[1]
    local, decay = b[:, 0], a[:, 0]
    locals_, decays = [local], [decay]
    for s in range(1, seg):
        local = a[:, s] * local + b[:, s]
        decay = a[:, s] * decay
        locals_.append(local)
        decays.append(decay)
    h_in = h_ref[:, SUBLANES - 1:SUBLANES, :]
    row = _sublane_iota(local.shape)
    seg_a, seg_b = decay, local
    d = 1
    while d < SUBLANES:
        a_prev = pltpu.roll(seg_a, d, 1)
        b_prev = pltpu.roll(seg_b, d, 1)
        keep = row >= d
        seg_b = jnp.where(keep, seg_a * b_prev + seg_b, seg_b)
        seg_a = jnp.where(keep, seg_a * a_prev, seg_a)
        d *= 2
    h_end = seg_a * h_in + seg_b
    h_ref[...] = h_end
    h_start = jnp.where(row == 0, h_in, pltpu.roll(h_end, 1, 1))
    return jnp.stack([locals_[s] + decays[s] * h_start for s in range(seg)], axis=1)


def _mixer_kernel(x_ref, ca0_ref, h0_ref, cb0_ref, gmix_ref, win_ref, caw_ref, cab_ref, wg_ref,
                  brg_ref, big_ref, lam_ref, cbw_ref, gv_ref, gsum_ref, ws_ref, bsf_ref, wout_ref,
                  xo_ref, cao_ref, ho_ref, cbo_ref, *rest, nb, gb, tt, chunk, emit_vc):
    if emit_vc:
        vco_ref, ca_tail, cb_tail, h_state, wmix, ycat, zbuf = rest
    else:
        ca_tail, cb_tail, h_state, wmix, ycat, zbuf = rest
    rows = gb * tt
    seg = tt // SUBLANES
    groups = nb // gb

    @pl.when(pl.program_id(1) == 0)
    def _():
        ca_tail[...] = ca0_ref[...]
        cb_tail[...] = cb0_ref[...]
        h_state[...] = h0_ref[...]
        ti = lax.broadcasted_iota(jnp.int32, (tt, tt), 0)
        tj = lax.broadcasted_iota(jnp.int32, (tt, tt), 1)
        time_i = (ti % SUBLANES) * seg + ti // SUBLANES
        time_j = (tj % SUBLANES) * seg + tj // SUBLANES
        allowed = (time_i >= time_j) & (time_i // chunk == time_j // chunk)
        for hd in range(C_HEADS):
            wmix[hd] = jnp.where(allowed, ws_ref[hd], 0.0).astype(BF16)

    def project_in(g):
        x = x_ref[g * gb:(g + 1) * gb].reshape(rows, D_MODEL)
        xn = _rms(x, gmix_ref[...]).astype(BF16)
        z = zbuf.at[g % Z_SLOTS]
        width = IN_COLS // PROJ_PARTS

        def part(c):
            z[:, c * width:(c + 1) * width] = _dot(xn, win_ref[:, c * width:(c + 1) * width])

        return z, [functools.partial(part, c) for c in range(PROJ_PARTS)]

    col = {}
    o = 0
    for name, width in (("xa", A_WIDTH), ("ga", A_WIDTH), ("xb", B_WIDTH), ("gb", B_WIDTH),
                        ("gc", B_WIDTH), ("uc", C_WIDTH), ("vc", C_WIDTH)):
        col[name] = slice(o, o + width)
        o += width
    half = A_WIDTH // 2

    def mix_matmuls(g, z):
        seqs = slice(g * gb, (g + 1) * gb)
        out = {}

        def gates_and_norm():
            xc = _causal_taps(z[:, col["xa"]].reshape(gb, seg, SUBLANES, A_WIDTH), ca_tail.at[seqs],
                              caw_ref[...])
            xc2 = (xc + cab_ref[...]).reshape(rows, A_WIDTH)
            xcb = xc2.astype(BF16)
            out["xc2"] = xc2
            out["gz0"] = _dot(xcb[:, :half], wg_ref[0])
            out["gz1"] = _dot(xcb[:, half:], wg_ref[1])
            v = _gelu(z[:, col["vc"]])
            sq = v * v
            sq_hi = sq.astype(BF16)
            sq_lo = (sq - sq_hi.astype(F32)).astype(BF16)
            out["v"] = v
            out["ms"] = _dot(sq_hi, gsum_ref[...]) + _dot(sq_lo, gsum_ref[...])

        def spatial_mix():
            vn = out.pop("v") * lax.rsqrt(out.pop("ms") + EPS) * gv_ref[...]
            if emit_vc:
                vco_ref[seqs] = vn.reshape(gb, tt, C_WIDTH)
            vnb = vn.astype(BF16)
            first_head = lax.broadcasted_iota(jnp.int32, (tt, LANES), 1) < C_HD
            mixed = []
            for b in range(gb):
                pieces = []
                for pair in range(C_WIDTH // LANES):
                    vp = vnb[b * tt:(b + 1) * tt, pair * LANES:(pair + 1) * LANES]
                    pieces.append(jnp.where(first_head, _dot(wmix[2 * pair], vp), _dot(wmix[2 * pair + 1], vp)))
                mixed.append(jnp.concatenate(pieces, axis=1) + bsf_ref[...])
            out["mixed"] = jnp.concatenate(mixed, axis=0)

        return out, [gates_and_norm, spatial_mix]

    def mix_elementwise(g, z, xc2, gz0, gz1, mixed):
        seqs = slice(g * gb, (g + 1) * gb)
        y = ycat.at[g * rows:(g + 1) * rows]
        r = jax.nn.sigmoid(jnp.concatenate([gz0[:, :half], gz1[:, :half]], axis=1) + brg_ref[...])
        gi = jax.nn.sigmoid(jnp.concatenate([gz0[:, half:], gz1[:, half:]], axis=1) + big_ref[...])
        neg_lam = -lam_ref[...]
        softplus = jnp.maximum(neg_lam, 0.0) + jnp.log1p(jnp.exp(-jnp.abs(neg_lam)))
        decay_rate = RG_C * softplus
        a = jnp.exp2((-LOG2_E) * decay_rate * r)
        one_minus_a2 = (1.0 + a * a) * jnp.tanh(decay_rate * r)
        bterm = jnp.sqrt(one_minus_a2) * (gi * xc2)
        h = _segment_scan(a.reshape(gb, seg, SUBLANES, A_WIDTH),
                          bterm.reshape(gb, seg, SUBLANES, A_WIDTH), h_state.at[seqs])
        y[:, 0:A_WIDTH] = (_gelu(z[:, col["ga"]]) * h.reshape(rows, A_WIDTH)).astype(BF16)
        zb = _causal_taps((z[:, col["gc"]] * z[:, col["xb"]]).reshape(gb, seg, SUBLANES, B_WIDTH),
                          cb_tail.at[seqs], cbw_ref[...])
        y[:, A_WIDTH:A_WIDTH + B_WIDTH] = (z[:, col["gb"]] * zb.reshape(rows, B_WIDTH)).astype(BF16)
        y[:, A_WIDTH + B_WIDTH:] = (_gelu(z[:, col["uc"]]) * mixed).astype(BF16)

    def project_out(g):
        x = x_ref[g * gb:(g + 1) * gb].reshape(rows, D_MODEL)
        out = x + _dot(ycat[g * rows:(g + 1) * rows], wout_ref[...])
        xo_ref[g * gb:(g + 1) * gb] = out.reshape(gb, tt, D_MODEL)

    def interleave(big_steps, small_steps):
        stride = max(1, len(big_steps) // max(1, len(small_steps)))
        pending = list(small_steps)
        for i, step in enumerate(big_steps):
            step()
            if pending and (i + 1) % stride == 0:
                pending.pop(0)()
        for step in pending:
            step()

    zs, small = {}, {}
    zs[0], parts = project_in(0)
    interleave(parts, [])
    small[0], steps = mix_matmuls(0, zs[0])
    if groups > 1:
        zs[1], parts = project_in(1)
        interleave(parts, steps)
    else:
        interleave([], steps)
    for g in range(groups):
        parts, steps = [], []
        if g + 2 < groups:
            zs[g + 2], parts = project_in(g + 2)
        if g + 1 < groups:
            small[g + 1], steps = mix_matmuls(g + 1, zs[g + 1])
        interleave(parts, steps)
        res = small.pop(g)
        mix_elementwise(g, zs.pop(g), res["xc2"], res["gz0"], res["gz1"], res["mixed"])
        project_out(g)
    cao_ref[...] = ca_tail[...]
    ho_ref[...] = h_state[...]
    cbo_ref[...] = cb_tail[...]


def _attn_kernel(x_ref, kb, vb, gx_ref, wq_ref, wo_ref, xo_ref, ocat, *, nb, gb, tt):
    rows = gb * tt
    groups = nb // gb

    def project_q(g):
        x = x_ref[g * gb:(g + 1) * gb].reshape(rows, D_MODEL)
        xn = _rms(x, gx_ref[...]).astype(BF16)
        return (_dot(xn, wq_ref[...]) * (X_HD ** -0.5)).astype(BF16)

    def attend(g, q):
        for b in range(gb):
            seq = g * gb + b
            for hd in range(X_HEADS):
                cols = slice(hd * X_HD, (hd + 1) * X_HD)
                s = lax.dot_general(q[b * tt:(b + 1) * tt, cols], kb[seq, :, cols], (((1,), (1,)), ((), ())),
                                    preferred_element_type=F32)
                e = jnp.exp(s - jnp.max(s, axis=-1, keepdims=True))
                denom = jnp.sum(e, axis=-1, keepdims=True)
                oh = _dot(e.astype(BF16), vb[seq, :, cols]) / denom
                ocat[seq * tt:(seq + 1) * tt, cols] = oh.astype(BF16)

    def project_out(g):
        x = x_ref[g * gb:(g + 1) * gb].reshape(rows, D_MODEL)
        out = x + _dot(ocat[g * rows:(g + 1) * rows], wo_ref[...])
        xo_ref[g * gb:(g + 1) * gb] = out.reshape(gb, tt, D_MODEL)

    q = project_q(0)
    for g in range(groups):
        q_next = project_q(g + 1) if g + 1 < groups else None
        attend(g, q)
        project_out(g)
        q = q_next


def _ffn_kernel(x_ref, cf0_ref, gffn_ref, wup_ref, cfw_ref, wdown_ref, gfin_ref, xo_ref, cfo_ref,
                cf_tail, *, nb, tt, final_norm):
    rows = nb * tt
    seg = tt // SUBLANES

    @pl.when(pl.program_id(1) == 0)
    def _():
        cf_tail[...] = cf0_ref[...]

    x = x_ref[...].reshape(rows, D_MODEL)
    xn = _rms(x, gffn_ref[...]).astype(BF16)
    gu = _dot(xn, wup_ref[...])
    gconv = _causal_taps(gu[:, :D_FF].reshape(nb, seg, SUBLANES, D_FF), cf_tail, cfw_ref[...])
    cfo_ref[...] = cf_tail[...]
    act = (jax.nn.silu(gconv.reshape(rows, D_FF)) * gu[:, D_FF:]).astype(BF16)
    out = x + _dot(act, wdown_ref[...])
    if final_norm:
        out = _rms(out, gfin_ref[...])
    xo_ref[...] = out.reshape(nb, tt, D_MODEL)


def _kv_kernel(mem_ref, wk_ref, wv_ref, ko_ref, vo_ref, kb_ref, vb_ref):
    m = mem_ref[0].astype(BF16)
    for w_ref, o_ref, b_ref in ((wk_ref, ko_ref, kb_ref), (wv_ref, vo_ref, vb_ref)):
        kv = _dot(m, w_ref[...])
        b_ref[...] = kv.astype(BF16)
        for hd in range(X_HEADS):
            o_ref[:, hd, :] = kv[:, hd * X_HD:(hd + 1) * X_HD]


def _kv_cast_kernel(k_ref, v_ref, kb_ref, vb_ref):
    for src, dst in ((k_ref, kb_ref), (v_ref, vb_ref)):
        for hd in range(X_HEADS):
            dst[:, hd * X_HD:(hd + 1) * X_HD] = src[:, hd, :].astype(BF16)


def _whole():
    return pl.BlockSpec(memory_space=pltpu.MemorySpace.VMEM)


def _of_layer(arr, l):
    zeros = (0,) * (arr.ndim - 1)
    return pl.BlockSpec((None,) + arr.shape[1:], lambda b, t: (l,) + zeros, pipeline_mode=pl.Buffered(1))


def _params():
    return pltpu.CompilerParams(dimension_semantics=("arbitrary", "arbitrary"),
                                vmem_limit_bytes=VMEM_LIMIT_BYTES)


def _tile_specs(nb, tt):
    x_spec = pl.BlockSpec((nb, tt, D_MODEL), lambda b, t: (b, t, 0))

    def state_spec(n, width):
        return pl.BlockSpec((nb, n, SUBLANES, width), lambda b, t: (b, 0, 0, 0))

    return x_spec, state_spec


def _mixer_call(x, ca0, h0, cb0, p, l, nb, tt, chunk, emit_vc):
    bsz, seq, _ = x.shape
    layer_params = [p[name] for name in ("g_mix", "w_in", "conv_a_w", "conv_a_b", "w_gates", "b_rg", "b_ig",
                                         "lam", "conv_b_w", "g_v")]
    mix_params = [p[name] for name in ("w_s", "b_s_full", "w_out")]
    x_spec, state_spec = _tile_specs(nb, tt)
    h_spec = pl.BlockSpec((nb, SUBLANES, A_WIDTH), lambda b, t: (b, 0, 0))
    out_shape = [jax.ShapeDtypeStruct(x.shape, F32),
                 jax.ShapeDtypeStruct(ca0.shape, F32),
                 jax.ShapeDtypeStruct(h0.shape, F32),
                 jax.ShapeDtypeStruct(cb0.shape, F32)]
    out_specs = [x_spec, state_spec(A_CONV - 1, A_WIDTH), h_spec, state_spec(B_CONV - 1, B_WIDTH)]
    if emit_vc:
        out_shape.append(jax.ShapeDtypeStruct((bsz, seq, C_WIDTH), F32))
        out_specs.append(pl.BlockSpec((nb, tt, C_WIDTH), lambda b, t: (b, t, 0)))
    gb = nb if seq == tt else 1
    kern = functools.partial(_mixer_kernel, nb=nb, gb=gb, tt=tt, chunk=chunk, emit_vc=emit_vc)
    return pl.pallas_call(
        kern,
        grid=(bsz // nb, seq // tt),
        in_specs=([x_spec, state_spec(A_CONV - 1, A_WIDTH), h_spec, state_spec(B_CONV - 1, B_WIDTH)]
                  + [_of_layer(a, l) for a in layer_params] + [_whole()]
                  + [_of_layer(a, l) for a in mix_params]),
        out_specs=out_specs,
        out_shape=out_shape,
        scratch_shapes=[pltpu.VMEM((nb, A_CONV - 1, SUBLANES, A_WIDTH), F32),
                        pltpu.VMEM((nb, B_CONV - 1, SUBLANES, B_WIDTH), F32),
                        pltpu.VMEM((nb, SUBLANES, A_WIDTH), F32),
                        pltpu.VMEM((C_HEADS, tt, tt), BF16),
                        pltpu.VMEM((nb * tt, D_MODEL), BF16),
                        pltpu.VMEM((Z_SLOTS, gb * tt, IN_COLS), F32)],
        compiler_params=_params(),
        name="mixer",
    )(x, ca0, h0, cb0, *layer_params, p["group_mean"], *mix_params)


def _attn_call(x, kb, vb, p, l, nb, tt):
    bsz, seq, _ = x.shape
    x_spec, _ = _tile_specs(nb, tt)
    kv_spec = pl.BlockSpec((None, nb, N_MEM, D_MODEL), lambda b, t: (l, b, 0, 0))
    weights = [p["g_x"], p["w_q"], p["w_o"]]
    gb = nb if seq == tt else 1
    kern = functools.partial(_attn_kernel, nb=nb, gb=gb, tt=tt)
    return pl.pallas_call(
        kern,
        grid=(bsz // nb, seq // tt),
        in_specs=[x_spec, kv_spec, kv_spec] + [_of_layer(a, l) for a in weights],
        out_specs=x_spec,
        out_shape=jax.ShapeDtypeStruct(x.shape, F32),
        scratch_shapes=[pltpu.VMEM((nb * tt, D_MODEL), BF16)],
        compiler_params=_params(),
        name="attn",
    )(x, kb, vb, *weights)


def _ffn_call(x, cf0, p, l, nb, tt, final_norm):
    bsz, seq, _ = x.shape
    x_spec, state_spec = _tile_specs(nb, tt)
    weights = [p["g_ffn"], p["w_up"], p["conv_f_w"], p["w_down"]]
    kern = functools.partial(_ffn_kernel, nb=nb, tt=tt, final_norm=final_norm)
    return pl.pallas_call(
        kern,
        grid=(bsz // nb, seq // tt),
        in_specs=[x_spec, state_spec(FFN_CONV - 1, D_FF)] + [_of_layer(a, l) for a in weights] + [_whole()],
        out_specs=[x_spec, state_spec(FFN_CONV - 1, D_FF)],
        out_shape=[jax.ShapeDtypeStruct(x.shape, F32), jax.ShapeDtypeStruct(cf0.shape, F32)],
        scratch_shapes=[pltpu.VMEM((nb, FFN_CONV - 1, SUBLANES, D_FF), F32)],
        compiler_params=_params(),
        name="ffn",
    )(x, cf0, *weights, p["g_final"])


def _kv_call(mem, w_k, w_v):
    bsz = mem.shape[0]
    depth = w_k.shape[0]
    w_spec = pl.BlockSpec((None, D_MODEL, D_MODEL), lambda l, b: (l, 0, 0))
    out5 = pl.BlockSpec((None, None, N_MEM, X_HEADS, X_HD), lambda l, b: (l, b, 0, 0, 0))
    out4 = pl.BlockSpec((None, None, N_MEM, D_MODEL), lambda l, b: (l, b, 0, 0))
    shape5 = jax.ShapeDtypeStruct((depth, bsz, N_MEM, X_HEADS, X_HD), F32)
    shape4 = jax.ShapeDtypeStruct((depth, bsz, N_MEM, D_MODEL), BF16)
    return pl.pallas_call(
        _kv_kernel,
        grid=(depth, bsz),
        in_specs=[pl.BlockSpec((1, N_MEM, D_MODEL), lambda l, b: (b, 0, 0)), w_spec, w_spec],
        out_specs=[out5, out5, out4, out4],
        out_shape=[shape5, shape5, shape4, shape4],
        compiler_params=_params(),
        name="memory_kv",
    )(mem, w_k, w_v)


def _kv_cast_call(cache_k, cache_v):
    depth, bsz = cache_k.shape[:2]
    in_spec = pl.BlockSpec((None, None, N_MEM, X_HEADS, X_HD), lambda l, b: (l, b, 0, 0, 0))
    out_spec = pl.BlockSpec((None, None, N_MEM, D_MODEL), lambda l, b: (l, b, 0, 0))
    shape = jax.ShapeDtypeStruct((depth, bsz, N_MEM, D_MODEL), BF16)
    return pl.pallas_call(
        _kv_cast_kernel,
        grid=(depth, bsz),
        in_specs=[in_spec, in_spec],
        out_specs=[out_spec, out_spec],
        out_shape=[shape, shape],
        compiler_params=_params(),
        name="cache_kv_cast",
    )(cache_k, cache_v)


def _block_diag(blocks):
    n, r, c = blocks.shape[-3:]
    eye = jnp.eye(n, dtype=blocks.dtype)
    return (blocks[..., :, :, None, :] * eye[:, None, :, None]).reshape(blocks.shape[:-3] + (n * r, n * c))


def _tile_times(tt):
    i = jnp.arange(tt)
    return (i % SUBLANES) * (tt // SUBLANES) + i // SUBLANES


def _to_segments(x, tt):
    bsz, seq, c = x.shape
    return x.reshape(bsz, seq // tt, SUBLANES, tt // SUBLANES, c).swapaxes(2, 3).reshape(bsz, seq, c)


def _from_segments(x, tt):
    bsz, seq, c = x.shape
    return x.reshape(bsz, seq // tt, tt // SUBLANES, SUBLANES, c).swapaxes(2, 3).reshape(bsz, seq, c)


def _on_sublanes(state):
    return jnp.broadcast_to(state[..., None, :], state.shape[:-1] + (SUBLANES, state.shape[-1]))


def _shared_params(g_mix, w_in, conv_a_w, conv_a_b, w_rg, b_rg, w_ig, b_ig, lam, conv_b_w, g_v, w_out,
                   g_x, w_q, w_o, g_ffn, w_up, conv_f_w, w_down, g_final):
    per_half = A_HEADS // 2
    gates = [jnp.concatenate([_block_diag(w_rg[:, j * per_half:(j + 1) * per_half]),
                              _block_diag(w_ig[:, j * per_half:(j + 1) * per_half])], axis=-1)
             for j in range(2)]
    group = jnp.arange(C_WIDTH) // C_HD

    def row(a):
        return a[:, None, :]

    def taps(a):
        return a[:, :, None, :]

    return dict(
        g_mix=row(g_mix), w_in=w_in.astype(BF16), conv_a_w=taps(conv_a_w), conv_a_b=row(conv_a_b),
        w_gates=jnp.stack(gates, axis=1).astype(BF16), b_rg=row(b_rg), b_ig=row(b_ig), lam=row(lam),
        conv_b_w=taps(conv_b_w), g_v=row(g_v),
        group_mean=((group[:, None] == group[None, :]).astype(F32) / C_HD).astype(BF16),
        w_out=w_out.astype(BF16), g_x=row(g_x), w_q=w_q.astype(BF16), w_o=w_o.astype(BF16),
        g_ffn=row(g_ffn), w_up=w_up.astype(BF16), conv_f_w=taps(conv_f_w), w_down=w_down.astype(BF16),
        g_final=g_final[None])


def _tile_params(tt, chunk, w_s, b_s):
    pos = _tile_times(tt) % chunk
    return dict(w_s=w_s[:, :, pos][:, :, :, pos],
                b_s_full=jnp.repeat(jnp.swapaxes(b_s[:, :, pos], 1, 2), C_HD, axis=2))


def _tiles(bsz, seq):
    if seq <= MLP_CHUNK:
        return bsz, seq
    return PROMPT_BATCH_TILE, PROMPT_TIME_TILE


def _trunk(x, kb, vb, states, p, emit_vc):
    bsz, seq, _ = x.shape
    nb, tt = _tiles(bsz, seq)
    chunk = min(seq, MLP_CHUNK)
    x = _to_segments(x, tt)
    new_states = [[] for _ in range(4)]
    vcs = []
    for l in range(DEPTH):
        ca0, h0, cb0, cf0 = (s[l] for s in states)
        mix = _mixer_call(x, ca0, h0, cb0, p, l, nb, tt, chunk, emit_vc)
        x = _attn_call(mix[0], kb, vb, p, l, nb, tt)
        x, cf = _ffn_call(x, cf0, p, l, nb, tt, l == DEPTH - 1)
        for acc, val in zip(new_states, (mix[1], mix[2], mix[3], cf)):
            acc.append(val)
        if emit_vc:
            vcs.append(mix[4])
    new_states = tuple(jnp.stack(s)[..., SUBLANES - 1, :] for s in new_states)
    vc = _from_segments(jnp.concatenate(vcs, axis=0), tt).reshape(DEPTH, bsz, seq, C_WIDTH) if emit_vc else None
    return _from_segments(x, tt), new_states, vc


def kernel(x_prompt, x_sample, mem_prompt, cache_mem_k, cache_mem_v, state_conv_a, state_h_a, state_conv_b, state_conv_ffn, g_mix, w_in, conv_a_w, conv_a_b, w_rg, b_rg, w_ig, b_ig, lam, conv_b_w, g_v, w_s, b_s, w_out, g_x, w_q, w_k, w_v, w_o, g_ffn, w_up, conv_f_w, w_down, g_final):
    bp, seq_p, _ = x_prompt.shape
    bs, seq_s, _ = x_sample.shape
    shared = _shared_params(g_mix, w_in, conv_a_w, conv_a_b, w_rg, b_rg, w_ig, b_ig, lam, conv_b_w, g_v,
                            w_out, g_x, w_q, w_o, g_ffn, w_up, conv_f_w, w_down, g_final)

    def tile_params(bsz, seq):
        return dict(shared, **_tile_params(_tiles(bsz, seq)[1], min(seq, MLP_CHUNK), w_s, b_s))

    mem_k, mem_v, kb, vb = _kv_call(mem_prompt, w_k.astype(BF16), w_v.astype(BF16))
    zero_states = tuple(jnp.zeros((DEPTH, bp) + shape + (SUBLANES, width), F32)
                        for shape, width in (((A_CONV - 1,), A_WIDTH), ((), A_WIDTH),
                                             ((B_CONV - 1,), B_WIDTH), ((FFN_CONV - 1,), D_FF)))
    y_p, states_p, _ = _trunk(x_prompt, kb, vb, zero_states, tile_params(bp, seq_p), False)
    carried = tuple(_on_sublanes(s) for s in (state_conv_a, state_h_a, state_conv_b, state_conv_ffn))
    cache_kb, cache_vb = _kv_cast_call(cache_mem_k, cache_mem_v)
    y_s, states_s, vc_s = _trunk(x_sample, cache_kb, cache_vb, carried, tile_params(bs, seq_s), True)
    return (y_p, y_s) + states_p + (mem_k, mem_v) + states_s + (vc_s,)
```

```python
import functools
import math

import jax
import jax.numpy as jnp
from jax import lax
from jax.experimental import pallas as pl
from jax.experimental.pallas import tpu as pltpu

D_MODEL = 1024
DEPTH = 2
A_WIDTH = 512
A_HEADS = 8
A_HD = 64
A_CONV = 4
RG_C = 8.0
B_WIDTH = 256
B_CONV = 3
C_WIDTH = 256
C_HEADS = 4
C_HD = 64
MLP_CHUNK = 128
N_MEM = 256
X_HEADS = 4
X_HD = 256
D_FF = 2816
FFN_CONV = 3
EPS = 1e-6
IN_COLS = 2 * A_WIDTH + 3 * B_WIDTH + 2 * C_WIDTH

SUBLANES = 8
LANES = 128
PROMPT_TIME_TILE = 256
PROMPT_BATCH_TILE = 4
Z_SLOTS = 3
PROJ_PARTS = 9
VMEM_LIMIT_BYTES = 56 * 1024 * 1024

LOG2_E = math.log2(math.e)
GELU_C0 = math.sqrt(2.0 / math.pi)
GELU_C1 = 0.044715

F32 = jnp.float32
BF16 = jnp.bfloat16


def _dot(a, b):
    return jnp.dot(a, b, preferred_element_type=F32)


def _rms(x, g):
    ms = jnp.mean(x * x, axis=-1, keepdims=True)
    return x * lax.rsqrt(ms + EPS) * g


def _gelu(x):
    half_x = 0.5 * x
    inner = (x * x * (GELU_C0 * GELU_C1) + GELU_C0) * x
    return half_x * jnp.tanh(inner) + half_x


def _sublane_iota(shape):
    return lax.broadcasted_iota(jnp.int32, shape, len(shape) - 2)


def _causal_taps(cur, tail_ref, w):
    n_prev = tail_ref.shape[1]
    seg = cur.shape[1]
    last = cur[:, seg - n_prev:]
    wrapped = pltpu.roll(jnp.where(_sublane_iota(last.shape) == SUBLANES - 1, tail_ref[...], last), 1, 2)
    tail_ref[...] = last
    xp = jnp.concatenate([wrapped, cur], axis=1)
    acc = xp[:, 0:seg] * w[0]
    for j in range(1, n_prev + 1):
        acc = acc + xp[:, j:j + seg] * w[j]
    return acc


def _segment_scan(a, b, h_ref):
    seg = a.shape[1]
    local, decay = b[:, 0], a[:, 0]
    locals_, decays = [local], [decay]
    for s in range(1, seg):
        local = a[:, s] * local + b[:, s]
        decay = a[:, s] * decay
        locals_.append(local)
        decays.append(decay)
    h_in = h_ref[:, SUBLANES - 1:SUBLANES, :]
    row = _sublane_iota(local.shape)
    seg_a, seg_b = decay, local
    d = 1
    while d < SUBLANES:
        a_prev = pltpu.roll(seg_a, d, 1)
        b_prev = pltpu.roll(seg_b, d, 1)
        keep = row >= d
        seg_b = jnp.where(keep, seg_a * b_prev + seg_b, seg_b)
        seg_a = jnp.where(keep, seg_a * a_prev, seg_a)
        d *= 2
    h_end = seg_a * h_in + seg_b
    h_ref[...] = h_end
    h_start = jnp.where(row == 0, h_in, pltpu.roll(h_end, 1, 1))
    return jnp.stack([locals_[s] + decays[s] * h_start for s in range(seg)], axis=1)


def _grid_step():
    return (pl.program_id(0) * pl.num_programs(1) + pl.program_id(1),
            pl.num_programs(0) * pl.num_programs(1))


def _tile_copies(hbm_ref, buf_ref, sem, b, t, nb, tt, to_vmem):
    seg = tt // SUBLANES
    copies = []
    for i in range(nb):
        for r in range(SUBLANES):
            hbm = hbm_ref.at[b * nb + i, pl.ds(t * tt + r * seg, seg), :]
            vmem = buf_ref.at[i, :, r, :]
            copies.append(pltpu.make_async_copy(hbm, vmem, sem) if to_vmem
                          else pltpu.make_async_copy(vmem, hbm, sem))
    return copies


def _mixer_kernel(x_ref, ca0_ref, h0_ref, cb0_ref, gmix_ref, win_ref, caw_ref, cab_ref, wg_ref,
                  brg_ref, big_ref, lam_ref, cbw_ref, gv_ref, gsum_ref, ws_ref, bsf_ref, wout_ref,
                  xo_ref, cao_ref, ho_ref, cbo_ref, *rest, nb, gb, tt, chunk, emit_vc, gather_x):
    if gather_x:
        *rest, x_buf, x_sem = rest
    if emit_vc:
        vco_ref, ca_tail, cb_tail, h_state, wmix, ycat, zbuf = rest
    else:
        ca_tail, cb_tail, h_state, wmix, ycat, zbuf = rest
    rows = gb * tt
    seg = tt // SUBLANES
    groups = nb // gb

    if gather_x:
        step, n_steps = _grid_step()
        slot = lax.rem(step, 2)
        wraps = pl.program_id(1) + 1 == pl.num_programs(1)
        next_b = pl.program_id(0) + wraps.astype(jnp.int32)
        next_t = jnp.where(wraps, 0, pl.program_id(1) + 1)

        def fetch(b, t, into):
            return _tile_copies(x_ref, x_buf.at[into], x_sem.at[into], b, t, nb, tt, to_vmem=True)

        @pl.when(step == 0)
        def _():
            for c in fetch(pl.program_id(0), pl.program_id(1), slot):
                c.start()

        @pl.when(step + 1 < n_steps)
        def _():
            for c in fetch(next_b, next_t, 1 - slot):
                c.start()

        for c in fetch(pl.program_id(0), pl.program_id(1), slot):
            c.wait()

    def read_x(g):
        if gather_x:
            return x_buf[slot, g * gb:(g + 1) * gb].reshape(rows, D_MODEL)
        return x_ref[g * gb:(g + 1) * gb].reshape(rows, D_MODEL)

    @pl.when(pl.program_id(1) == 0)
    def _():
        ca_tail[...] = ca0_ref[...]
        cb_tail[...] = cb0_ref[...]
        h_state[...] = h0_ref[...]
        ti = lax.broadcasted_iota(jnp.int32, (tt, tt), 0)
        tj = lax.broadcasted_iota(jnp.int32, (tt, tt), 1)
        time_i = (ti % SUBLANES) * seg + ti // SUBLANES
        time_j = (tj % SUBLANES) * seg + tj // SUBLANES
        allowed = (time_i >= time_j) & (time_i // chunk == time_j // chunk)
        for hd in range(C_HEADS):
            wmix[hd] = jnp.where(allowed, ws_ref[hd], 0.0).astype(BF16)

    def project_in(g):
        xn = _rms(read_x(g), gmix_ref[...]).astype(BF16)
        z = zbuf.at[g % Z_SLOTS]
        width = IN_COLS // PROJ_PARTS

        def part(c):
            z[:, c * width:(c + 1) * width] = _dot(xn, win_ref[:, c * width:(c + 1) * width])

        return z, [functools.partial(part, c) for c in range(PROJ_PARTS)]

    col = {}
    o = 0
    for name, width in (("xa", A_WIDTH), ("ga", A_WIDTH), ("xb", B_WIDTH), ("gb", B_WIDTH),
                        ("gc", B_WIDTH), ("uc", C_WIDTH), ("vc", C_WIDTH)):
        col[name] = slice(o, o + width)
        o += width
    half = A_WIDTH // 2

    def mix_matmuls(g, z):
        seqs = slice(g * gb, (g + 1) * gb)
        out = {}

        def gates_and_norm():
            xc = _causal_taps(z[:, col["xa"]].reshape(gb, seg, SUBLANES, A_WIDTH), ca_tail.at[seqs],
                              caw_ref[...])
            xc2 = (xc + cab_ref[...]).reshape(rows, A_WIDTH)
            xcb = xc2.astype(BF16)
            out["xc2"] = xc2
            out["gz0"] = _dot(xcb[:, :half], wg_ref[0])
            out["gz1"] = _dot(xcb[:, half:], wg_ref[1])
            v = _gelu(z[:, col["vc"]])
            sq = v * v
            sq_hi = sq.astype(BF16)
            sq_lo = (sq - sq_hi.astype(F32)).astype(BF16)
            out["v"] = v
            out["ms"] = _dot(sq_hi, gsum_ref[...]) + _dot(sq_lo, gsum_ref[...])

        def spatial_mix():
            vn = out.pop("v") * lax.rsqrt(out.pop("ms") + EPS) * gv_ref[...]
            if emit_vc:
                vco_ref[seqs] = vn.reshape(gb, tt, C_WIDTH)
            vnb = vn.astype(BF16)
            first_head = lax.broadcasted_iota(jnp.int32, (tt, LANES), 1) < C_HD
            mixed = []
            for b in range(gb):
                pieces = []
                for pair in range(C_WIDTH // LANES):
                    vp = vnb[b * tt:(b + 1) * tt, pair * LANES:(pair + 1) * LANES]
                    pieces.append(jnp.where(first_head, _dot(wmix[2 * pair], vp), _dot(wmix[2 * pair + 1], vp)))
                mixed.append(jnp.concatenate(pieces, axis=1) + bsf_ref[...])
            out["mixed"] = jnp.concatenate(mixed, axis=0)

        return out, [gates_and_norm, spatial_mix]

    def mix_elementwise(g, z, xc2, gz0, gz1, mixed):
        seqs = slice(g * gb, (g + 1) * gb)
        y = ycat.at[g * rows:(g + 1) * rows]
        r = jax.nn.sigmoid(jnp.concatenate([gz0[:, :half], gz1[:, :half]], axis=1) + brg_ref[...])
        gi = jax.nn.sigmoid(jnp.concatenate([gz0[:, half:], gz1[:, half:]], axis=1) + big_ref[...])
        neg_lam = -lam_ref[...]
        softplus = jnp.maximum(neg_lam, 0.0) + jnp.log1p(jnp.exp(-jnp.abs(neg_lam)))
        decay_rate = RG_C * softplus
        a = jnp.exp2((-LOG2_E) * decay_rate * r)
        one_minus_a2 = (1.0 + a * a) * jnp.tanh(decay_rate * r)
        bterm = jnp.sqrt(one_minus_a2) * (gi * xc2)
        h = _segment_scan(a.reshape(gb, seg, SUBLANES, A_WIDTH),
                          bterm.reshape(gb, seg, SUBLANES, A_WIDTH), h_state.at[seqs])
        y[:, 0:A_WIDTH] = (_gelu(z[:, col["ga"]]) * h.reshape(rows, A_WIDTH)).astype(BF16)
        zb = _causal_taps((z[:, col["gc"]] * z[:, col["xb"]]).reshape(gb, seg, SUBLANES, B_WIDTH),
                          cb_tail.at[seqs], cbw_ref[...])
        y[:, A_WIDTH:A_WIDTH + B_WIDTH] = (z[:, col["gb"]] * zb.reshape(rows, B_WIDTH)).astype(BF16)
        y[:, A_WIDTH + B_WIDTH:] = (_gelu(z[:, col["uc"]]) * mixed).astype(BF16)

    def project_out(g):
        out = read_x(g) + _dot(ycat[g * rows:(g + 1) * rows], wout_ref[...])
        xo_ref[g * gb:(g + 1) * gb] = out.reshape(gb, tt, D_MODEL)

    def interleave(big_steps, small_steps):
        stride = max(1, len(big_steps) // max(1, len(small_steps)))
        pending = list(small_steps)
        for i, step in enumerate(big_steps):
            step()
            if pending and (i + 1) % stride == 0:
                pending.pop(0)()
        for step in pending:
            step()

    zs, small = {}, {}
    zs[0], parts = project_in(0)
    interleave(parts, [])
    small[0], steps = mix_matmuls(0, zs[0])
    if groups > 1:
        zs[1], parts = project_in(1)
        interleave(parts, steps)
    else:
        interleave([], steps)
    for g in range(groups):
        parts, steps = [], []
        if g + 2 < groups:
            zs[g + 2], parts = project_in(g + 2)
        if g + 1 < groups:
            small[g + 1], steps = mix_matmuls(g + 1, zs[g + 1])
        interleave(parts, steps)
        res = small.pop(g)
        mix_elementwise(g, zs.pop(g), res["xc2"], res["gz0"], res["gz1"], res["mixed"])
        project_out(g)
    cao_ref[...] = ca_tail[...]
    ho_ref[...] = h_state[...]
    cbo_ref[...] = cb_tail[...]


def _attn_kernel(x_ref, kb, vb, gx_ref, wq_ref, wo_ref, xo_ref, ocat, *, nb, gb, tt):
    rows = gb * tt
    groups = nb // gb

    def project_q(g):
        x = x_ref[g * gb:(g + 1) * gb].reshape(rows, D_MODEL)
        xn = _rms(x, gx_ref[...]).astype(BF16)
        return (_dot(xn, wq_ref[...]) * (X_HD ** -0.5)).astype(BF16)

    def attend(g, q):
        for b in range(gb):
            seq = g * gb + b
            for hd in range(X_HEADS):
                cols = slice(hd * X_HD, (hd + 1) * X_HD)
                s = lax.dot_general(q[b * tt:(b + 1) * tt, cols], kb[seq, :, cols], (((1,), (1,)), ((), ())),
                                    preferred_element_type=F32)
                e = jnp.exp(s - jnp.max(s, axis=-1, keepdims=True))
                denom = jnp.sum(e, axis=-1, keepdims=True)
                oh = _dot(e.astype(BF16), vb[seq, :, cols]) / denom
                ocat[seq * tt:(seq + 1) * tt, cols] = oh.astype(BF16)

    def project_out(g):
        x = x_ref[g * gb:(g + 1) * gb].reshape(rows, D_MODEL)
        out = x + _dot(ocat[g * rows:(g + 1) * rows], wo_ref[...])
        xo_ref[g * gb:(g + 1) * gb] = out.reshape(gb, tt, D_MODEL)

    q = project_q(0)
    for g in range(groups):
        q_next = project_q(g + 1) if g + 1 < groups else None
        attend(g, q)
        project_out(g)
        q = q_next


def _ffn_kernel(x_ref, cf0_ref, gffn_ref, wup_ref, cfw_ref, wdown_ref, gfin_ref, xo_ref, cfo_ref,
                cf_tail, *rest, nb, tt, final_norm, scatter_out):
    rows = nb * tt
    seg = tt // SUBLANES

    @pl.when(pl.program_id(1) == 0)
    def _():
        cf_tail[...] = cf0_ref[...]

    x = x_ref[...].reshape(rows, D_MODEL)
    xn = _rms(x, gffn_ref[...]).astype(BF16)
    gu = _dot(xn, wup_ref[...])
    gconv = _causal_taps(gu[:, :D_FF].reshape(nb, seg, SUBLANES, D_FF), cf_tail, cfw_ref[...])
    cfo_ref[...] = cf_tail[...]
    act = (jax.nn.silu(gconv.reshape(rows, D_FF)) * gu[:, D_FF:]).astype(BF16)
    out = x + _dot(act, wdown_ref[...])
    if final_norm:
        out = _rms(out, gfin_ref[...])
    if not scatter_out:
        xo_ref[...] = out.reshape(nb, tt, D_MODEL)
        return

    out_buf, out_sem = rest
    step, n_steps = _grid_step()
    slot = lax.rem(step, 2)

    def flush(from_slot):
        return _tile_copies(xo_ref, out_buf.at[from_slot], out_sem.at[from_slot],
                            pl.program_id(0), pl.program_id(1), nb, tt, to_vmem=False)

    @pl.when(step >= 2)
    def _():
        for c in flush(slot):
            c.wait()

    out_buf[slot] = out.reshape(nb, seg, SUBLANES, D_MODEL)
    for c in flush(slot):
        c.start()

    @pl.when(step == n_steps - 1)
    def _():
        @pl.when(n_steps > 1)
        def _():
            for c in flush(1 - slot):
                c.wait()
        for c in flush(slot):
            c.wait()


def _kv_kernel(mem_ref, wk_ref, wv_ref, ko_ref, vo_ref, kb_ref, vb_ref):
    m = mem_ref[0].astype(BF16)
    for w_ref, o_ref, b_ref in ((wk_ref, ko_ref, kb_ref), (wv_ref, vo_ref, vb_ref)):
        kv = _dot(m, w_ref[...])
        b_ref[...] = kv.astype(BF16)
        for hd in range(X_HEADS):
            o_ref[:, hd, :] = kv[:, hd * X_HD:(hd + 1) * X_HD]


def _kv_cast_kernel(k_ref, v_ref, kb_ref, vb_ref):
    for src, dst in ((k_ref, kb_ref), (v_ref, vb_ref)):
        for hd in range(X_HEADS):
            dst[:, hd * X_HD:(hd + 1) * X_HD] = src[:, hd, :].astype(BF16)


def _whole():
    return pl.BlockSpec(memory_space=pltpu.MemorySpace.VMEM)


def _of_layer(arr, l):
    zeros = (0,) * (arr.ndim - 1)
    return pl.BlockSpec((None,) + arr.shape[1:], lambda b, t: (l,) + zeros, pipeline_mode=pl.Buffered(1))


def _params():
    return pltpu.CompilerParams(dimension_semantics=("arbitrary", "arbitrary"),
                                vmem_limit_bytes=VMEM_LIMIT_BYTES)


def _tile_specs(nb, tt):
    x_spec = pl.BlockSpec((nb, tt, D_MODEL), lambda b, t: (b, t, 0))

    def state_spec(n, width):
        return pl.BlockSpec((nb, n, SUBLANES, width), lambda b, t: (b, 0, 0, 0))

    return x_spec, state_spec


def _mixer_call(x, ca0, h0, cb0, p, l, nb, tt, chunk, emit_vc, gather_x):
    bsz, seq, _ = x.shape
    layer_params = [p[name] for name in ("g_mix", "w_in", "conv_a_w", "conv_a_b", "w_gates", "b_rg", "b_ig",
                                         "lam", "conv_b_w", "g_v")]
    mix_params = [p[name] for name in ("w_s", "b_s_full", "w_out")]
    x_spec, state_spec = _tile_specs(nb, tt)
    h_spec = pl.BlockSpec((nb, SUBLANES, A_WIDTH), lambda b, t: (b, 0, 0))
    out_shape = [jax.ShapeDtypeStruct(x.shape, F32),
                 jax.ShapeDtypeStruct(ca0.shape, F32),
                 jax.ShapeDtypeStruct(h0.shape, F32),
                 jax.ShapeDtypeStruct(cb0.shape, F32)]
    out_specs = [x_spec, state_spec(A_CONV - 1, A_WIDTH), h_spec, state_spec(B_CONV - 1, B_WIDTH)]
    if emit_vc:
        out_shape.append(jax.ShapeDtypeStruct((bsz, seq, C_WIDTH), F32))
        out_specs.append(pl.BlockSpec((nb, tt, C_WIDTH), lambda b, t: (b, t, 0)))
    gb = nb if seq == tt else 1
    kern = functools.partial(_mixer_kernel, nb=nb, gb=gb, tt=tt, chunk=chunk, emit_vc=emit_vc,
                             gather_x=gather_x)
    gather_scratch = [pltpu.VMEM((2, nb, tt // SUBLANES, SUBLANES, D_MODEL), F32),
                      pltpu.SemaphoreType.DMA((2,))] if gather_x else []
    return pl.pallas_call(
        kern,
        grid=(bsz // nb, seq // tt),
        in_specs=([pl.BlockSpec(memory_space=pl.ANY) if gather_x else x_spec,
                   state_spec(A_CONV - 1, A_WIDTH), h_spec, state_spec(B_CONV - 1, B_WIDTH)]
                  + [_of_layer(a, l) for a in layer_params] + [_whole()]
                  + [_of_layer(a, l) for a in mix_params]),
        out_specs=out_specs,
        out_shape=out_shape,
        scratch_shapes=[pltpu.VMEM((nb, A_CONV - 1, SUBLANES, A_WIDTH), F32),
                        pltpu.VMEM((nb, B_CONV - 1, SUBLANES, B_WIDTH), F32),
                        pltpu.VMEM((nb, SUBLANES, A_WIDTH), F32),
                        pltpu.VMEM((C_HEADS, tt, tt), BF16),
                        pltpu.VMEM((nb * tt, D_MODEL), BF16),
                        pltpu.VMEM((Z_SLOTS, gb * tt, IN_COLS), F32)] + gather_scratch,
        compiler_params=_params(),
        name="mixer",
    )(x, ca0, h0, cb0, *layer_params, p["group_mean"], *mix_params)


def _attn_call(x, kb, vb, p, l, nb, tt):
    bsz, seq, _ = x.shape
    x_spec, _ = _tile_specs(nb, tt)
    kv_spec = pl.BlockSpec((None, nb, N_MEM, D_MODEL), lambda b, t: (l, b, 0, 0))
    weights = [p["g_x"], p["w_q"], p["w_o"]]
    gb = nb if seq == tt else 1
    kern = functools.partial(_attn_kernel, nb=nb, gb=gb, tt=tt)
    return pl.pallas_call(
        kern,
        grid=(bsz // nb, seq // tt),
        in_specs=[x_spec, kv_spec, kv_spec] + [_of_layer(a, l) for a in weights],
        out_specs=x_spec,
        out_shape=jax.ShapeDtypeStruct(x.shape, F32),
        scratch_shapes=[pltpu.VMEM((nb * tt, D_MODEL), BF16)],
        compiler_params=_params(),
        name="attn",
    )(x, kb, vb, *weights)


def _ffn_call(x, cf0, p, l, nb, tt, final_norm, scatter_out):
    bsz, seq, _ = x.shape
    x_spec, state_spec = _tile_specs(nb, tt)
    weights = [p["g_ffn"], p["w_up"], p["conv_f_w"], p["w_down"]]
    kern = functools.partial(_ffn_kernel, nb=nb, tt=tt, final_norm=final_norm, scatter_out=scatter_out)
    scatter_scratch = [pltpu.VMEM((2, nb, tt // SUBLANES, SUBLANES, D_MODEL), F32),
                       pltpu.SemaphoreType.DMA((2,))] if scatter_out else []
    return pl.pallas_call(
        kern,
        grid=(bsz // nb, seq // tt),
        in_specs=[x_spec, state_spec(FFN_CONV - 1, D_FF)] + [_of_layer(a, l) for a in weights] + [_whole()],
        out_specs=[pl.BlockSpec(memory_space=pl.ANY) if scatter_out else x_spec,
                   state_spec(FFN_CONV - 1, D_FF)],
        out_shape=[jax.ShapeDtypeStruct(x.shape, F32), jax.ShapeDtypeStruct(cf0.shape, F32)],
        scratch_shapes=[pltpu.VMEM((nb, FFN_CONV - 1, SUBLANES, D_FF), F32)] + scatter_scratch,
        compiler_params=_params(),
        name="ffn",
    )(x, cf0, *weights, p["g_final"])


def _kv_call(mem, w_k, w_v):
    bsz = mem.shape[0]
    depth = w_k.shape[0]
    w_spec = pl.BlockSpec((None, D_MODEL, D_MODEL), lambda l, b: (l, 0, 0))
    out5 = pl.BlockSpec((None, None, N_MEM, X_HEADS, X_HD), lambda l, b: (l, b, 0, 0, 0))
    out4 = pl.BlockSpec((None, None, N_MEM, D_MODEL), lambda l, b: (l, b, 0, 0))
    shape5 = jax.ShapeDtypeStruct((depth, bsz, N_MEM, X_HEADS, X_HD), F32)
    shape4 = jax.ShapeDtypeStruct((depth, bsz, N_MEM, D_MODEL), BF16)
    return pl.pallas_call(
        _kv_kernel,
        grid=(depth, bsz),
        in_specs=[pl.BlockSpec((1, N_MEM, D_MODEL), lambda l, b: (b, 0, 0)), w_spec, w_spec],
        out_specs=[out5, out5, out4, out4],
        out_shape=[shape5, shape5, shape4, shape4],
        compiler_params=_params(),
        name="memory_kv",
    )(mem, w_k, w_v)


def _kv_cast_call(cache_k, cache_v):
    depth, bsz = cache_k.shape[:2]
    in_spec = pl.BlockSpec((None, None, N_MEM, X_HEADS, X_HD), lambda l, b: (l, b, 0, 0, 0))
    out_spec = pl.BlockSpec((None, None, N_MEM, D_MODEL), lambda l, b: (l, b, 0, 0))
    shape = jax.ShapeDtypeStruct((depth, bsz, N_MEM, D_MODEL), BF16)
    return pl.pallas_call(
        _kv_cast_kernel,
        grid=(depth, bsz),
        in_specs=[in_spec, in_spec],
        out_specs=[out_spec, out_spec],
        out_shape=[shape, shape],
        compiler_params=_params(),
        name="cache_kv_cast",
    )(cache_k, cache_v)


def _block_diag(blocks):
    n, r, c = blocks.shape[-3:]
    eye = jnp.eye(n, dtype=blocks.dtype)
    return (blocks[..., :, :, None, :] * eye[:, None, :, None]).reshape(blocks.shape[:-3] + (n * r, n * c))


def _tile_times(tt):
    i = jnp.arange(tt)
    return (i % SUBLANES) * (tt // SUBLANES) + i // SUBLANES


def _to_segments(x, tt):
    bsz, seq, c = x.shape
    return x.reshape(bsz, seq // tt, SUBLANES, tt // SUBLANES, c).swapaxes(2, 3).reshape(bsz, seq, c)


def _from_segments(x, tt):
    bsz, seq, c = x.shape
    return x.reshape(bsz, seq // tt, tt // SUBLANES, SUBLANES, c).swapaxes(2, 3).reshape(bsz, seq, c)


def _on_sublanes(state):
    return jnp.broadcast_to(state[..., None, :], state.shape[:-1] + (SUBLANES, state.shape[-1]))


def _shared_params(g_mix, w_in, conv_a_w, conv_a_b, w_rg, b_rg, w_ig, b_ig, lam, conv_b_w, g_v, w_out,
                   g_x, w_q, w_o, g_ffn, w_up, conv_f_w, w_down, g_final):
    per_half = A_HEADS // 2
    gates = [jnp.concatenate([_block_diag(w_rg[:, j * per_half:(j + 1) * per_half]),
                              _block_diag(w_ig[:, j * per_half:(j + 1) * per_half])], axis=-1)
             for j in range(2)]
    group = jnp.arange(C_WIDTH) // C_HD

    def row(a):
        return a[:, None, :]

    def taps(a):
        return a[:, :, None, :]

    return dict(
        g_mix=row(g_mix), w_in=w_in.astype(BF16), conv_a_w=taps(conv_a_w), conv_a_b=row(conv_a_b),
        w_gates=jnp.stack(gates, axis=1).astype(BF16), b_rg=row(b_rg), b_ig=row(b_ig), lam=row(lam),
        conv_b_w=taps(conv_b_w), g_v=row(g_v),
        group_mean=((group[:, None] == group[None, :]).astype(F32) / C_HD).astype(BF16),
        w_out=w_out.astype(BF16), g_x=row(g_x), w_q=w_q.astype(BF16), w_o=w_o.astype(BF16),
        g_ffn=row(g_ffn), w_up=w_up.astype(BF16), conv_f_w=taps(conv_f_w), w_down=w_down.astype(BF16),
        g_final=g_final[None])


def _tile_params(tt, chunk, w_s, b_s):
    pos = _tile_times(tt) % chunk
    return dict(w_s=w_s[:, :, pos][:, :, :, pos],
                b_s_full=jnp.repeat(jnp.swapaxes(b_s[:, :, pos], 1, 2), C_HD, axis=2))


def _tiles(bsz, seq):
    if seq <= MLP_CHUNK:
        return bsz, seq
    return PROMPT_BATCH_TILE, PROMPT_TIME_TILE


def _trunk(x, kb, vb, states, p, emit_vc):
    bsz, seq, _ = x.shape
    nb, tt = _tiles(bsz, seq)
    chunk = min(seq, MLP_CHUNK)
    by_dma = seq > tt
    if not by_dma:
        x = _to_segments(x, tt)
    new_states = [[] for _ in range(4)]
    vcs = []
    for l in range(DEPTH):
        last = l == DEPTH - 1
        ca0, h0, cb0, cf0 = (s[l] for s in states)
        mix = _mixer_call(x, ca0, h0, cb0, p, l, nb, tt, chunk, emit_vc, by_dma and l == 0)
        x = _attn_call(mix[0], kb, vb, p, l, nb, tt)
        x, cf = _ffn_call(x, cf0, p, l, nb, tt, last, by_dma and last)
        for acc, val in zip(new_states, (mix[1], mix[2], mix[3], cf)):
            acc.append(val)
        if emit_vc:
            vcs.append(mix[4])
    new_states = tuple(jnp.stack(s)[..., SUBLANES - 1, :] for s in new_states)
    vc = _from_segments(jnp.concatenate(vcs, axis=0), tt).reshape(DEPTH, bsz, seq, C_WIDTH) if emit_vc else None
    return (x if by_dma else _from_segments(x, tt)), new_states, vc


def kernel(x_prompt, x_sample, mem_prompt, cache_mem_k, cache_mem_v, state_conv_a, state_h_a, state_conv_b, state_conv_ffn, g_mix, w_in, conv_a_w, conv_a_b, w_rg, b_rg, w_ig, b_ig, lam, conv_b_w, g_v, w_s, b_s, w_out, g_x, w_q, w_k, w_v, w_o, g_ffn, w_up, conv_f_w, w_down, g_final):
    bp, seq_p, _ = x_prompt.shape
    bs, seq_s, _ = x_sample.shape
    shared = _shared_params(g_mix, w_in, conv_a_w, conv_a_b, w_rg, b_rg, w_ig, b_ig, lam, conv_b_w, g_v,
                            w_out, g_x, w_q, w_o, g_ffn, w_up, conv_f_w, w_down, g_final)

    def tile_params(bsz, seq):
        return dict(shared, **_tile_params(_tiles(bsz, seq)[1], min(seq, MLP_CHUNK), w_s, b_s))

    mem_k, mem_v, kb, vb = _kv_call(mem_prompt, w_k.astype(BF16), w_v.astype(BF16))
    zero_states = tuple(jnp.zeros((DEPTH, bp) + shape + (SUBLANES, width), F32)
                        for shape, width in (((A_CONV - 1,), A_WIDTH), ((), A_WIDTH),
                                             ((B_CONV - 1,), B_WIDTH), ((FFN_CONV - 1,), D_FF)))
    y_p, states_p, _ = _trunk(x_prompt, kb, vb, zero_states, tile_params(bp, seq_p), False)
    carried = tuple(_on_sublanes(s) for s in (state_conv_a, state_h_a, state_conv_b, state_conv_ffn))
    cache_kb, cache_vb = _kv_cast_call(cache_mem_k, cache_mem_v)
    y_s, states_s, vc_s = _trunk(x_sample, cache_kb, cache_vb, carried, tile_params(bs, seq_s), True)
    return (y_p, y_s) + states_p + (mem_k, mem_v) + states_s + (vc_s,)
```

```python
import functools
import math

import jax
import jax.numpy as jnp
from jax import lax
from jax.experimental import pallas as pl
from jax.experimental.pallas import tpu as pltpu

D_MODEL = 1024
DEPTH = 2
A_WIDTH = 512
A_HEADS = 8
A_HD = 64
A_CONV = 4
RG_C = 8.0
B_WIDTH = 256
B_CONV = 3
C_WIDTH = 256
C_HEADS = 4
C_HD = 64
MLP_CHUNK = 128
N_MEM = 256
X_HEADS = 4
X_HD = 256
D_FF = 2816
FFN_CONV = 3
EPS = 1e-6
IN_COLS = 2 * A_WIDTH + 3 * B_WIDTH + 2 * C_WIDTH

SUBLANES = 8
LANES = 128
PROMPT_TIME_TILE = 256
PROMPT_BATCH_TILE = 4
Z_SLOTS = 3
PROJ_PARTS = 9
VMEM_LIMIT_BYTES = 56 * 1024 * 1024

LOG2_E = math.log2(math.e)
GELU_C0 = math.sqrt(2.0 / math.pi)
GELU_C1 = 0.044715

F32 = jnp.float32
BF16 = jnp.bfloat16


def _dot(a, b):
    return jnp.dot(a, b, preferred_element_type=F32)


def _rms(x, g):
    ms = jnp.mean(x * x, axis=-1, keepdims=True)
    return x * lax.rsqrt(ms + EPS) * g


def _gelu(x):
    half_x = 0.5 * x
    inner = (x * x * (GELU_C0 * GELU_C1) + GELU_C0) * x
    return half_x * jnp.tanh(inner) + half_x


def _sublane_iota(shape):
    return lax.broadcasted_iota(jnp.int32, shape, len(shape) - 2)


def _causal_taps(cur, tail_ref, w):
    n_prev = tail_ref.shape[1]
    seg = cur.shape[1]
    last = cur[:, seg - n_prev:]
    wrapped = pltpu.roll(jnp.where(_sublane_iota(last.shape) == SUBLANES - 1, tail_ref[...], last), 1, 2)
    tail_ref[...] = last
    xp = jnp.concatenate([wrapped, cur], axis=1)
    acc = xp[:, 0:seg] * w[0]
    for j in range(1, n_prev + 1):
        acc = acc + xp[:, j:j + seg] * w[j]
    return acc


def _segment_scan(a, b, h_ref):
    seg = a.shape[1]
    local, decay = b[:, 0], a[:, 0]
    locals_, decays = [local], [decay]
    for s in range(1, seg):
        local = a[:, s] * local + b[:, s]
        decay = a[:, s] * decay
        locals_.append(local)
        decays.append(decay)
    h_in = h_ref[:, SUBLANES - 1:SUBLANES, :]
    row = _sublane_iota(local.shape)
    seg_a, seg_b = decay, local
    d = 1
    while d < SUBLANES:
        a_prev = pltpu.roll(seg_a, d, 1)
        b_prev = pltpu.roll(seg_b, d, 1)
        keep = row >= d
        seg_b = jnp.where(keep, seg_a * b_prev + seg_b, seg_b)
        seg_a = jnp.where(keep, seg_a * a_prev, seg_a)
        d *= 2
    h_end = seg_a * h_in + seg_b
    h_ref[...] = h_end
    h_start = jnp.where(row == 0, h_in, pltpu.roll(h_end, 1, 1))
    return jnp.stack([locals_[s] + decays[s] * h_start for s in range(seg)], axis=1)


def _grid_step():
    return (pl.program_id(0) * pl.num_programs(1) + pl.program_id(1),
            pl.num_programs(0) * pl.num_programs(1))


def _tile_copies(hbm_ref, buf_ref, sem, b, t, nb, tt, to_vmem):
    seg = tt // SUBLANES
    copies = []
    for r in range(SUBLANES):
        hbm = hbm_ref.at[pl.ds(b * nb, nb), pl.ds(t * tt + r * seg, seg), :]
        vmem = buf_ref.at[:, :, r, :]
        copies.append(pltpu.make_async_copy(hbm, vmem, sem) if to_vmem
                      else pltpu.make_async_copy(vmem, hbm, sem))
    return copies


def _head_copies(hbm_refs, buf_ref, sem, l, b, to_vmem):
    copies = []
    for j, hbm_ref in enumerate(hbm_refs):
        for hd in range(X_HEADS):
            hbm = hbm_ref.at[l, b, :, hd, :]
            vmem = buf_ref.at[j, :, pl.ds(hd * X_HD, X_HD)]
            copies.append(pltpu.make_async_copy(hbm, vmem, sem) if to_vmem
                          else pltpu.make_async_copy(vmem, hbm, sem))
    return copies


def _next_grid_index():
    wraps = pl.program_id(1) + 1 == pl.num_programs(1)
    return pl.program_id(0) + wraps.astype(jnp.int32), jnp.where(wraps, 0, pl.program_id(1) + 1)


def _prefetched(fetch):
    step, n_steps = _grid_step()
    slot = lax.rem(step, 2)

    @pl.when(step == 0)
    def _():
        for c in fetch(pl.program_id(0), pl.program_id(1), slot):
            c.start()

    @pl.when(step + 1 < n_steps)
    def _():
        for c in fetch(*_next_grid_index(), 1 - slot):
            c.start()

    for c in fetch(pl.program_id(0), pl.program_id(1), slot):
        c.wait()
    return slot


def _flushed(flush, fill):
    step, n_steps = _grid_step()
    slot = lax.rem(step, 2)

    @pl.when(step >= 2)
    def _():
        for c in flush(slot):
            c.wait()

    fill(slot)
    for c in flush(slot):
        c.start()

    @pl.when(step == n_steps - 1)
    def _():
        @pl.when(n_steps > 1)
        def _():
            for c in flush(1 - slot):
                c.wait()
        for c in flush(slot):
            c.wait()


def _mixer_kernel(x_ref, ca0_ref, h0_ref, cb0_ref, gmix_ref, win_ref, caw_ref, cab_ref, wg_ref,
                  brg_ref, big_ref, lam_ref, cbw_ref, gv_ref, gsum_ref, ws_ref, bsf_ref, wout_ref,
                  xo_ref, cao_ref, ho_ref, cbo_ref, *rest, nb, gb, tt, chunk, emit_vc, gather_x):
    if gather_x:
        *rest, x_buf, x_sem = rest
    if emit_vc:
        vco_ref, ca_tail, cb_tail, h_state, wmix, ycat, zbuf = rest
    else:
        ca_tail, cb_tail, h_state, wmix, ycat, zbuf = rest
    rows = gb * tt
    seg = tt // SUBLANES
    groups = nb // gb

    if gather_x:
        slot = _prefetched(lambda b, t, into: _tile_copies(x_ref, x_buf.at[into], x_sem.at[into],
                                                           b, t, nb, tt, to_vmem=True))

    def read_x(g):
        if gather_x:
            return x_buf[slot, g * gb:(g + 1) * gb].reshape(rows, D_MODEL)
        return x_ref[g * gb:(g + 1) * gb].reshape(rows, D_MODEL)

    @pl.when(pl.program_id(1) == 0)
    def _():
        ca_tail[...] = ca0_ref[...]
        cb_tail[...] = cb0_ref[...]
        h_state[...] = h0_ref[...]
        ti = lax.broadcasted_iota(jnp.int32, (tt, tt), 0)
        tj = lax.broadcasted_iota(jnp.int32, (tt, tt), 1)
        time_i = (ti % SUBLANES) * seg + ti // SUBLANES
        time_j = (tj % SUBLANES) * seg + tj // SUBLANES
        allowed = (time_i >= time_j) & (time_i // chunk == time_j // chunk)
        for hd in range(C_HEADS):
            wmix[hd] = jnp.where(allowed, ws_ref[hd], 0.0).astype(BF16)

    def project_in(g):
        xn = _rms(read_x(g), gmix_ref[...]).astype(BF16)
        z = zbuf.at[g % Z_SLOTS]
        width = IN_COLS // PROJ_PARTS

        def part(c):
            z[:, c * width:(c + 1) * width] = _dot(xn, win_ref[:, c * width:(c + 1) * width])

        return z, [functools.partial(part, c) for c in range(PROJ_PARTS)]

    col = {}
    o = 0
    for name, width in (("xa", A_WIDTH), ("ga", A_WIDTH), ("xb", B_WIDTH), ("gb", B_WIDTH),
                        ("gc", B_WIDTH), ("uc", C_WIDTH), ("vc", C_WIDTH)):
        col[name] = slice(o, o + width)
        o += width
    half = A_WIDTH // 2

    def mix_matmuls(g, z):
        seqs = slice(g * gb, (g + 1) * gb)
        out = {}

        def gates_and_norm():
            xc = _causal_taps(z[:, col["xa"]].reshape(gb, seg, SUBLANES, A_WIDTH), ca_tail.at[seqs],
                              caw_ref[...])
            xc2 = (xc + cab_ref[...]).reshape(rows, A_WIDTH)
            xcb = xc2.astype(BF16)
            out["xc2"] = xc2
            out["gz0"] = _dot(xcb[:, :half], wg_ref[0])
            out["gz1"] = _dot(xcb[:, half:], wg_ref[1])
            v = _gelu(z[:, col["vc"]])
            sq = v * v
            sq_hi = sq.astype(BF16)
            sq_lo = (sq - sq_hi.astype(F32)).astype(BF16)
            out["v"] = v
            out["ms"] = _dot(sq_hi, gsum_ref[...]) + _dot(sq_lo, gsum_ref[...])

        def spatial_mix():
            vn = out.pop("v") * lax.rsqrt(out.pop("ms") + EPS) * gv_ref[...]
            if emit_vc:
                vco_ref[seqs] = vn.reshape(gb, tt, C_WIDTH)
            vnb = vn.astype(BF16)
            first_head = lax.broadcasted_iota(jnp.int32, (tt, LANES), 1) < C_HD
            mixed = []
            for b in range(gb):
                pieces = []
                for pair in range(C_WIDTH // LANES):
                    vp = vnb[b * tt:(b + 1) * tt, pair * LANES:(pair + 1) * LANES]
                    pieces.append(jnp.where(first_head, _dot(wmix[2 * pair], vp), _dot(wmix[2 * pair + 1], vp)))
                mixed.append(jnp.concatenate(pieces, axis=1) + bsf_ref[...])
            out["mixed"] = jnp.concatenate(mixed, axis=0)

        return out, [gates_and_norm, spatial_mix]

    def mix_elementwise(g, z, xc2, gz0, gz1, mixed):
        seqs = slice(g * gb, (g + 1) * gb)
        y = ycat.at[g * rows:(g + 1) * rows]
        r = jax.nn.sigmoid(jnp.concatenate([gz0[:, :half], gz1[:, :half]], axis=1) + brg_ref[...])
        gi = jax.nn.sigmoid(jnp.concatenate([gz0[:, half:], gz1[:, half:]], axis=1) + big_ref[...])
        neg_lam = -lam_ref[...]
        softplus = jnp.maximum(neg_lam, 0.0) + jnp.log1p(jnp.exp(-jnp.abs(neg_lam)))
        decay_rate = RG_C * softplus
        a = jnp.exp2((-LOG2_E) * decay_rate * r)
        one_minus_a2 = (1.0 + a * a) * jnp.tanh(decay_rate * r)
        root = jnp.where(one_minus_a2 > 0.0, one_minus_a2 * lax.rsqrt(one_minus_a2), 0.0)
        bterm = root * (gi * xc2)
        h = _segment_scan(a.reshape(gb, seg, SUBLANES, A_WIDTH),
                          bterm.reshape(gb, seg, SUBLANES, A_WIDTH), h_state.at[seqs])
        y[:, 0:A_WIDTH] = (_gelu(z[:, col["ga"]]) * h.reshape(rows, A_WIDTH)).astype(BF16)
        zb = _causal_taps((z[:, col["gc"]] * z[:, col["xb"]]).reshape(gb, seg, SUBLANES, B_WIDTH),
                          cb_tail.at[seqs], cbw_ref[...])
        y[:, A_WIDTH:A_WIDTH + B_WIDTH] = (z[:, col["gb"]] * zb.reshape(rows, B_WIDTH)).astype(BF16)
        y[:, A_WIDTH + B_WIDTH:] = (_gelu(z[:, col["uc"]]) * mixed).astype(BF16)

    def project_out(g):
        out = read_x(g) + _dot(ycat[g * rows:(g + 1) * rows], wout_ref[...])
        xo_ref[g * gb:(g + 1) * gb] = out.reshape(gb, tt, D_MODEL)

    def interleave(big_steps, small_steps):
        stride = max(1, len(big_steps) // max(1, len(small_steps)))
        pending = list(small_steps)
        for i, step in enumerate(big_steps):
            step()
            if pending and (i + 1) % stride == 0:
                pending.pop(0)()
        for step in pending:
            step()

    zs, small = {}, {}
    zs[0], parts = project_in(0)
    interleave(parts, [])
    small[0], steps = mix_matmuls(0, zs[0])
    if groups > 1:
        zs[1], parts = project_in(1)
        interleave(parts, steps)
    else:
        interleave([], steps)
    for g in range(groups):
        parts, steps = [], []
        if g + 2 < groups:
            zs[g + 2], parts = project_in(g + 2)
        if g + 1 < groups:
            small[g + 1], steps = mix_matmuls(g + 1, zs[g + 1])
        interleave(parts, steps)
        res = small.pop(g)
        mix_elementwise(g, zs.pop(g), res["xc2"], res["gz0"], res["gz1"], res["mixed"])
        project_out(g)
    cao_ref[...] = ca_tail[...]
    ho_ref[...] = h_state[...]
    cbo_ref[...] = cb_tail[...]


def _attn_kernel(x_ref, kb, vb, gx_ref, wq_ref, wo_ref, xo_ref, ocat, *, nb, gb, tt):
    rows = gb * tt
    groups = nb // gb

    def project_q(g):
        x = x_ref[g * gb:(g + 1) * gb].reshape(rows, D_MODEL)
        xn = _rms(x, gx_ref[...]).astype(BF16)
        return (_dot(xn, wq_ref[...]) * (X_HD ** -0.5)).astype(BF16)

    def attend(g, q):
        for b in range(gb):
            seq = g * gb + b
            for hd in range(X_HEADS):
                cols = slice(hd * X_HD, (hd + 1) * X_HD)
                s = lax.dot_general(q[b * tt:(b + 1) * tt, cols], kb[seq, :, cols], (((1,), (1,)), ((), ())),
                                    preferred_element_type=F32)
                e = jnp.exp(s - jnp.max(s, axis=-1, keepdims=True))
                denom = jnp.sum(e, axis=-1, keepdims=True)
                oh = _dot(e.astype(BF16), vb[seq, :, cols]) / denom
                ocat[seq * tt:(seq + 1) * tt, cols] = oh.astype(BF16)

    def project_out(g):
        x = x_ref[g * gb:(g + 1) * gb].reshape(rows, D_MODEL)
        out = x + _dot(ocat[g * rows:(g + 1) * rows], wo_ref[...])
        xo_ref[g * gb:(g + 1) * gb] = out.reshape(gb, tt, D_MODEL)

    q = project_q(0)
    for g in range(groups):
        q_next = project_q(g + 1) if g + 1 < groups else None
        attend(g, q)
        project_out(g)
        q = q_next


def _ffn_kernel(x_ref, cf0_ref, gffn_ref, wup_ref, cfw_ref, wdown_ref, gfin_ref, xo_ref, cfo_ref,
                cf_tail, *rest, nb, tt, final_norm, scatter_out):
    rows = nb * tt
    seg = tt // SUBLANES

    @pl.when(pl.program_id(1) == 0)
    def _():
        cf_tail[...] = cf0_ref[...]

    x = x_ref[...].reshape(rows, D_MODEL)
    xn = _rms(x, gffn_ref[...]).astype(BF16)
    gu = _dot(xn, wup_ref[...])
    gconv = _causal_taps(gu[:, :D_FF].reshape(nb, seg, SUBLANES, D_FF), cf_tail, cfw_ref[...])
    cfo_ref[...] = cf_tail[...]
    act = (jax.nn.silu(gconv.reshape(rows, D_FF)) * gu[:, D_FF:]).astype(BF16)
    out = x + _dot(act, wdown_ref[...])
    if final_norm:
        out = _rms(out, gfin_ref[...])
    if not scatter_out:
        xo_ref[...] = out.reshape(nb, tt, D_MODEL)
        return

    out_buf, out_sem = rest

    def flush(from_slot):
        return _tile_copies(xo_ref, out_buf.at[from_slot], out_sem.at[from_slot],
                            pl.program_id(0), pl.program_id(1), nb, tt, to_vmem=False)

    def fill(slot):
        out_buf[slot] = out.reshape(nb, seg, SUBLANES, D_MODEL)

    _flushed(flush, fill)


def _kv_kernel(mem_ref, wk_ref, wv_ref, ko_hbm, vo_hbm, kb_ref, vb_ref, kv_buf, sem):
    m = mem_ref[0].astype(BF16)
    k = _dot(m, wk_ref[...])
    v = _dot(m, wv_ref[...])
    kb_ref[...] = k.astype(BF16)
    vb_ref[...] = v.astype(BF16)

    def flush(from_slot):
        return _head_copies((ko_hbm, vo_hbm), kv_buf.at[from_slot], sem.at[from_slot],
                            pl.program_id(0), pl.program_id(1), to_vmem=False)

    def fill(slot):
        kv_buf[slot, 0] = k
        kv_buf[slot, 1] = v

    _flushed(flush, fill)


def _kv_cast_kernel(k_hbm, v_hbm, kb_ref, vb_ref, kv_buf, sem):
    slot = _prefetched(lambda l, b, into: _head_copies((k_hbm, v_hbm), kv_buf.at[into], sem.at[into],
                                                       l, b, to_vmem=True))
    kb_ref[...] = kv_buf[slot, 0].astype(BF16)
    vb_ref[...] = kv_buf[slot, 1].astype(BF16)


def _whole():
    return pl.BlockSpec(memory_space=pltpu.MemorySpace.VMEM)


def _of_layer(arr, l):
    zeros = (0,) * (arr.ndim - 1)
    return pl.BlockSpec((None,) + arr.shape[1:], lambda b, t: (l,) + zeros, pipeline_mode=pl.Buffered(1))


def _params():
    return pltpu.CompilerParams(dimension_semantics=("arbitrary", "arbitrary"),
                                vmem_limit_bytes=VMEM_LIMIT_BYTES)


def _tile_specs(nb, tt):
    x_spec = pl.BlockSpec((nb, tt, D_MODEL), lambda b, t: (b, t, 0))

    def state_spec(n, width):
        return pl.BlockSpec((nb, n, SUBLANES, width), lambda b, t: (b, 0, 0, 0))

    return x_spec, state_spec


def _mixer_call(x, ca0, h0, cb0, p, l, nb, tt, chunk, emit_vc, gather_x):
    bsz, seq, _ = x.shape
    layer_params = [p[name] for name in ("g_mix", "w_in", "conv_a_w", "conv_a_b", "w_gates", "b_rg", "b_ig",
                                         "lam", "conv_b_w", "g_v")]
    mix_params = [p[name] for name in ("w_s", "b_s_full", "w_out")]
    x_spec, state_spec = _tile_specs(nb, tt)
    h_spec = pl.BlockSpec((nb, SUBLANES, A_WIDTH), lambda b, t: (b, 0, 0))
    out_shape = [jax.ShapeDtypeStruct(x.shape, F32),
                 jax.ShapeDtypeStruct(ca0.shape, F32),
                 jax.ShapeDtypeStruct(h0.shape, F32),
                 jax.ShapeDtypeStruct(cb0.shape, F32)]
    out_specs = [x_spec, state_spec(A_CONV - 1, A_WIDTH), h_spec, state_spec(B_CONV - 1, B_WIDTH)]
    if emit_vc:
        out_shape.append(jax.ShapeDtypeStruct((bsz, seq, C_WIDTH), F32))
        out_specs.append(pl.BlockSpec((nb, tt, C_WIDTH), lambda b, t: (b, t, 0)))
    gb = nb if seq == tt else 1
    kern = functools.partial(_mixer_kernel, nb=nb, gb=gb, tt=tt, chunk=chunk, emit_vc=emit_vc,
                             gather_x=gather_x)
    gather_scratch = [pltpu.VMEM((2, nb, tt // SUBLANES, SUBLANES, D_MODEL), F32),
                      pltpu.SemaphoreType.DMA((2,))] if gather_x else []
    return pl.pallas_call(
        kern,
        grid=(bsz // nb, seq // tt),
        in_specs=([pl.BlockSpec(memory_space=pl.ANY) if gather_x else x_spec,
                   state_spec(A_CONV - 1, A_WIDTH), h_spec, state_spec(B_CONV - 1, B_WIDTH)]
                  + [_of_layer(a, l) for a in layer_params] + [_whole()]
                  + [_of_layer(a, l) for a in mix_params]),
        out_specs=out_specs,
        out_shape=out_shape,
        scratch_shapes=[pltpu.VMEM((nb, A_CONV - 1, SUBLANES, A_WIDTH), F32),
                        pltpu.VMEM((nb, B_CONV - 1, SUBLANES, B_WIDTH), F32),
                        pltpu.VMEM((nb, SUBLANES, A_WIDTH), F32),
                        pltpu.VMEM((C_HEADS, tt, tt), BF16),
                        pltpu.VMEM((nb * tt, D_MODEL), BF16),
                        pltpu.VMEM((Z_SLOTS, gb * tt, IN_COLS), F32)] + gather_scratch,
        compiler_params=_params(),
        name="mixer",
    )(x, ca0, h0, cb0, *layer_params, p["group_mean"], *mix_params)


def _attn_call(x, kb, vb, p, l, nb, tt):
    bsz, seq, _ = x.shape
    x_spec, _ = _tile_specs(nb, tt)
    kv_spec = pl.BlockSpec((None, nb, N_MEM, D_MODEL), lambda b, t: (l, b, 0, 0))
    weights = [p["g_x"], p["w_q"], p["w_o"]]
    gb = nb if seq == tt else 1
    kern = functools.partial(_attn_kernel, nb=nb, gb=gb, tt=tt)
    return pl.pallas_call(
        kern,
        grid=(bsz // nb, seq // tt),
        in_specs=[x_spec, kv_spec, kv_spec] + [_of_layer(a, l) for a in weights],
        out_specs=x_spec,
        out_shape=jax.ShapeDtypeStruct(x.shape, F32),
        scratch_shapes=[pltpu.VMEM((nb * tt, D_MODEL), BF16)],
        compiler_params=_params(),
        name="attn",
    )(x, kb, vb, *weights)


def _ffn_call(x, cf0, p, l, nb, tt, final_norm, scatter_out):
    bsz, seq, _ = x.shape
    x_spec, state_spec = _tile_specs(nb, tt)
    weights = [p["g_ffn"], p["w_up"], p["conv_f_w"], p["w_down"]]
    kern = functools.partial(_ffn_kernel, nb=nb, tt=tt, final_norm=final_norm, scatter_out=scatter_out)
    scatter_scratch = [pltpu.VMEM((2, nb, tt // SUBLANES, SUBLANES, D_MODEL), F32),
                       pltpu.SemaphoreType.DMA((2,))] if scatter_out else []
    return pl.pallas_call(
        kern,
        grid=(bsz // nb, seq // tt),
        in_specs=[x_spec, state_spec(FFN_CONV - 1, D_FF)] + [_of_layer(a, l) for a in weights] + [_whole()],
        out_specs=[pl.BlockSpec(memory_space=pl.ANY) if scatter_out else x_spec,
                   state_spec(FFN_CONV - 1, D_FF)],
        out_shape=[jax.ShapeDtypeStruct(x.shape, F32), jax.ShapeDtypeStruct(cf0.shape, F32)],
        scratch_shapes=[pltpu.VMEM((nb, FFN_CONV - 1, SUBLANES, D_FF), F32)] + scatter_scratch,
        compiler_params=_params(),
        name="ffn",
    )(x, cf0, *weights, p["g_final"])


def _kv_call(mem, w_k, w_v):
    bsz = mem.shape[0]
    depth = w_k.shape[0]
    w_spec = pl.BlockSpec((None, D_MODEL, D_MODEL), lambda l, b: (l, 0, 0))
    out5 = pl.BlockSpec(memory_space=pl.ANY)
    out4 = pl.BlockSpec((None, None, N_MEM, D_MODEL), lambda l, b: (l, b, 0, 0))
    shape5 = jax.ShapeDtypeStruct((depth, bsz, N_MEM, X_HEADS, X_HD), F32)
    shape4 = jax.ShapeDtypeStruct((depth, bsz, N_MEM, D_MODEL), BF16)
    return pl.pallas_call(
        _kv_kernel,
        grid=(depth, bsz),
        in_specs=[pl.BlockSpec((1, N_MEM, D_MODEL), lambda l, b: (b, 0, 0)), w_spec, w_spec],
        out_specs=[out5, out5, out4, out4],
        out_shape=[shape5, shape5, shape4, shape4],
        scratch_shapes=[pltpu.VMEM((2, 2, N_MEM, D_MODEL), F32), pltpu.SemaphoreType.DMA((2,))],
        compiler_params=_params(),
        name="memory_kv",
    )(mem, w_k, w_v)


def _kv_cast_call(cache_k, cache_v):
    depth, bsz = cache_k.shape[:2]
    in_spec = pl.BlockSpec(memory_space=pl.ANY)
    out_spec = pl.BlockSpec((None, None, N_MEM, D_MODEL), lambda l, b: (l, b, 0, 0))
    shape = jax.ShapeDtypeStruct((depth, bsz, N_MEM, D_MODEL), BF16)
    return pl.pallas_call(
        _kv_cast_kernel,
        grid=(depth, bsz),
        in_specs=[in_spec, in_spec],
        out_specs=[out_spec, out_spec],
        out_shape=[shape, shape],
        scratch_shapes=[pltpu.VMEM((2, 2, N_MEM, D_MODEL), F32), pltpu.SemaphoreType.DMA((2,))],
        compiler_params=_params(),
        name="cache_kv_cast",
    )(cache_k, cache_v)


def _block_diag(blocks):
    n, r, c = blocks.shape[-3:]
    eye = jnp.eye(n, dtype=blocks.dtype)
    return (blocks[..., :, :, None, :] * eye[:, None, :, None]).reshape(blocks.shape[:-3] + (n * r, n * c))


def _tile_times(tt):
    i = jnp.arange(tt)
    return (i % SUBLANES) * (tt // SUBLANES) + i // SUBLANES


def _to_segments(x, tt):
    bsz, seq, c = x.shape
    return x.reshape(bsz, seq // tt, SUBLANES, tt // SUBLANES, c).swapaxes(2, 3).reshape(bsz, seq, c)


def _from_segments(x, tt):
    bsz, seq, c = x.shape
    return x.reshape(bsz, seq // tt, tt // SUBLANES, SUBLANES, c).swapaxes(2, 3).reshape(bsz, seq, c)


def _on_sublanes(state):
    return jnp.broadcast_to(state[..., None, :], state.shape[:-1] + (SUBLANES, state.shape[-1]))


def _shared_params(g_mix, w_in, conv_a_w, conv_a_b, w_rg, b_rg, w_ig, b_ig, lam, conv_b_w, g_v, w_out,
                   g_x, w_q, w_o, g_ffn, w_up, conv_f_w, w_down, g_final):
    per_half = A_HEADS // 2
    gates = [jnp.concatenate([_block_diag(w_rg[:, j * per_half:(j + 1) * per_half]),
                              _block_diag(w_ig[:, j * per_half:(j + 1) * per_half])], axis=-1)
             for j in range(2)]
    group = jnp.arange(C_WIDTH) // C_HD

    def row(a):
        return a[:, None, :]

    def taps(a):
        return a[:, :, None, :]

    return dict(
        g_mix=row(g_mix), w_in=w_in.astype(BF16), conv_a_w=taps(conv_a_w), conv_a_b=row(conv_a_b),
        w_gates=jnp.stack(gates, axis=1).astype(BF16), b_rg=row(b_rg), b_ig=row(b_ig), lam=row(lam),
        conv_b_w=taps(conv_b_w), g_v=row(g_v),
        group_mean=((group[:, None] == group[None, :]).astype(F32) / C_HD).astype(BF16),
        w_out=w_out.astype(BF16), g_x=row(g_x), w_q=w_q.astype(BF16), w_o=w_o.astype(BF16),
        g_ffn=row(g_ffn), w_up=w_up.astype(BF16), conv_f_w=taps(conv_f_w), w_down=w_down.astype(BF16),
        g_final=g_final[None])


def _tile_params(tt, chunk, w_s, b_s):
    pos = _tile_times(tt) % chunk
    return dict(w_s=w_s[:, :, pos][:, :, :, pos],
                b_s_full=jnp.repeat(jnp.swapaxes(b_s[:, :, pos], 1, 2), C_HD, axis=2))


def _tiles(bsz, seq):
    if seq <= MLP_CHUNK:
        return bsz, seq
    return PROMPT_BATCH_TILE, PROMPT_TIME_TILE


def _trunk(x, kb, vb, states, p, emit_vc):
    bsz, seq, _ = x.shape
    nb, tt = _tiles(bsz, seq)
    chunk = min(seq, MLP_CHUNK)
    by_dma = seq > tt
    if not by_dma:
        x = _to_segments(x, tt)
    new_states = [[] for _ in range(4)]
    vcs = []
    for l in range(DEPTH):
        last = l == DEPTH - 1
        ca0, h0, cb0, cf0 = (s[l] for s in states)
        mix = _mixer_call(x, ca0, h0, cb0, p, l, nb, tt, chunk, emit_vc, by_dma and l == 0)
        x = _attn_call(mix[0], kb, vb, p, l, nb, tt)
        x, cf = _ffn_call(x, cf0, p, l, nb, tt, last, by_dma and last)
        for acc, val in zip(new_states, (mix[1], mix[2], mix[3], cf)):
            acc.append(val)
        if emit_vc:
            vcs.append(mix[4])
    new_states = tuple(jnp.stack(s)[..., SUBLANES - 1, :] for s in new_states)
    vc = _from_segments(jnp.concatenate(vcs, axis=0), tt).reshape(DEPTH, bsz, seq, C_WIDTH) if emit_vc else None
    return (x if by_dma else _from_segments(x, tt)), new_states, vc


def kernel(x_prompt, x_sample, mem_prompt, cache_mem_k, cache_mem_v, state_conv_a, state_h_a, state_conv_b, state_conv_ffn, g_mix, w_in, conv_a_w, conv_a_b, w_rg, b_rg, w_ig, b_ig, lam, conv_b_w, g_v, w_s, b_s, w_out, g_x, w_q, w_k, w_v, w_o, g_ffn, w_up, conv_f_w, w_down, g_final):
    bp, seq_p, _ = x_prompt.shape
    bs, seq_s, _ = x_sample.shape
    shared = _shared_params(g_mix, w_in, conv_a_w, conv_a_b, w_rg, b_rg, w_ig, b_ig, lam, conv_b_w, g_v,
                            w_out, g_x, w_q, w_o, g_ffn, w_up, conv_f_w, w_down, g_final)

    def tile_params(bsz, seq):
        return dict(shared, **_tile_params(_tiles(bsz, seq)[1], min(seq, MLP_CHUNK), w_s, b_s))

    mem_k, mem_v, kb, vb = _kv_call(mem_prompt, w_k.astype(BF16), w_v.astype(BF16))
    zero_states = tuple(jnp.zeros((DEPTH, bp) + shape + (SUBLANES, width), F32)
                        for shape, width in (((A_CONV - 1,), A_WIDTH), ((), A_WIDTH),
                                             ((B_CONV - 1,), B_WIDTH), ((FFN_CONV - 1,), D_FF)))
    y_p, states_p, _ = _trunk(x_prompt, kb, vb, zero_states, tile_params(bp, seq_p), False)
    carried = tuple(_on_sublanes(s) for s in (state_conv_a, state_h_a, state_conv_b, state_conv_ffn))
    cache_kb, cache_vb = _kv_cast_call(cache_mem_k, cache_mem_v)
    y_s, states_s, vc_s = _trunk(x_sample, cache_kb, cache_vb, carried, tile_params(bs, seq_s), True)
    return (y_p, y_s) + states_p + (mem_k, mem_v) + states_s + (vc_s,)
```

```python
import functools
import math

import jax
import jax.numpy as jnp
from jax import lax
from jax.experimental import pallas as pl
from jax.experimental.pallas import tpu as pltpu

D_MODEL = 1024
DEPTH = 2
A_WIDTH = 512
A_HEADS = 8
A_HD = 64
A_CONV = 4
RG_C = 8.0
B_WIDTH = 256
B_CONV = 3
C_WIDTH = 256
C_HEADS = 4
C_HD = 64
MLP_CHUNK = 128
N_MEM = 256
X_HEADS = 4
X_HD = 256
D_FF = 2816
FFN_CONV = 3
EPS = 1e-6
IN_COLS = 2 * A_WIDTH + 3 * B_WIDTH + 2 * C_WIDTH

SUBLANES = 8
LANES = 128
PROMPT_TIME_TILE = 256
PROMPT_BATCH_TILE = 4
MIXER_GROUP = 2
ATTN_TIME_TILE = 1024
Z_SLOTS = 3
PROJ_PARTS = 9
VMEM_LIMIT_BYTES = 56 * 1024 * 1024

LOG2_E = math.log2(math.e)
GELU_C0 = math.sqrt(2.0 / math.pi)
GELU_C1 = 0.044715

F32 = jnp.float32
BF16 = jnp.bfloat16


def _dot(a, b):
    return jnp.dot(a, b, preferred_element_type=F32)


def _rms(x, g):
    ms = jnp.mean(x * x, axis=-1, keepdims=True)
    return x * lax.rsqrt(ms + EPS) * g


def _gelu(x):
    half_x = 0.5 * x
    inner = (x * x * (GELU_C0 * GELU_C1) + GELU_C0) * x
    return half_x * jnp.tanh(inner) + half_x


def _sublane_iota(shape):
    return lax.broadcasted_iota(jnp.int32, shape, len(shape) - 2)


def _causal_taps(cur, tail_ref, w):
    n_prev = tail_ref.shape[1]
    seg = cur.shape[1]
    last = cur[:, seg - n_prev:]
    wrapped = pltpu.roll(jnp.where(_sublane_iota(last.shape) == SUBLANES - 1, tail_ref[...], last), 1, 2)
    tail_ref[...] = last
    xp = jnp.concatenate([wrapped, cur], axis=1)
    acc = xp[:, 0:seg] * w[0]
    for j in range(1, n_prev + 1):
        acc = acc + xp[:, j:j + seg] * w[j]
    return acc


def _segment_scan(a, b, h_ref):
    seg = a.shape[1]
    local, decay = b[:, 0], a[:, 0]
    locals_, decays = [local], [decay]
    for s in range(1, seg):
        local = a[:, s] * local + b[:, s]
        decay = a[:, s] * decay
        locals_.append(local)
        decays.append(decay)
    h_in = h_ref[:, SUBLANES - 1:SUBLANES, :]
    row = _sublane_iota(local.shape)
    seg_a, seg_b = decay, local
    d = 1
    while d < SUBLANES:
        a_prev = pltpu.roll(seg_a, d, 1)
        b_prev = pltpu.roll(seg_b, d, 1)
        keep = row >= d
        seg_b = jnp.where(keep, seg_a * b_prev + seg_b, seg_b)
        seg_a = jnp.where(keep, seg_a * a_prev, seg_a)
        d *= 2
    h_end = seg_a * h_in + seg_b
    h_ref[...] = h_end
    h_start = jnp.where(row == 0, h_in, pltpu.roll(h_end, 1, 1))
    return jnp.stack([locals_[s] + decays[s] * h_start for s in range(seg)], axis=1)


def _grid_step():
    return (pl.program_id(0) * pl.num_programs(1) + pl.program_id(1),
            pl.num_programs(0) * pl.num_programs(1))


def _tile_copies(hbm_ref, buf_ref, sem, b, t, nb, tt, to_vmem):
    seg = tt // SUBLANES
    copies = []
    for r in range(SUBLANES):
        hbm = hbm_ref.at[pl.ds(b * nb, nb), pl.ds(t * tt + r * seg, seg), :]
        vmem = buf_ref.at[:, :, r, :]
        copies.append(pltpu.make_async_copy(hbm, vmem, sem) if to_vmem
                      else pltpu.make_async_copy(vmem, hbm, sem))
    return copies


def _head_copies(hbm_refs, buf_ref, sem, l, b, to_vmem):
    copies = []
    for j, hbm_ref in enumerate(hbm_refs):
        for hd in range(X_HEADS):
            hbm = hbm_ref.at[l, b, :, hd, :]
            vmem = buf_ref.at[j, :, pl.ds(hd * X_HD, X_HD)]
            copies.append(pltpu.make_async_copy(hbm, vmem, sem) if to_vmem
                          else pltpu.make_async_copy(vmem, hbm, sem))
    return copies


def _next_grid_index():
    wraps = pl.program_id(1) + 1 == pl.num_programs(1)
    return pl.program_id(0) + wraps.astype(jnp.int32), jnp.where(wraps, 0, pl.program_id(1) + 1)


def _prefetched(fetch):
    step, n_steps = _grid_step()
    slot = lax.rem(step, 2)

    @pl.when(step == 0)
    def _():
        for c in fetch(pl.program_id(0), pl.program_id(1), slot):
            c.start()

    @pl.when(step + 1 < n_steps)
    def _():
        for c in fetch(*_next_grid_index(), 1 - slot):
            c.start()

    for c in fetch(pl.program_id(0), pl.program_id(1), slot):
        c.wait()
    return slot


def _flushed(flush, fill):
    step, n_steps = _grid_step()
    slot = lax.rem(step, 2)

    @pl.when(step >= 2)
    def _():
        for c in flush(slot):
            c.wait()

    fill(slot)
    for c in flush(slot):
        c.start()

    @pl.when(step == n_steps - 1)
    def _():
        @pl.when(n_steps > 1)
        def _():
            for c in flush(1 - slot):
                c.wait()
        for c in flush(slot):
            c.wait()


def _mixer_kernel(x_ref, ca0_ref, h0_ref, cb0_ref, gmix_ref, win_ref, caw_ref, cab_ref, wg_ref,
                  brg_ref, big_ref, lam_ref, cbw_ref, gv_ref, gsum_ref, ws_ref, bsf_ref, wout_ref,
                  xo_ref, cao_ref, ho_ref, cbo_ref, *rest, nb, gb, tt, chunk, emit_vc, gather_x):
    if gather_x:
        *rest, x_buf, x_sem = rest
    if emit_vc:
        vco_ref, ca_tail, cb_tail, h_state, wmix, ycat, zbuf = rest
    else:
        ca_tail, cb_tail, h_state, wmix, ycat, zbuf = rest
    rows = gb * tt
    seg = tt // SUBLANES
    groups = nb // gb

    if gather_x:
        slot = _prefetched(lambda b, t, into: _tile_copies(x_ref, x_buf.at[into], x_sem.at[into],
                                                           b, t, nb, tt, to_vmem=True))

    def read_x(g):
        if gather_x:
            return x_buf[slot, g * gb:(g + 1) * gb].reshape(rows, D_MODEL)
        return x_ref[g * gb:(g + 1) * gb].reshape(rows, D_MODEL)

    @pl.when(pl.program_id(1) == 0)
    def _():
        ca_tail[...] = ca0_ref[...]
        cb_tail[...] = cb0_ref[...]
        h_state[...] = h0_ref[...]
        ti = lax.broadcasted_iota(jnp.int32, (tt, tt), 0)
        tj = lax.broadcasted_iota(jnp.int32, (tt, tt), 1)
        time_i = (ti % SUBLANES) * seg + ti // SUBLANES
        time_j = (tj % SUBLANES) * seg + tj // SUBLANES
        allowed = (time_i >= time_j) & (time_i // chunk == time_j // chunk)
        for hd in range(C_HEADS):
            wmix[hd] = jnp.where(allowed, ws_ref[hd], 0.0).astype(BF16)

    def project_in(g):
        xn = _rms(read_x(g), gmix_ref[...]).astype(BF16)
        z = zbuf.at[g % Z_SLOTS]
        width = IN_COLS // PROJ_PARTS

        def part(c):
            z[:, c * width:(c + 1) * width] = _dot(xn, win_ref[:, c * width:(c + 1) * width])

        return z, [functools.partial(part, c) for c in range(PROJ_PARTS)]

    col = {}
    o = 0
    for name, width in (("xa", A_WIDTH), ("ga", A_WIDTH), ("xb", B_WIDTH), ("gb", B_WIDTH),
                        ("gc", B_WIDTH), ("uc", C_WIDTH), ("vc", C_WIDTH)):
        col[name] = slice(o, o + width)
        o += width
    half = A_WIDTH // 2

    def mix_matmuls(g, z):
        seqs = slice(g * gb, (g + 1) * gb)
        out = {}

        def gates_and_norm():
            xc = _causal_taps(z[:, col["xa"]].reshape(gb, seg, SUBLANES, A_WIDTH), ca_tail.at[seqs],
                              caw_ref[...])
            xc2 = (xc + cab_ref[...]).reshape(rows, A_WIDTH)
            xcb = xc2.astype(BF16)
            out["xc2"] = xc2
            out["gz0"] = _dot(xcb[:, :half], wg_ref[0])
            out["gz1"] = _dot(xcb[:, half:], wg_ref[1])
            v = _gelu(z[:, col["vc"]])
            sq = v * v
            sq_hi = sq.astype(BF16)
            sq_lo = (sq - sq_hi.astype(F32)).astype(BF16)
            out["v"] = v
            out["ms"] = _dot(sq_hi, gsum_ref[...]) + _dot(sq_lo, gsum_ref[...])

        def spatial_mix():
            vn = out.pop("v") * lax.rsqrt(out.pop("ms") + EPS) * gv_ref[...]
            if emit_vc:
                vco_ref[seqs] = vn.reshape(gb, tt, C_WIDTH)
            vnb = vn.astype(BF16)
            first_head = lax.broadcasted_iota(jnp.int32, (tt, LANES), 1) < C_HD
            mixed = []
            for b in range(gb):
                pieces = []
                for pair in range(C_WIDTH // LANES):
                    vp = vnb[b * tt:(b + 1) * tt, pair * LANES:(pair + 1) * LANES]
                    pieces.append(jnp.where(first_head, _dot(wmix[2 * pair], vp), _dot(wmix[2 * pair + 1], vp)))
                mixed.append(jnp.concatenate(pieces, axis=1) + bsf_ref[...])
            out["mixed"] = jnp.concatenate(mixed, axis=0)

        return out, [gates_and_norm, spatial_mix]

    def mix_elementwise(g, z, xc2, gz0, gz1, mixed):
        seqs = slice(g * gb, (g + 1) * gb)
        y = ycat.at[g * rows:(g + 1) * rows]
        r = jax.nn.sigmoid(jnp.concatenate([gz0[:, :half], gz1[:, :half]], axis=1) + brg_ref[...])
        gi = jax.nn.sigmoid(jnp.concatenate([gz0[:, half:], gz1[:, half:]], axis=1) + big_ref[...])
        neg_lam = -lam_ref[...]
        softplus = jnp.maximum(neg_lam, 0.0) + jnp.log1p(jnp.exp(-jnp.abs(neg_lam)))
        decay_rate = RG_C * softplus
        a = jnp.exp2((-LOG2_E) * decay_rate * r)
        one_minus_a2 = (1.0 + a * a) * jnp.tanh(decay_rate * r)
        root = jnp.where(one_minus_a2 > 0.0, one_minus_a2 * lax.rsqrt(one_minus_a2), 0.0)
        bterm = root * (gi * xc2)
        h = _segment_scan(a.reshape(gb, seg, SUBLANES, A_WIDTH),
                          bterm.reshape(gb, seg, SUBLANES, A_WIDTH), h_state.at[seqs])
        y[:, 0:A_WIDTH] = (_gelu(z[:, col["ga"]]) * h.reshape(rows, A_WIDTH)).astype(BF16)
        zb = _causal_taps((z[:, col["gc"]] * z[:, col["xb"]]).reshape(gb, seg, SUBLANES, B_WIDTH),
                          cb_tail.at[seqs], cbw_ref[...])
        y[:, A_WIDTH:A_WIDTH + B_WIDTH] = (z[:, col["gb"]] * zb.reshape(rows, B_WIDTH)).astype(BF16)
        y[:, A_WIDTH + B_WIDTH:] = (_gelu(z[:, col["uc"]]) * mixed).astype(BF16)

    def project_out(g):
        out = read_x(g) + _dot(ycat[g * rows:(g + 1) * rows], wout_ref[...])
        xo_ref[g * gb:(g + 1) * gb] = out.reshape(gb, tt, D_MODEL)

    def interleave(big_steps, small_steps):
        stride = max(1, len(big_steps) // max(1, len(small_steps)))
        pending = list(small_steps)
        for i, step in enumerate(big_steps):
            step()
            if pending and (i + 1) % stride == 0:
                pending.pop(0)()
        for step in pending:
            step()

    zs, small = {}, {}
    zs[0], parts = project_in(0)
    interleave(parts, [])
    small[0], steps = mix_matmuls(0, zs[0])
    if groups > 1:
        zs[1], parts = project_in(1)
        interleave(parts, steps)
    else:
        interleave([], steps)
    for g in range(groups):
        parts, steps = [], []
        if g + 2 < groups:
            zs[g + 2], parts = project_in(g + 2)
        if g + 1 < groups:
            small[g + 1], steps = mix_matmuls(g + 1, zs[g + 1])
        interleave(parts, steps)
        res = small.pop(g)
        mix_elementwise(g, zs.pop(g), res["xc2"], res["gz0"], res["gz1"], res["mixed"])
        project_out(g)
    cao_ref[...] = ca_tail[...]
    ho_ref[...] = h_state[...]
    cbo_ref[...] = cb_tail[...]


def _attn_kernel(x_ref, kb, vb, gx_ref, wq_ref, wo_ref, xo_ref, ocat, *, nb, gb, tt):
    rows = gb * tt
    groups = nb // gb

    def project_q(g):
        x = x_ref[g * gb:(g + 1) * gb].reshape(rows, D_MODEL)
        xn = _rms(x, gx_ref[...]).astype(BF16)
        return (_dot(xn, wq_ref[...]) * (X_HD ** -0.5)).astype(BF16)

    def attend(g, q):
        for b in range(gb):
            seq = g * gb + b
            for hd in range(X_HEADS):
                cols = slice(hd * X_HD, (hd + 1) * X_HD)
                s = lax.dot_general(q[b * tt:(b + 1) * tt, cols], kb[seq, :, cols], (((1,), (1,)), ((), ())),
                                    preferred_element_type=F32)
                e = jnp.exp(s - jnp.max(s, axis=-1, keepdims=True))
                denom = jnp.sum(e, axis=-1, keepdims=True)
                oh = _dot(e.astype(BF16), vb[seq, :, cols]) / denom
                ocat[seq * tt:(seq + 1) * tt, cols] = oh.astype(BF16)

    def project_out(g):
        x = x_ref[g * gb:(g + 1) * gb].reshape(rows, D_MODEL)
        out = x + _dot(ocat[g * rows:(g + 1) * rows], wo_ref[...])
        xo_ref[g * gb:(g + 1) * gb] = out.reshape(gb, tt, D_MODEL)

    q = project_q(0)
    for g in range(groups):
        q_next = project_q(g + 1) if g + 1 < groups else None
        attend(g, q)
        project_out(g)
        q = q_next


def _ffn_kernel(x_ref, cf0_ref, gffn_ref, wup_ref, cfw_ref, wdown_ref, gfin_ref, xo_ref, cfo_ref,
                cf_tail, *rest, nb, tt, final_norm, scatter_out):
    rows = nb * tt
    seg = tt // SUBLANES

    @pl.when(pl.program_id(1) == 0)
    def _():
        cf_tail[...] = cf0_ref[...]

    x = x_ref[...].reshape(rows, D_MODEL)
    xn = _rms(x, gffn_ref[...]).astype(BF16)
    gu = _dot(xn, wup_ref[...])
    gconv = _causal_taps(gu[:, :D_FF].reshape(nb, seg, SUBLANES, D_FF), cf_tail, cfw_ref[...])
    cfo_ref[...] = cf_tail[...]
    act = (jax.nn.silu(gconv.reshape(rows, D_FF)) * gu[:, D_FF:]).astype(BF16)
    out = x + _dot(act, wdown_ref[...])
    if final_norm:
        out = _rms(out, gfin_ref[...])
    if not scatter_out:
        xo_ref[...] = out.reshape(nb, tt, D_MODEL)
        return

    out_buf, out_sem = rest

    def flush(from_slot):
        return _tile_copies(xo_ref, out_buf.at[from_slot], out_sem.at[from_slot],
                            pl.program_id(0), pl.program_id(1), nb, tt, to_vmem=False)

    def fill(slot):
        out_buf[slot] = out.reshape(nb, seg, SUBLANES, D_MODEL)

    _flushed(flush, fill)


def _kv_kernel(mem_ref, wk_ref, wv_ref, ko_hbm, vo_hbm, kb_ref, vb_ref, kv_buf, sem):
    m = mem_ref[0].astype(BF16)
    k = _dot(m, wk_ref[...])
    v = _dot(m, wv_ref[...])
    kb_ref[...] = k.astype(BF16)
    vb_ref[...] = v.astype(BF16)

    def flush(from_slot):
        return _head_copies((ko_hbm, vo_hbm), kv_buf.at[from_slot], sem.at[from_slot],
                            pl.program_id(0), pl.program_id(1), to_vmem=False)

    def fill(slot):
        kv_buf[slot, 0] = k
        kv_buf[slot, 1] = v

    _flushed(flush, fill)


def _kv_cast_kernel(k_hbm, v_hbm, kb_ref, vb_ref, kv_buf, sem):
    slot = _prefetched(lambda l, b, into: _head_copies((k_hbm, v_hbm), kv_buf.at[into], sem.at[into],
                                                       l, b, to_vmem=True))
    kb_ref[...] = kv_buf[slot, 0].astype(BF16)
    vb_ref[...] = kv_buf[slot, 1].astype(BF16)


def _whole():
    return pl.BlockSpec(memory_space=pltpu.MemorySpace.VMEM)


def _of_layer(arr, l):
    zeros = (0,) * (arr.ndim - 1)
    return pl.BlockSpec((None,) + arr.shape[1:], lambda b, t: (l,) + zeros, pipeline_mode=pl.Buffered(1))


def _params():
    return pltpu.CompilerParams(dimension_semantics=("arbitrary", "arbitrary"),
                                vmem_limit_bytes=VMEM_LIMIT_BYTES)


def _tile_specs(nb, tt):
    x_spec = pl.BlockSpec((nb, tt, D_MODEL), lambda b, t: (b, t, 0))

    def state_spec(n, width):
        return pl.BlockSpec((nb, n, SUBLANES, width), lambda b, t: (b, 0, 0, 0))

    return x_spec, state_spec


def _mixer_call(x, ca0, h0, cb0, p, l, nb, tt, chunk, emit_vc, gather_x):
    bsz, seq, _ = x.shape
    layer_params = [p[name] for name in ("g_mix", "w_in", "conv_a_w", "conv_a_b", "w_gates", "b_rg", "b_ig",
                                         "lam", "conv_b_w", "g_v")]
    mix_params = [p[name] for name in ("w_s", "b_s_full", "w_out")]
    x_spec, state_spec = _tile_specs(nb, tt)
    h_spec = pl.BlockSpec((nb, SUBLANES, A_WIDTH), lambda b, t: (b, 0, 0))
    out_shape = [jax.ShapeDtypeStruct(x.shape, F32),
                 jax.ShapeDtypeStruct(ca0.shape, F32),
                 jax.ShapeDtypeStruct(h0.shape, F32),
                 jax.ShapeDtypeStruct(cb0.shape, F32)]
    out_specs = [x_spec, state_spec(A_CONV - 1, A_WIDTH), h_spec, state_spec(B_CONV - 1, B_WIDTH)]
    if emit_vc:
        out_shape.append(jax.ShapeDtypeStruct((bsz, seq, C_WIDTH), F32))
        out_specs.append(pl.BlockSpec((nb, tt, C_WIDTH), lambda b, t: (b, t, 0)))
    gb = nb if seq == tt else MIXER_GROUP
    kern = functools.partial(_mixer_kernel, nb=nb, gb=gb, tt=tt, chunk=chunk, emit_vc=emit_vc,
                             gather_x=gather_x)
    gather_scratch = [pltpu.VMEM((2, nb, tt // SUBLANES, SUBLANES, D_MODEL), F32),
                      pltpu.SemaphoreType.DMA((2,))] if gather_x else []
    return pl.pallas_call(
        kern,
        grid=(bsz // nb, seq // tt),
        in_specs=([pl.BlockSpec(memory_space=pl.ANY) if gather_x else x_spec,
                   state_spec(A_CONV - 1, A_WIDTH), h_spec, state_spec(B_CONV - 1, B_WIDTH)]
                  + [_of_layer(a, l) for a in layer_params] + [_whole()]
                  + [_of_layer(a, l) for a in mix_params]),
        out_specs=out_specs,
        out_shape=out_shape,
        scratch_shapes=[pltpu.VMEM((nb, A_CONV - 1, SUBLANES, A_WIDTH), F32),
                        pltpu.VMEM((nb, B_CONV - 1, SUBLANES, B_WIDTH), F32),
                        pltpu.VMEM((nb, SUBLANES, A_WIDTH), F32),
                        pltpu.VMEM((C_HEADS, tt, tt), BF16),
                        pltpu.VMEM((nb * tt, D_MODEL), BF16),
                        pltpu.VMEM((Z_SLOTS, gb * tt, IN_COLS), F32)] + gather_scratch,
        compiler_params=_params(),
        name="mixer",
    )(x, ca0, h0, cb0, *layer_params, p["group_mean"], *mix_params)


def _attn_call(x, kb, vb, p, l, nb, tt):
    bsz, seq, _ = x.shape
    x_spec, _ = _tile_specs(nb, tt)
    kv_spec = pl.BlockSpec((None, nb, N_MEM, D_MODEL), lambda b, t: (l, b, 0, 0))
    weights = [p["g_x"], p["w_q"], p["w_o"]]
    gb = nb if seq == tt else 1
    kern = functools.partial(_attn_kernel, nb=nb, gb=gb, tt=tt)
    return pl.pallas_call(
        kern,
        grid=(bsz // nb, seq // tt),
        in_specs=[x_spec, kv_spec, kv_spec] + [_of_layer(a, l) for a in weights],
        out_specs=x_spec,
        out_shape=jax.ShapeDtypeStruct(x.shape, F32),
        scratch_shapes=[pltpu.VMEM((nb * tt, D_MODEL), BF16)],
        compiler_params=_params(),
        name="attn",
    )(x, kb, vb, *weights)


def _ffn_call(x, cf0, p, l, nb, tt, final_norm, scatter_out):
    bsz, seq, _ = x.shape
    x_spec, state_spec = _tile_specs(nb, tt)
    weights = [p["g_ffn"], p["w_up"], p["conv_f_w"], p["w_down"]]
    kern = functools.partial(_ffn_kernel, nb=nb, tt=tt, final_norm=final_norm, scatter_out=scatter_out)
    scatter_scratch = [pltpu.VMEM((2, nb, tt // SUBLANES, SUBLANES, D_MODEL), F32),
                       pltpu.SemaphoreType.DMA((2,))] if scatter_out else []
    return pl.pallas_call(
        kern,
        grid=(bsz // nb, seq // tt),
        in_specs=[x_spec, state_spec(FFN_CONV - 1, D_FF)] + [_of_layer(a, l) for a in weights] + [_whole()],
        out_specs=[pl.BlockSpec(memory_space=pl.ANY) if scatter_out else x_spec,
                   state_spec(FFN_CONV - 1, D_FF)],
        out_shape=[jax.ShapeDtypeStruct(x.shape, F32), jax.ShapeDtypeStruct(cf0.shape, F32)],
        scratch_shapes=[pltpu.VMEM((nb, FFN_CONV - 1, SUBLANES, D_FF), F32)] + scatter_scratch,
        compiler_params=_params(),
        name="ffn",
    )(x, cf0, *weights, p["g_final"])


def _kv_call(mem, w_k, w_v):
    bsz = mem.shape[0]
    depth = w_k.shape[0]
    w_spec = pl.BlockSpec((None, D_MODEL, D_MODEL), lambda l, b: (l, 0, 0))
    out5 = pl.BlockSpec(memory_space=pl.ANY)
    out4 = pl.BlockSpec((None, None, N_MEM, D_MODEL), lambda l, b: (l, b, 0, 0))
    shape5 = jax.ShapeDtypeStruct((depth, bsz, N_MEM, X_HEADS, X_HD), F32)
    shape4 = jax.ShapeDtypeStruct((depth, bsz, N_MEM, D_MODEL), BF16)
    return pl.pallas_call(
        _kv_kernel,
        grid=(depth, bsz),
        in_specs=[pl.BlockSpec((1, N_MEM, D_MODEL), lambda l, b: (b, 0, 0)), w_spec, w_spec],
        out_specs=[out5, out5, out4, out4],
        out_shape=[shape5, shape5, shape4, shape4],
        scratch_shapes=[pltpu.VMEM((2, 2, N_MEM, D_MODEL), F32), pltpu.SemaphoreType.DMA((2,))],
        compiler_params=_params(),
        name="memory_kv",
    )(mem, w_k, w_v)


def _kv_cast_call(cache_k, cache_v):
    depth, bsz = cache_k.shape[:2]
    in_spec = pl.BlockSpec(memory_space=pl.ANY)
    out_spec = pl.BlockSpec((None, None, N_MEM, D_MODEL), lambda l, b: (l, b, 0, 0))
    shape = jax.ShapeDtypeStruct((depth, bsz, N_MEM, D_MODEL), BF16)
    return pl.pallas_call(
        _kv_cast_kernel,
        grid=(depth, bsz),
        in_specs=[in_spec, in_spec],
        out_specs=[out_spec, out_spec],
        out_shape=[shape, shape],
        scratch_shapes=[pltpu.VMEM((2, 2, N_MEM, D_MODEL), F32), pltpu.SemaphoreType.DMA((2,))],
        compiler_params=_params(),
        name="cache_kv_cast",
    )(cache_k, cache_v)


def _block_diag(blocks):
    n, r, c = blocks.shape[-3:]
    eye = jnp.eye(n, dtype=blocks.dtype)
    return (blocks[..., :, :, None, :] * eye[:, None, :, None]).reshape(blocks.shape[:-3] + (n * r, n * c))


def _tile_times(tt):
    i = jnp.arange(tt)
    return (i % SUBLANES) * (tt // SUBLANES) + i // SUBLANES


def _to_segments(x, tt):
    bsz, seq, c = x.shape
    return x.reshape(bsz, seq // tt, SUBLANES, tt // SUBLANES, c).swapaxes(2, 3).reshape(bsz, seq, c)


def _from_segments(x, tt):
    bsz, seq, c = x.shape
    return x.reshape(bsz, seq // tt, tt // SUBLANES, SUBLANES, c).swapaxes(2, 3).reshape(bsz, seq, c)


def _on_sublanes(state):
    return jnp.broadcast_to(state[..., None, :], state.shape[:-1] + (SUBLANES, state.shape[-1]))


def _shared_params(g_mix, w_in, conv_a_w, conv_a_b, w_rg, b_rg, w_ig, b_ig, lam, conv_b_w, g_v, w_out,
                   g_x, w_q, w_o, g_ffn, w_up, conv_f_w, w_down, g_final):
    per_half = A_HEADS // 2
    gates = [jnp.concatenate([_block_diag(w_rg[:, j * per_half:(j + 1) * per_half]),
                              _block_diag(w_ig[:, j * per_half:(j + 1) * per_half])], axis=-1)
             for j in range(2)]
    group = jnp.arange(C_WIDTH) // C_HD

    def row(a):
        return a[:, None, :]

    def taps(a):
        return a[:, :, None, :]

    return dict(
        g_mix=row(g_mix), w_in=w_in.astype(BF16), conv_a_w=taps(conv_a_w), conv_a_b=row(conv_a_b),
        w_gates=jnp.stack(gates, axis=1).astype(BF16), b_rg=row(b_rg), b_ig=row(b_ig), lam=row(lam),
        conv_b_w=taps(conv_b_w), g_v=row(g_v),
        group_mean=((group[:, None] == group[None, :]).astype(F32) / C_HD).astype(BF16),
        w_out=w_out.astype(BF16), g_x=row(g_x), w_q=w_q.astype(BF16), w_o=w_o.astype(BF16),
        g_ffn=row(g_ffn), w_up=w_up.astype(BF16), conv_f_w=taps(conv_f_w), w_down=w_down.astype(BF16),
        g_final=g_final[None])


def _tile_params(tt, chunk, w_s, b_s):
    pos = _tile_times(tt) % chunk
    return dict(w_s=w_s[:, :, pos][:, :, :, pos],
                b_s_full=jnp.repeat(jnp.swapaxes(b_s[:, :, pos], 1, 2), C_HD, axis=2))


def _tiles(bsz, seq):
    if seq <= MLP_CHUNK:
        return bsz, seq
    return PROMPT_BATCH_TILE, PROMPT_TIME_TILE


def _trunk(x, kb, vb, states, p, emit_vc):
    bsz, seq, _ = x.shape
    nb, tt = _tiles(bsz, seq)
    chunk = min(seq, MLP_CHUNK)
    by_dma = seq > tt
    if not by_dma:
        x = _to_segments(x, tt)
    new_states = [[] for _ in range(4)]
    vcs = []
    for l in range(DEPTH):
        last = l == DEPTH - 1
        ca0, h0, cb0, cf0 = (s[l] for s in states)
        mix = _mixer_call(x, ca0, h0, cb0, p, l, nb, tt, chunk, emit_vc, by_dma and l == 0)
        x = _attn_call(mix[0], kb, vb, p, l, *((1, ATTN_TIME_TILE) if by_dma else (nb, tt)))
        x, cf = _ffn_call(x, cf0, p, l, nb, tt, last, by_dma and last)
        for acc, val in zip(new_states, (mix[1], mix[2], mix[3], cf)):
            acc.append(val)
        if emit_vc:
            vcs.append(mix[4])
    new_states = tuple(jnp.stack(s)[..., SUBLANES - 1, :] for s in new_states)
    vc = _from_segments(jnp.concatenate(vcs, axis=0), tt).reshape(DEPTH, bsz, seq, C_WIDTH) if emit_vc else None
    return (x if by_dma else _from_segments(x, tt)), new_states, vc


def kernel(x_prompt, x_sample, mem_prompt, cache_mem_k, cache_mem_v, state_conv_a, state_h_a, state_conv_b, state_conv_ffn, g_mix, w_in, conv_a_w, conv_a_b, w_rg, b_rg, w_ig, b_ig, lam, conv_b_w, g_v, w_s, b_s, w_out, g_x, w_q, w_k, w_v, w_o, g_ffn, w_up, conv_f_w, w_down, g_final):
    bp, seq_p, _ = x_prompt.shape
    bs, seq_s, _ = x_sample.shape
    shared = _shared_params(g_mix, w_in, conv_a_w, conv_a_b, w_rg, b_rg, w_ig, b_ig, lam, conv_b_w, g_v,
                            w_out, g_x, w_q, w_o, g_ffn, w_up, conv_f_w, w_down, g_final)

    def tile_params(bsz, seq):
        return dict(shared, **_tile_params(_tiles(bsz, seq)[1], min(seq, MLP_CHUNK), w_s, b_s))

    mem_k, mem_v, kb, vb = _kv_call(mem_prompt, w_k.astype(BF16), w_v.astype(BF16))
    zero_states = tuple(jnp.zeros((DEPTH, bp) + shape + (SUBLANES, width), F32)
                        for shape, width in (((A_CONV - 1,), A_WIDTH), ((), A_WIDTH),
                                             ((B_CONV - 1,), B_WIDTH), ((FFN_CONV - 1,), D_FF)))
    y_p, states_p, _ = _trunk(x_prompt, kb, vb, zero_states, tile_params(bp, seq_p), False)
    carried = tuple(_on_sublanes(s) for s in (state_conv_a, state_h_a, state_conv_b, state_conv_ffn))
    cache_kb, cache_vb = _kv_cast_call(cache_mem_k, cache_mem_v)
    y_s, states_s, vc_s = _trunk(x_sample, cache_kb, cache_vb, carried, tile_params(bs, seq_s), True)
    return (y_p, y_s) + states_p + (mem_k, mem_v) + states_s + (vc_s,)
```

```python
import functools
import math

import jax
import jax.numpy as jnp
from jax import lax
from jax.experimental import pallas as pl
from jax.experimental.pallas import tpu as pltpu

D_MODEL = 1024
DEPTH = 2
A_WIDTH = 512
A_HEADS = 8
A_HD = 64
A_CONV = 4
RG_C = 8.0
B_WIDTH = 256
B_CONV = 3
C_WIDTH = 256
C_HEADS = 4
C_HD = 64
MLP_CHUNK = 128
N_MEM = 256
X_HEADS = 4
X_HD = 256
D_FF = 2816
FFN_CONV = 3
EPS = 1e-6
IN_COLS = 2 * A_WIDTH + 3 * B_WIDTH + 2 * C_WIDTH

SUBLANES = 8
LANES = 128
PROMPT_TIME_TILE = 256
PROMPT_BATCH_TILE = 4
MIXER_GROUP = 1
ATTN_TIME_TILE = 1024
ATTN_GROUPS = 2
KV_BATCH_TILE = 4
Z_SLOTS = 3
PROJ_PARTS = 9
VMEM_LIMIT_BYTES = 56 * 1024 * 1024

LOG2_E = math.log2(math.e)
GELU_C0 = math.sqrt(2.0 / math.pi)
GELU_C1 = 0.044715

F32 = jnp.float32
BF16 = jnp.bfloat16


def _dot(a, b):
    return jnp.dot(a, b, preferred_element_type=F32)


def _rms(x, g):
    ms = jnp.mean(x * x, axis=-1, keepdims=True)
    return x * lax.rsqrt(ms + EPS) * g


def _gelu(x):
    half_x = 0.5 * x
    inner = (x * x * (GELU_C0 * GELU_C1) + GELU_C0) * x
    return half_x * jnp.tanh(inner) + half_x


def _sublane_iota(shape):
    return lax.broadcasted_iota(jnp.int32, shape, len(shape) - 2)


def _causal_taps(cur, tail_ref, w):
    n_prev = tail_ref.shape[1]
    seg = cur.shape[1]
    last = cur[:, seg - n_prev:]
    wrapped = pltpu.roll(jnp.where(_sublane_iota(last.shape) == SUBLANES - 1, tail_ref[...], last), 1, 2)
    tail_ref[...] = last
    xp = jnp.concatenate([wrapped, cur], axis=1)
    acc = xp[:, 0:seg] * w[0]
    for j in range(1, n_prev + 1):
        acc = acc + xp[:, j:j + seg] * w[j]
    return acc


def _segment_scan(a, b, h_ref):
    seg = a.shape[1]
    local, decay = b[:, 0], a[:, 0]
    locals_, decays = [local], [decay]
    for s in range(1, seg):
        local = a[:, s] * local + b[:, s]
        decay = a[:, s] * decay
        locals_.append(local)
        decays.append(decay)
    h_in = h_ref[:, SUBLANES - 1:SUBLANES, :]
    row = _sublane_iota(local.shape)
    seg_a, seg_b = decay, local
    d = 1
    while d < SUBLANES:
        a_prev = pltpu.roll(seg_a, d, 1)
        b_prev = pltpu.roll(seg_b, d, 1)
        keep = row >= d
        seg_b = jnp.where(keep, seg_a * b_prev + seg_b, seg_b)
        seg_a = jnp.where(keep, seg_a * a_prev, seg_a)
        d *= 2
    h_end = seg_a * h_in + seg_b
    h_ref[...] = h_end
    h_start = jnp.where(row == 0, h_in, pltpu.roll(h_end, 1, 1))
    return jnp.stack([locals_[s] + decays[s] * h_start for s in range(seg)], axis=1)


def _grid_step():
    return (pl.program_id(0) * pl.num_programs(1) + pl.program_id(1),
            pl.num_programs(0) * pl.num_programs(1))


def _tile_copies(hbm_ref, buf_ref, sem, b, t, nb, tt, to_vmem):
    seg = tt // SUBLANES
    copies = []
    for r in range(SUBLANES):
        hbm = hbm_ref.at[pl.ds(b * nb, nb), pl.ds(t * tt + r * seg, seg), :]
        vmem = buf_ref.at[:, :, r, :]
        copies.append(pltpu.make_async_copy(hbm, vmem, sem) if to_vmem
                      else pltpu.make_async_copy(vmem, hbm, sem))
    return copies


def _head_copies(hbm_refs, buf_ref, sem, l, b, to_vmem):
    n_seq = buf_ref.shape[1]
    copies = []
    for j, hbm_ref in enumerate(hbm_refs):
        for hd in range(X_HEADS):
            hbm = hbm_ref.at[l, pl.ds(b * n_seq, n_seq), :, hd, :]
            vmem = buf_ref.at[j, :, :, pl.ds(hd * X_HD, X_HD)]
            copies.append(pltpu.make_async_copy(hbm, vmem, sem) if to_vmem
                          else pltpu.make_async_copy(vmem, hbm, sem))
    return copies


def _next_grid_index():
    wraps = pl.program_id(1) + 1 == pl.num_programs(1)
    return pl.program_id(0) + wraps.astype(jnp.int32), jnp.where(wraps, 0, pl.program_id(1) + 1)


def _prefetched(fetch):
    step, n_steps = _grid_step()
    slot = lax.rem(step, 2)

    @pl.when(step == 0)
    def _():
        for c in fetch(pl.program_id(0), pl.program_id(1), slot):
            c.start()

    @pl.when(step + 1 < n_steps)
    def _():
        for c in fetch(*_next_grid_index(), 1 - slot):
            c.start()

    for c in fetch(pl.program_id(0), pl.program_id(1), slot):
        c.wait()
    return slot


def _flushed(flush, fill):
    step, n_steps = _grid_step()
    slot = lax.rem(step, 2)

    @pl.when(step >= 2)
    def _():
        for c in flush(slot):
            c.wait()

    fill(slot)
    for c in flush(slot):
        c.start()

    @pl.when(step == n_steps - 1)
    def _():
        @pl.when(n_steps > 1)
        def _():
            for c in flush(1 - slot):
                c.wait()
        for c in flush(slot):
            c.wait()


def _mixer_kernel(x_ref, ca0_ref, h0_ref, cb0_ref, gmix_ref, win_ref, caw_ref, cab_ref, wg_ref,
                  brg_ref, big_ref, lam_ref, cbw_ref, gv_ref, gsum_ref, ws_ref, bsf_ref, wout_ref, gnext_ref,
                  xo_ref, xno_ref, cao_ref, ho_ref, cbo_ref, *rest, nb, gb, tt, chunk, emit_vc, gather_x):
    if gather_x:
        *rest, x_buf, x_sem = rest
    if emit_vc:
        vco_ref, ca_tail, cb_tail, h_state, wmix, ycat, zbuf = rest
    else:
        ca_tail, cb_tail, h_state, wmix, ycat, zbuf = rest
    rows = gb * tt
    seg = tt // SUBLANES
    groups = nb // gb

    if gather_x:
        slot = _prefetched(lambda b, t, into: _tile_copies(x_ref, x_buf.at[into], x_sem.at[into],
                                                           b, t, nb, tt, to_vmem=True))

    def read_x(g):
        if gather_x:
            return x_buf[slot, g * gb:(g + 1) * gb].reshape(rows, D_MODEL)
        return x_ref[g * gb:(g + 1) * gb].reshape(rows, D_MODEL)

    @pl.when(pl.program_id(1) == 0)
    def _():
        ca_tail[...] = ca0_ref[...]
        cb_tail[...] = cb0_ref[...]
        h_state[...] = h0_ref[...]
        ti = lax.broadcasted_iota(jnp.int32, (tt, tt), 0)
        tj = lax.broadcasted_iota(jnp.int32, (tt, tt), 1)
        time_i = (ti % SUBLANES) * seg + ti // SUBLANES
        time_j = (tj % SUBLANES) * seg + tj // SUBLANES
        allowed = (time_i >= time_j) & (time_i // chunk == time_j // chunk)
        for hd in range(C_HEADS):
            wmix[hd] = jnp.where(allowed, ws_ref[hd], 0.0).astype(BF16)

    def project_in(g):
        xn = _rms(read_x(g), gmix_ref[...]).astype(BF16)
        z = zbuf.at[g % Z_SLOTS]
        width = IN_COLS // PROJ_PARTS

        def part(c):
            z[:, c * width:(c + 1) * width] = _dot(xn, win_ref[:, c * width:(c + 1) * width])

        return z, [functools.partial(part, c) for c in range(PROJ_PARTS)]

    col = {}
    o = 0
    for name, width in (("xa", A_WIDTH), ("ga", A_WIDTH), ("xb", B_WIDTH), ("gb", B_WIDTH),
                        ("gc", B_WIDTH), ("uc", C_WIDTH), ("vc", C_WIDTH)):
        col[name] = slice(o, o + width)
        o += width
    half = A_WIDTH // 2

    def mix_matmuls(g, z):
        seqs = slice(g * gb, (g + 1) * gb)
        out = {}

        def gates_and_norm():
            xc = _causal_taps(z[:, col["xa"]].reshape(gb, seg, SUBLANES, A_WIDTH), ca_tail.at[seqs],
                              caw_ref[...])
            xc2 = (xc + cab_ref[...]).reshape(rows, A_WIDTH)
            xcb = xc2.astype(BF16)
            out["xc2"] = xc2
            out["gz0"] = _dot(xcb[:, :half], wg_ref[0])
            out["gz1"] = _dot(xcb[:, half:], wg_ref[1])
            v = _gelu(z[:, col["vc"]])
            sq = v * v
            sq_hi = sq.astype(BF16)
            sq_lo = (sq - sq_hi.astype(F32)).astype(BF16)
            out["v"] = v
            out["ms"] = _dot(sq_hi, gsum_ref[...]) + _dot(sq_lo, gsum_ref[...])

        def spatial_mix():
            vn = out.pop("v") * lax.rsqrt(out.pop("ms") + EPS) * gv_ref[...]
            if emit_vc:
                vco_ref[seqs] = vn.reshape(gb, tt, C_WIDTH)
            vnb = vn.astype(BF16)
            first_head = lax.broadcasted_iota(jnp.int32, (tt, LANES), 1) < C_HD
            mixed = []
            for b in range(gb):
                pieces = []
                for pair in range(C_WIDTH // LANES):
                    vp = vnb[b * tt:(b + 1) * tt, pair * LANES:(pair + 1) * LANES]
                    pieces.append(jnp.where(first_head, _dot(wmix[2 * pair], vp), _dot(wmix[2 * pair + 1], vp)))
                mixed.append(jnp.concatenate(pieces, axis=1) + bsf_ref[...])
            out["mixed"] = jnp.concatenate(mixed, axis=0)

        return out, [gates_and_norm, spatial_mix]

    def mix_elementwise(g, z, xc2, gz0, gz1, mixed):
        seqs = slice(g * gb, (g + 1) * gb)
        y = ycat.at[g * rows:(g + 1) * rows]
        r = jax.nn.sigmoid(jnp.concatenate([gz0[:, :half], gz1[:, :half]], axis=1) + brg_ref[...])
        gi = jax.nn.sigmoid(jnp.concatenate([gz0[:, half:], gz1[:, half:]], axis=1) + big_ref[...])
        neg_lam = -lam_ref[...]
        softplus = jnp.maximum(neg_lam, 0.0) + jnp.log1p(jnp.exp(-jnp.abs(neg_lam)))
        decay_rate = RG_C * softplus
        a = jnp.exp2((-LOG2_E) * decay_rate * r)
        one_minus_a2 = (1.0 + a * a) * jnp.tanh(decay_rate * r)
        root = jnp.where(one_minus_a2 > 0.0, one_minus_a2 * lax.rsqrt(one_minus_a2), 0.0)
        bterm = root * (gi * xc2)
        h = _segment_scan(a.reshape(gb, seg, SUBLANES, A_WIDTH),
                          bterm.reshape(gb, seg, SUBLANES, A_WIDTH), h_state.at[seqs])
        y[:, 0:A_WIDTH] = (_gelu(z[:, col["ga"]]) * h.reshape(rows, A_WIDTH)).astype(BF16)
        zb = _causal_taps((z[:, col["gc"]] * z[:, col["xb"]]).reshape(gb, seg, SUBLANES, B_WIDTH),
                          cb_tail.at[seqs], cbw_ref[...])
        y[:, A_WIDTH:A_WIDTH + B_WIDTH] = (z[:, col["gb"]] * zb.reshape(rows, B_WIDTH)).astype(BF16)
        y[:, A_WIDTH + B_WIDTH:] = (_gelu(z[:, col["uc"]]) * mixed).astype(BF16)

    def project_out(g):
        out = read_x(g) + _dot(ycat[g * rows:(g + 1) * rows], wout_ref[...])
        xo_ref[g * gb:(g + 1) * gb] = out.reshape(gb, tt, D_MODEL)
        xno_ref[g * gb:(g + 1) * gb] = _rms(out, gnext_ref[...]).astype(BF16).reshape(gb, tt, D_MODEL)

    def interleave(big_steps, small_steps):
        stride = max(1, len(big_steps) // max(1, len(small_steps)))
        pending = list(small_steps)
        for i, step in enumerate(big_steps):
            step()
            if pending and (i + 1) % stride == 0:
                pending.pop(0)()
        for step in pending:
            step()

    zs, small = {}, {}
    zs[0], parts = project_in(0)
    interleave(parts, [])
    small[0], steps = mix_matmuls(0, zs[0])
    if groups > 1:
        zs[1], parts = project_in(1)
        interleave(parts, steps)
    else:
        interleave([], steps)
    for g in range(groups):
        parts, steps = [], []
        if g + 2 < groups:
            zs[g + 2], parts = project_in(g + 2)
        if g + 1 < groups:
            small[g + 1], steps = mix_matmuls(g + 1, zs[g + 1])
        interleave(parts, steps)
        res = small.pop(g)
        mix_elementwise(g, zs.pop(g), res["xc2"], res["gz0"], res["gz1"], res["mixed"])
        project_out(g)
    cao_ref[...] = ca_tail[...]
    ho_ref[...] = h_state[...]
    cbo_ref[...] = cb_tail[...]


def _attn_kernel(x_ref, xn_ref, kb, vb, wq_ref, wo_ref, gnext_ref, xo_ref, xno_ref, ocat, *, nb, tt, groups):
    rows = nb * tt // groups
    span = min(rows, tt)

    def read(ref, g):
        if rows >= tt:
            return ref[g * rows // tt:(g + 1) * rows // tt].reshape(rows, D_MODEL)
        return ref[g * rows // tt, (g * rows) % tt:(g * rows) % tt + rows]

    def write(ref, g, val):
        if rows >= tt:
            ref[g * rows // tt:(g + 1) * rows // tt] = val.reshape(rows // tt, tt, D_MODEL)
        else:
            ref[g * rows // tt, (g * rows) % tt:(g * rows) % tt + rows] = val

    def project_q(g):
        return (_dot(read(xn_ref, g), wq_ref[...]) * (X_HD ** -0.5)).astype(BF16)

    def attend(g, q):
        for part in range(rows // span):
            r0 = g * rows + part * span
            seq = r0 // tt
            for hd in range(X_HEADS):
                cols = slice(hd * X_HD, (hd + 1) * X_HD)
                s = lax.dot_general(q[part * span:(part + 1) * span, cols], kb[seq, :, cols],
                                    (((1,), (1,)), ((), ())), preferred_element_type=F32)
                e = jnp.exp(s - jnp.max(s, axis=-1, keepdims=True))
                denom = jnp.sum(e, axis=-1, keepdims=True)
                oh = _dot(e.astype(BF16), vb[seq, :, cols]) / denom
                ocat[r0:r0 + span, cols] = oh.astype(BF16)

    def project_out(g):
        out = read(x_ref, g) + _dot(ocat[g * rows:(g + 1) * rows], wo_ref[...])
        write(xo_ref, g, out)
        write(xno_ref, g, _rms(out, gnext_ref[...]).astype(BF16))

    q = project_q(0)
    for g in range(groups):
        q_next = project_q(g + 1) if g + 1 < groups else None
        attend(g, q)
        project_out(g)
        q = q_next


def _ffn_kernel(x_ref, xn_ref, cf0_ref, wup_ref, cfw_ref, wdown_ref, gfin_ref, xo_ref, cfo_ref,
                cf_tail, *rest, nb, tt, final_norm, scatter_out):
    rows = nb * tt
    seg = tt // SUBLANES

    @pl.when(pl.program_id(1) == 0)
    def _():
        cf_tail[...] = cf0_ref[...]

    x = x_ref[...].reshape(rows, D_MODEL)
    gu = _dot(xn_ref[...].reshape(rows, D_MODEL), wup_ref[...])
    gconv = _causal_taps(gu[:, :D_FF].reshape(nb, seg, SUBLANES, D_FF), cf_tail, cfw_ref[...])
    cfo_ref[...] = cf_tail[...]
    act = (jax.nn.silu(gconv.reshape(rows, D_FF)) * gu[:, D_FF:]).astype(BF16)
    out = x + _dot(act, wdown_ref[...])
    if final_norm:
        out = _rms(out, gfin_ref[...])
    if not scatter_out:
        xo_ref[...] = out.reshape(nb, tt, D_MODEL)
        return

    out_buf, out_sem = rest

    def flush(from_slot):
        return _tile_copies(xo_ref, out_buf.at[from_slot], out_sem.at[from_slot],
                            pl.program_id(0), pl.program_id(1), nb, tt, to_vmem=False)

    def fill(slot):
        out_buf[slot] = out.reshape(nb, seg, SUBLANES, D_MODEL)

    _flushed(flush, fill)


def _kv_kernel(mem_ref, wk_ref, wv_ref, ko_hbm, vo_hbm, kb_ref, vb_ref, kv_buf, sem):
    n_seq = mem_ref.shape[0]
    m = mem_ref[...].reshape(n_seq * N_MEM, D_MODEL).astype(BF16)
    k = _dot(m, wk_ref[...]).reshape(n_seq, N_MEM, D_MODEL)
    v = _dot(m, wv_ref[...]).reshape(n_seq, N_MEM, D_MODEL)
    kb_ref[...] = k.astype(BF16)
    vb_ref[...] = v.astype(BF16)

    def flush(from_slot):
        return _head_copies((ko_hbm, vo_hbm), kv_buf.at[from_slot], sem.at[from_slot],
                            pl.program_id(0), pl.program_id(1), to_vmem=False)

    def fill(slot):
        kv_buf[slot, 0] = k
        kv_buf[slot, 1] = v

    _flushed(flush, fill)


def _kv_cast_kernel(k_hbm, v_hbm, kb_ref, vb_ref, kv_buf, sem):
    slot = _prefetched(lambda l, b, into: _head_copies((k_hbm, v_hbm), kv_buf.at[into], sem.at[into],
                                                       l, b, to_vmem=True))
    kb_ref[...] = kv_buf[slot, 0].astype(BF16)
    vb_ref[...] = kv_buf[slot, 1].astype(BF16)


def _whole():
    return pl.BlockSpec(memory_space=pltpu.MemorySpace.VMEM)


def _of_layer(arr, l):
    zeros = (0,) * (arr.ndim - 1)
    return pl.BlockSpec((None,) + arr.shape[1:], lambda b, t: (l,) + zeros, pipeline_mode=pl.Buffered(1))


def _params():
    return pltpu.CompilerParams(dimension_semantics=("arbitrary", "arbitrary"),
                                vmem_limit_bytes=VMEM_LIMIT_BYTES)


def _tile_specs(nb, tt):
    x_spec = pl.BlockSpec((nb, tt, D_MODEL), lambda b, t: (b, t, 0))

    def state_spec(n, width):
        return pl.BlockSpec((nb, n, SUBLANES, width), lambda b, t: (b, 0, 0, 0))

    return x_spec, state_spec


def _mixer_call(x, ca0, h0, cb0, p, l, nb, tt, chunk, emit_vc, gather_x):
    bsz, seq, _ = x.shape
    layer_params = [p[name] for name in ("g_mix", "w_in", "conv_a_w", "conv_a_b", "w_gates", "b_rg", "b_ig",
                                         "lam", "conv_b_w", "g_v")]
    mix_params = [p[name] for name in ("w_s", "b_s_full", "w_out", "g_x")]
    x_spec, state_spec = _tile_specs(nb, tt)
    h_spec = pl.BlockSpec((nb, SUBLANES, A_WIDTH), lambda b, t: (b, 0, 0))
    out_shape = [jax.ShapeDtypeStruct(x.shape, F32),
                 jax.ShapeDtypeStruct(x.shape, BF16),
                 jax.ShapeDtypeStruct(ca0.shape, F32),
                 jax.ShapeDtypeStruct(h0.shape, F32),
                 jax.ShapeDtypeStruct(cb0.shape, F32)]
    out_specs = [x_spec, x_spec, state_spec(A_CONV - 1, A_WIDTH), h_spec, state_spec(B_CONV - 1, B_WIDTH)]
    if emit_vc:
        out_shape.append(jax.ShapeDtypeStruct((bsz, seq, C_WIDTH), F32))
        out_specs.append(pl.BlockSpec((nb, tt, C_WIDTH), lambda b, t: (b, t, 0)))
    gb = nb if seq == tt else MIXER_GROUP
    kern = functools.partial(_mixer_kernel, nb=nb, gb=gb, tt=tt, chunk=chunk, emit_vc=emit_vc,
                             gather_x=gather_x)
    gather_scratch = [pltpu.VMEM((2, nb, tt // SUBLANES, SUBLANES, D_MODEL), F32),
                      pltpu.SemaphoreType.DMA((2,))] if gather_x else []
    return pl.pallas_call(
        kern,
        grid=(bsz // nb, seq // tt),
        in_specs=([pl.BlockSpec(memory_space=pl.ANY) if gather_x else x_spec,
                   state_spec(A_CONV - 1, A_WIDTH), h_spec, state_spec(B_CONV - 1, B_WIDTH)]
                  + [_of_layer(a, l) for a in layer_params] + [_whole()]
                  + [_of_layer(a, l) for a in mix_params]),
        out_specs=out_specs,
        out_shape=out_shape,
        scratch_shapes=[pltpu.VMEM((nb, A_CONV - 1, SUBLANES, A_WIDTH), F32),
                        pltpu.VMEM((nb, B_CONV - 1, SUBLANES, B_WIDTH), F32),
                        pltpu.VMEM((nb, SUBLANES, A_WIDTH), F32),
                        pltpu.VMEM((C_HEADS, tt, tt), BF16),
                        pltpu.VMEM((nb * tt, D_MODEL), BF16),
                        pltpu.VMEM((Z_SLOTS, gb * tt, IN_COLS), F32)] + gather_scratch,
        compiler_params=_params(),
        name="mixer",
    )(x, ca0, h0, cb0, *layer_params, p["group_mean"], *mix_params)


def _attn_call(x, xn, kb, vb, p, l, nb, tt):
    bsz, seq, _ = x.shape
    x_spec, _ = _tile_specs(nb, tt)
    kv_spec = pl.BlockSpec((None, nb, N_MEM, D_MODEL), lambda b, t: (l, b, 0, 0))
    weights = [p["w_q"], p["w_o"], p["g_ffn"]]
    groups = ATTN_GROUPS if nb == 1 else 1
    kern = functools.partial(_attn_kernel, nb=nb, tt=tt, groups=groups)
    return pl.pallas_call(
        kern,
        grid=(bsz // nb, seq // tt),
        in_specs=[x_spec, x_spec, kv_spec, kv_spec] + [_of_layer(a, l) for a in weights],
        out_specs=[x_spec, x_spec],
        out_shape=[jax.ShapeDtypeStruct(x.shape, F32), jax.ShapeDtypeStruct(x.shape, BF16)],
        scratch_shapes=[pltpu.VMEM((nb * tt, D_MODEL), BF16)],
        compiler_params=_params(),
        name="attn",
    )(x, xn, kb, vb, *weights)


def _ffn_call(x, xn, cf0, p, l, nb, tt, final_norm, scatter_out):
    bsz, seq, _ = x.shape
    x_spec, state_spec = _tile_specs(nb, tt)
    weights = [p["w_up"], p["conv_f_w"], p["w_down"]]
    kern = functools.partial(_ffn_kernel, nb=nb, tt=tt, final_norm=final_norm, scatter_out=scatter_out)
    scatter_scratch = [pltpu.VMEM((2, nb, tt // SUBLANES, SUBLANES, D_MODEL), F32),
                       pltpu.SemaphoreType.DMA((2,))] if scatter_out else []
    return pl.pallas_call(
        kern,
        grid=(bsz // nb, seq // tt),
        in_specs=([x_spec, x_spec, state_spec(FFN_CONV - 1, D_FF)] + [_of_layer(a, l) for a in weights]
                  + [_whole()]),
        out_specs=[pl.BlockSpec(memory_space=pl.ANY) if scatter_out else x_spec,
                   state_spec(FFN_CONV - 1, D_FF)],
        out_shape=[jax.ShapeDtypeStruct(x.shape, F32), jax.ShapeDtypeStruct(cf0.shape, F32)],
        scratch_shapes=[pltpu.VMEM((nb, FFN_CONV - 1, SUBLANES, D_FF), F32)] + scatter_scratch,
        compiler_params=_params(),
        name="ffn",
    )(x, xn, cf0, *weights, p["g_final"])


def _kv_call(mem, w_k, w_v):
    bsz = mem.shape[0]
    depth = w_k.shape[0]
    n_seq = min(bsz, KV_BATCH_TILE)
    w_spec = pl.BlockSpec((None, D_MODEL, D_MODEL), lambda l, b: (l, 0, 0))
    out5 = pl.BlockSpec(memory_space=pl.ANY)
    out4 = pl.BlockSpec((None, n_seq, N_MEM, D_MODEL), lambda l, b: (l, b, 0, 0))
    shape5 = jax.ShapeDtypeStruct((depth, bsz, N_MEM, X_HEADS, X_HD), F32)
    shape4 = jax.ShapeDtypeStruct((depth, bsz, N_MEM, D_MODEL), BF16)
    return pl.pallas_call(
        _kv_kernel,
        grid=(depth, bsz // n_seq),
        in_specs=[pl.BlockSpec((n_seq, N_MEM, D_MODEL), lambda l, b: (b, 0, 0)), w_spec, w_spec],
        out_specs=[out5, out5, out4, out4],
        out_shape=[shape5, shape5, shape4, shape4],
        scratch_shapes=[pltpu.VMEM((2, 2, n_seq, N_MEM, D_MODEL), F32), pltpu.SemaphoreType.DMA((2,))],
        compiler_params=_params(),
        name="memory_kv",
    )(mem, w_k, w_v)


def _kv_cast_call(cache_k, cache_v):
    depth, bsz = cache_k.shape[:2]
    n_seq = min(bsz, KV_BATCH_TILE)
    in_spec = pl.BlockSpec(memory_space=pl.ANY)
    out_spec = pl.BlockSpec((None, n_seq, N_MEM, D_MODEL), lambda l, b: (l, b, 0, 0))
    shape = jax.ShapeDtypeStruct((depth, bsz, N_MEM, D_MODEL), BF16)
    return pl.pallas_call(
        _kv_cast_kernel,
        grid=(depth, bsz // n_seq),
        in_specs=[in_spec, in_spec],
        out_specs=[out_spec, out_spec],
        out_shape=[shape, shape],
        scratch_shapes=[pltpu.VMEM((2, 2, n_seq, N_MEM, D_MODEL), F32), pltpu.SemaphoreType.DMA((2,))],
        compiler_params=_params(),
        name="cache_kv_cast",
    )(cache_k, cache_v)


def _block_diag(blocks):
    n, r, c = blocks.shape[-3:]
    eye = jnp.eye(n, dtype=blocks.dtype)
    return (blocks[..., :, :, None, :] * eye[:, None, :, None]).reshape(blocks.shape[:-3] + (n * r, n * c))


def _tile_times(tt):
    i = jnp.arange(tt)
    return (i % SUBLANES) * (tt // SUBLANES) + i // SUBLANES


def _to_segments(x, tt):
    bsz, seq, c = x.shape
    return x.reshape(bsz, seq // tt, SUBLANES, tt // SUBLANES, c).swapaxes(2, 3).reshape(bsz, seq, c)


def _from_segments(x, tt):
    bsz, seq, c = x.shape
    return x.reshape(bsz, seq // tt, tt // SUBLANES, SUBLANES, c).swapaxes(2, 3).reshape(bsz, seq, c)


def _on_sublanes(state):
    return jnp.broadcast_to(state[..., None, :], state.shape[:-1] + (SUBLANES, state.shape[-1]))


def _shared_params(g_mix, w_in, conv_a_w, conv_a_b, w_rg, b_rg, w_ig, b_ig, lam, conv_b_w, g_v, w_out,
                   g_x, w_q, w_o, g_ffn, w_up, conv_f_w, w_down, g_final):
    per_half = A_HEADS // 2
    gates = [jnp.concatenate([_block_diag(w_rg[:, j * per_half:(j + 1) * per_half]),
                              _block_diag(w_ig[:, j * per_half:(j + 1) * per_half])], axis=-1)
             for j in range(2)]
    group = jnp.arange(C_WIDTH) // C_HD

    def row(a):
        return a[:, None, :]

    def taps(a):
        return a[:, :, None, :]

    return dict(
        g_mix=row(g_mix), w_in=w_in.astype(BF16), conv_a_w=taps(conv_a_w), conv_a_b=row(conv_a_b),
        w_gates=jnp.stack(gates, axis=1).astype(BF16), b_rg=row(b_rg), b_ig=row(b_ig), lam=row(lam),
        conv_b_w=taps(conv_b_w), g_v=row(g_v),
        group_mean=((group[:, None] == group[None, :]).astype(F32) / C_HD).astype(BF16),
        w_out=w_out.astype(BF16), g_x=row(g_x), w_q=w_q.astype(BF16), w_o=w_o.astype(BF16),
        g_ffn=row(g_ffn), w_up=w_up.astype(BF16), conv_f_w=taps(conv_f_w), w_down=w_down.astype(BF16),
        g_final=g_final[None])


def _tile_params(tt, chunk, w_s, b_s):
    pos = _tile_times(tt) % chunk
    return dict(w_s=w_s[:, :, pos][:, :, :, pos],
                b_s_full=jnp.repeat(jnp.swapaxes(b_s[:, :, pos], 1, 2), C_HD, axis=2))


def _tiles(bsz, seq):
    if seq <= MLP_CHUNK:
        return bsz, seq
    return PROMPT_BATCH_TILE, PROMPT_TIME_TILE


def _trunk(x, kb, vb, states, p, emit_vc):
    bsz, seq, _ = x.shape
    nb, tt = _tiles(bsz, seq)
    chunk = min(seq, MLP_CHUNK)
    by_dma = seq > tt
    if not by_dma:
        x = _to_segments(x, tt)
    new_states = [[] for _ in range(4)]
    vcs = []
    for l in range(DEPTH):
        last = l == DEPTH - 1
        ca0, h0, cb0, cf0 = (s[l] for s in states)
        mix = _mixer_call(x, ca0, h0, cb0, p, l, nb, tt, chunk, emit_vc, by_dma and l == 0)
        x, xn = _attn_call(mix[0], mix[1], kb, vb, p, l, *((1, ATTN_TIME_TILE) if by_dma else (nb, tt)))
        x, cf = _ffn_call(x, xn, cf0, p, l, nb, tt, last, by_dma and last)
        for acc, val in zip(new_states, (mix[2], mix[3], mix[4], cf)):
            acc.append(val)
        if emit_vc:
            vcs.append(mix[5])
    new_states = tuple(jnp.stack(s)[..., SUBLANES - 1, :] for s in new_states)
    vc = _from_segments(jnp.concatenate(vcs, axis=0), tt).reshape(DEPTH, bsz, seq, C_WIDTH) if emit_vc else None
    return (x if by_dma else _from_segments(x, tt)), new_states, vc


def kernel(x_prompt, x_sample, mem_prompt, cache_mem_k, cache_mem_v, state_conv_a, state_h_a, state_conv_b, state_conv_ffn, g_mix, w_in, conv_a_w, conv_a_b, w_rg, b_rg, w_ig, b_ig, lam, conv_b_w, g_v, w_s, b_s, w_out, g_x, w_q, w_k, w_v, w_o, g_ffn, w_up, conv_f_w, w_down, g_final):
    bp, seq_p, _ = x_prompt.shape
    bs, seq_s, _ = x_sample.shape
    shared = _shared_params(g_mix, w_in, conv_a_w, conv_a_b, w_rg, b_rg, w_ig, b_ig, lam, conv_b_w, g_v,
                            w_out, g_x, w_q, w_o, g_ffn, w_up, conv_f_w, w_down, g_final)

    def tile_params(bsz, seq):
        return dict(shared, **_tile_params(_tiles(bsz, seq)[1], min(seq, MLP_CHUNK), w_s, b_s))

    mem_k, mem_v, kb, vb = _kv_call(mem_prompt, w_k.astype(BF16), w_v.astype(BF16))
    zero_states = tuple(jnp.zeros((DEPTH, bp) + shape + (SUBLANES, width), F32)
                        for shape, width in (((A_CONV - 1,), A_WIDTH), ((), A_WIDTH),
                                             ((B_CONV - 1,), B_WIDTH), ((FFN_CONV - 1,), D_FF)))
    y_p, states_p, _ = _trunk(x_prompt, kb, vb, zero_states, tile_params(bp, seq_p), False)
    carried = tuple(_on_sublanes(s) for s in (state_conv_a, state_h_a, state_conv_b, state_conv_ffn))
    cache_kb, cache_vb = _kv_cast_call(cache_mem_k, cache_mem_v)
    y_s, states_s, vc_s = _trunk(x_sample, cache_kb, cache_vb, carried, tile_params(bs, seq_s), True)
    return (y_p, y_s) + states_p + (mem_k, mem_v) + states_s + (vc_s,)
```

```python
import functools
import math

import jax
import jax.numpy as jnp
from jax import lax
from jax.experimental import pallas as pl
from jax.experimental.pallas import tpu as pltpu

D_MODEL = 1024
DEPTH = 2
A_WIDTH = 512
A_HEADS = 8
A_HD = 64
A_CONV = 4
RG_C = 8.0
B_WIDTH = 256
B_CONV = 3
C_WIDTH = 256
C_HEADS = 4
C_HD = 64
MLP_CHUNK = 128
N_MEM = 256
X_HEADS = 4
X_HD = 256
D_FF = 2816
FFN_CONV = 3
EPS = 1e-6
IN_COLS = 2 * A_WIDTH + 3 * B_WIDTH + 2 * C_WIDTH

SUBLANES = 8
LANES = 128
PROMPT_TIME_TILE = 256
PROMPT_BATCH_TILE = 4
MIXER_GROUP = 1
ATTN_TIME_TILE = 2048
KV_BATCH_TILE = 4
Z_SLOTS = 3
PROJ_PARTS = 9
VMEM_LIMIT_BYTES = 56 * 1024 * 1024

LOG2_E = math.log2(math.e)
GELU_C0 = math.sqrt(2.0 / math.pi)
GELU_C1 = 0.044715

F32 = jnp.float32
BF16 = jnp.bfloat16


def _dot(a, b):
    return jnp.dot(a, b, preferred_element_type=F32)


def _rms(x, g):
    ms = jnp.mean(x * x, axis=-1, keepdims=True)
    return x * lax.rsqrt(ms + EPS) * g


def _gelu(x):
    scale = -2.0 * LOG2_E * GELU_C0
    return x / (1.0 + jnp.exp2((x * x * (scale * GELU_C1) + scale) * x))


def _sublane_iota(shape):
    return lax.broadcasted_iota(jnp.int32, shape, len(shape) - 2)


def _causal_taps(cur, tail_ref, w):
    n_prev = tail_ref.shape[1]
    seg = cur.shape[1]
    last = cur[:, seg - n_prev:]
    wrapped = pltpu.roll(jnp.where(_sublane_iota(last.shape) == SUBLANES - 1, tail_ref[...], last), 1, 2)
    tail_ref[...] = last
    xp = jnp.concatenate([wrapped, cur], axis=1)
    acc = xp[:, 0:seg] * w[0]
    for j in range(1, n_prev + 1):
        acc = acc + xp[:, j:j + seg] * w[j]
    return acc


def _segment_scan(a, b, h_ref):
    seg = a.shape[1]
    local, decay = b[:, 0], a[:, 0]
    locals_, decays = [local], [decay]
    for s in range(1, seg):
        local = a[:, s] * local + b[:, s]
        decay = a[:, s] * decay
        locals_.append(local)
        decays.append(decay)
    h_in = h_ref[:, SUBLANES - 1:SUBLANES, :]
    row = _sublane_iota(local.shape)
    seg_a, seg_b = decay, local
    d = 1
    while d < SUBLANES:
        a_prev = pltpu.roll(seg_a, d, 1)
        b_prev = pltpu.roll(seg_b, d, 1)
        keep = row >= d
        seg_b = jnp.where(keep, seg_a * b_prev + seg_b, seg_b)
        seg_a = jnp.where(keep, seg_a * a_prev, seg_a)
        d *= 2
    h_end = seg_a * h_in + seg_b
    h_ref[...] = h_end
    h_start = jnp.where(row == 0, h_in, pltpu.roll(h_end, 1, 1))
    return jnp.stack([locals_[s] + decays[s] * h_start for s in range(seg)], axis=1)


def _grid_step():
    return (pl.program_id(0) * pl.num_programs(1) + pl.program_id(1),
            pl.num_programs(0) * pl.num_programs(1))


def _tile_copies(hbm_ref, buf_ref, sem, b, t, nb, tt, to_vmem):
    seg = tt // SUBLANES
    copies = []
    for r in range(SUBLANES):
        hbm = hbm_ref.at[pl.ds(b * nb, nb), pl.ds(t * tt + r * seg, seg), :]
        vmem = buf_ref.at[:, :, r, :]
        copies.append(pltpu.make_async_copy(hbm, vmem, sem) if to_vmem
                      else pltpu.make_async_copy(vmem, hbm, sem))
    return copies


def _head_copies(hbm_refs, buf_ref, sem, l, b, to_vmem):
    n_seq = buf_ref.shape[1]
    copies = []
    for j, hbm_ref in enumerate(hbm_refs):
        for hd in range(X_HEADS):
            hbm = hbm_ref.at[l, pl.ds(b * n_seq, n_seq), :, hd, :]
            vmem = buf_ref.at[j, :, :, pl.ds(hd * X_HD, X_HD)]
            copies.append(pltpu.make_async_copy(hbm, vmem, sem) if to_vmem
                          else pltpu.make_async_copy(vmem, hbm, sem))
    return copies


def _next_grid_index():
    wraps = pl.program_id(1) + 1 == pl.num_programs(1)
    return pl.program_id(0) + wraps.astype(jnp.int32), jnp.where(wraps, 0, pl.program_id(1) + 1)


def _prefetched(fetch):
    step, n_steps = _grid_step()
    slot = lax.rem(step, 2)

    @pl.when(step == 0)
    def _():
        for c in fetch(pl.program_id(0), pl.program_id(1), slot):
            c.start()

    @pl.when(step + 1 < n_steps)
    def _():
        for c in fetch(*_next_grid_index(), 1 - slot):
            c.start()

    for c in fetch(pl.program_id(0), pl.program_id(1), slot):
        c.wait()
    return slot


def _flushed(flush, fill):
    step, n_steps = _grid_step()
    slot = lax.rem(step, 2)

    @pl.when(step >= 2)
    def _():
        for c in flush(slot):
            c.wait()

    fill(slot)
    for c in flush(slot):
        c.start()

    @pl.when(step == n_steps - 1)
    def _():
        @pl.when(n_steps > 1)
        def _():
            for c in flush(1 - slot):
                c.wait()
        for c in flush(slot):
            c.wait()


def _mixer_kernel(x_ref, ca0_ref, h0_ref, cb0_ref, gmix_ref, win_ref, caw_ref, cab_ref, wg_ref,
                  brg_ref, big_ref, lam_ref, cbw_ref, gv_ref, gsum_ref, ws_ref, bsf_ref, wout_ref,
                  xo_ref, cao_ref, ho_ref, cbo_ref, *rest, nb, gb, tt, chunk, emit_vc, gather_x):
    if gather_x:
        *rest, x_buf, x_sem = rest
    if emit_vc:
        vco_ref, ca_tail, cb_tail, h_state, wmix, ycat, zbuf = rest
    else:
        ca_tail, cb_tail, h_state, wmix, ycat, zbuf = rest
    rows = gb * tt
    seg = tt // SUBLANES
    groups = nb // gb

    if gather_x:
        slot = _prefetched(lambda b, t, into: _tile_copies(x_ref, x_buf.at[into], x_sem.at[into],
                                                           b, t, nb, tt, to_vmem=True))

    def read_x(g):
        if gather_x:
            return x_buf[slot, g * gb:(g + 1) * gb].reshape(rows, D_MODEL)
        return x_ref[g * gb:(g + 1) * gb].reshape(rows, D_MODEL)

    @pl.when(pl.program_id(1) == 0)
    def _():
        ca_tail[...] = ca0_ref[...]
        cb_tail[...] = cb0_ref[...]
        h_state[...] = h0_ref[...]
        ti = lax.broadcasted_iota(jnp.int32, (tt, tt), 0)
        tj = lax.broadcasted_iota(jnp.int32, (tt, tt), 1)
        time_i = (ti % SUBLANES) * seg + ti // SUBLANES
        time_j = (tj % SUBLANES) * seg + tj // SUBLANES
        allowed = (time_i >= time_j) & (time_i // chunk == time_j // chunk)
        for hd in range(C_HEADS):
            wmix[hd] = jnp.where(allowed, ws_ref[hd], 0.0).astype(BF16)

    def project_in(g):
        xn = _rms(read_x(g), gmix_ref[...]).astype(BF16)
        z = zbuf.at[g % Z_SLOTS]
        width = IN_COLS // PROJ_PARTS

        def part(c):
            z[:, c * width:(c + 1) * width] = _dot(xn, win_ref[:, c * width:(c + 1) * width])

        return z, [functools.partial(part, c) for c in range(PROJ_PARTS)]

    col = {}
    o = 0
    for name, width in (("xa", A_WIDTH), ("ga", A_WIDTH), ("xb", B_WIDTH), ("gb", B_WIDTH),
                        ("gc", B_WIDTH), ("uc", C_WIDTH), ("vc", C_WIDTH)):
        col[name] = slice(o, o + width)
        o += width
    half = A_WIDTH // 2

    def mix_matmuls(g, z):
        seqs = slice(g * gb, (g + 1) * gb)
        out = {}

        def gates_and_norm():
            xc = _causal_taps(z[:, col["xa"]].reshape(gb, seg, SUBLANES, A_WIDTH), ca_tail.at[seqs],
                              caw_ref[...])
            xc2 = (xc + cab_ref[...]).reshape(rows, A_WIDTH)
            xcb = xc2.astype(BF16)
            out["xc2"] = xc2
            out["gz0"] = _dot(xcb[:, :half], wg_ref[0])
            out["gz1"] = _dot(xcb[:, half:], wg_ref[1])
            v = _gelu(z[:, col["vc"]])
            sq = v * v
            sq_hi = sq.astype(BF16)
            sq_lo = (sq - sq_hi.astype(F32)).astype(BF16)
            out["v"] = v
            out["ms"] = _dot(sq_hi, gsum_ref[...]) + _dot(sq_lo, gsum_ref[...])

        def spatial_mix():
            vn = out.pop("v") * lax.rsqrt(out.pop("ms") + EPS) * gv_ref[...]
            if emit_vc:
                vco_ref[seqs] = vn.reshape(gb, tt, C_WIDTH)
            vnb = vn.astype(BF16)
            first_head = lax.broadcasted_iota(jnp.int32, (tt, LANES), 1) < C_HD
            mixed = []
            for b in range(gb):
                pieces = []
                for pair in range(C_WIDTH // LANES):
                    vp = vnb[b * tt:(b + 1) * tt, pair * LANES:(pair + 1) * LANES]
                    pieces.append(jnp.where(first_head, _dot(wmix[2 * pair], vp), _dot(wmix[2 * pair + 1], vp)))
                mixed.append(jnp.concatenate(pieces, axis=1) + bsf_ref[...])
            out["mixed"] = jnp.concatenate(mixed, axis=0)

        return out, [gates_and_norm, spatial_mix]

    def mix_elementwise(g, z, xc2, gz0, gz1, mixed):
        seqs = slice(g * gb, (g + 1) * gb)
        y = ycat.at[g * rows:(g + 1) * rows]
        r = jax.nn.sigmoid(jnp.concatenate([gz0[:, :half], gz1[:, :half]], axis=1) + brg_ref[...])
        gi = jax.nn.sigmoid(jnp.concatenate([gz0[:, half:], gz1[:, half:]], axis=1) + big_ref[...])
        neg_lam = -lam_ref[...]
        softplus = jnp.maximum(neg_lam, 0.0) + jnp.log1p(jnp.exp(-jnp.abs(neg_lam)))
        decay_rate = RG_C * softplus
        a = jnp.exp2((-LOG2_E) * decay_rate * r)
        one_minus_a2 = (1.0 + a * a) * jnp.tanh(decay_rate * r)
        root = jnp.where(one_minus_a2 > 0.0, one_minus_a2 * lax.rsqrt(one_minus_a2), 0.0)
        bterm = root * (gi * xc2)
        h = _segment_scan(a.reshape(gb, seg, SUBLANES, A_WIDTH),
                          bterm.reshape(gb, seg, SUBLANES, A_WIDTH), h_state.at[seqs])
        y[:, 0:A_WIDTH] = (_gelu(z[:, col["ga"]]) * h.reshape(rows, A_WIDTH)).astype(BF16)
        zb = _causal_taps((z[:, col["gc"]] * z[:, col["xb"]]).reshape(gb, seg, SUBLANES, B_WIDTH),
                          cb_tail.at[seqs], cbw_ref[...])
        y[:, A_WIDTH:A_WIDTH + B_WIDTH] = (z[:, col["gb"]] * zb.reshape(rows, B_WIDTH)).astype(BF16)
        y[:, A_WIDTH + B_WIDTH:] = (_gelu(z[:, col["uc"]]) * mixed).astype(BF16)

    def project_out(g):
        out = read_x(g) + _dot(ycat[g * rows:(g + 1) * rows], wout_ref[...])
        xo_ref[g * gb:(g + 1) * gb] = out.reshape(gb, tt, D_MODEL)

    def interleave(big_steps, small_steps):
        stride = max(1, len(big_steps) // max(1, len(small_steps)))
        pending = list(small_steps)
        for i, step in enumerate(big_steps):
            step()
            if pending and (i + 1) % stride == 0:
                pending.pop(0)()
        for step in pending:
            step()

    zs, small = {}, {}
    zs[0], parts = project_in(0)
    interleave(parts, [])
    small[0], steps = mix_matmuls(0, zs[0])
    if groups > 1:
        zs[1], parts = project_in(1)
        interleave(parts, steps)
    else:
        interleave([], steps)
    for g in range(groups):
        parts, steps = [], []
        if g + 2 < groups:
            zs[g + 2], parts = project_in(g + 2)
        if g + 1 < groups:
            small[g + 1], steps = mix_matmuls(g + 1, zs[g + 1])
        interleave(parts, steps)
        res = small.pop(g)
        mix_elementwise(g, zs.pop(g), res["xc2"], res["gz0"], res["gz1"], res["mixed"])
        project_out(g)
    cao_ref[...] = ca_tail[...]
    ho_ref[...] = h_state[...]
    cbo_ref[...] = cb_tail[...]


def _attn_kernel(x_ref, kb, vb, gx_ref, wq_ref, wo_ref, xo_ref, ocat, *, nb, tt):
    rows = nb * tt
    x = x_ref[...].reshape(rows, D_MODEL)
    xn = _rms(x, gx_ref[...]).astype(BF16)
    q = (_dot(xn, wq_ref[...]) * (X_HD ** -0.5)).astype(BF16)
    for seq in range(nb):
        r0 = seq * tt
        for hd in range(X_HEADS):
            cols = slice(hd * X_HD, (hd + 1) * X_HD)
            s = lax.dot_general(q[r0:r0 + tt, cols], kb[seq, :, cols], (((1,), (1,)), ((), ())),
                                preferred_element_type=F32)
            e = jnp.exp(s - jnp.max(s, axis=-1, keepdims=True))
            denom = jnp.sum(e, axis=-1, keepdims=True)
            oh = _dot(e.astype(BF16), vb[seq, :, cols]) / denom
            ocat[r0:r0 + tt, cols] = oh.astype(BF16)
    out = x + _dot(ocat[...], wo_ref[...])
    xo_ref[...] = out.reshape(nb, tt, D_MODEL)


def _ffn_kernel(x_ref, cf0_ref, gffn_ref, wup_ref, cfw_ref, wdown_ref, gfin_ref, xo_ref, cfo_ref,
                cf_tail, *rest, nb, tt, final_norm, scatter_out):
    rows = nb * tt
    seg = tt // SUBLANES

    @pl.when(pl.program_id(1) == 0)
    def _():
        cf_tail[...] = cf0_ref[...]

    x = x_ref[...].reshape(rows, D_MODEL)
    xn = _rms(x, gffn_ref[...]).astype(BF16)
    gu = _dot(xn, wup_ref[...])
    gconv = _causal_taps(gu[:, :D_FF].reshape(nb, seg, SUBLANES, D_FF), cf_tail, cfw_ref[...])
    cfo_ref[...] = cf_tail[...]
    act = (jax.nn.silu(gconv.reshape(rows, D_FF)) * gu[:, D_FF:]).astype(BF16)
    out = x + _dot(act, wdown_ref[...])
    if final_norm:
        out = _rms(out, gfin_ref[...])
    if not scatter_out:
        xo_ref[...] = out.reshape(nb, tt, D_MODEL)
        return

    out_buf, out_sem = rest

    def flush(from_slot):
        return _tile_copies(xo_ref, out_buf.at[from_slot], out_sem.at[from_slot],
                            pl.program_id(0), pl.program_id(1), nb, tt, to_vmem=False)

    def fill(slot):
        out_buf[slot] = out.reshape(nb, seg, SUBLANES, D_MODEL)

    _flushed(flush, fill)


def _kv_kernel(mem_ref, wk_ref, wv_ref, ko_hbm, vo_hbm, kb_ref, vb_ref, kv_buf, sem):
    n_seq = mem_ref.shape[0]
    m = mem_ref[...].reshape(n_seq * N_MEM, D_MODEL).astype(BF16)
    k = _dot(m, wk_ref[...]).reshape(n_seq, N_MEM, D_MODEL)
    v = _dot(m, wv_ref[...]).reshape(n_seq, N_MEM, D_MODEL)
    kb_ref[...] = k.astype(BF16)
    vb_ref[...] = v.astype(BF16)

    def flush(from_slot):
        return _head_copies((ko_hbm, vo_hbm), kv_buf.at[from_slot], sem.at[from_slot],
                            pl.program_id(0), pl.program_id(1), to_vmem=False)

    def fill(slot):
        kv_buf[slot, 0] = k
        kv_buf[slot, 1] = v

    _flushed(flush, fill)


def _kv_cast_kernel(k_hbm, v_hbm, kb_ref, vb_ref, kv_buf, sem):
    slot = _prefetched(lambda l, b, into: _head_copies((k_hbm, v_hbm), kv_buf.at[into], sem.at[into],
                                                       l, b, to_vmem=True))
    kb_ref[...] = kv_buf[slot, 0].astype(BF16)
    vb_ref[...] = kv_buf[slot, 1].astype(BF16)


def _whole():
    return pl.BlockSpec(memory_space=pltpu.MemorySpace.VMEM)


def _of_layer(arr, l):
    zeros = (0,) * (arr.ndim - 1)
    return pl.BlockSpec((None,) + arr.shape[1:], lambda b, t: (l,) + zeros, pipeline_mode=pl.Buffered(1))


def _params():
    return pltpu.CompilerParams(dimension_semantics=("arbitrary", "arbitrary"),
                                vmem_limit_bytes=VMEM_LIMIT_BYTES)


def _tile_specs(nb, tt):
    x_spec = pl.BlockSpec((nb, tt, D_MODEL), lambda b, t: (b, t, 0))

    def state_spec(n, width):
        return pl.BlockSpec((nb, n, SUBLANES, width), lambda b, t: (b, 0, 0, 0))

    return x_spec, state_spec


def _mixer_call(x, ca0, h0, cb0, p, l, nb, tt, chunk, emit_vc, gather_x):
    bsz, seq, _ = x.shape
    layer_params = [p[name] for name in ("g_mix", "w_in", "conv_a_w", "conv_a_b", "w_gates", "b_rg", "b_ig",
                                         "lam", "conv_b_w", "g_v")]
    mix_params = [p[name] for name in ("w_s", "b_s_full", "w_out")]
    x_spec, state_spec = _tile_specs(nb, tt)
    h_spec = pl.BlockSpec((nb, SUBLANES, A_WIDTH), lambda b, t: (b, 0, 0))
    out_shape = [jax.ShapeDtypeStruct(x.shape, F32),
                 jax.ShapeDtypeStruct(ca0.shape, F32),
                 jax.ShapeDtypeStruct(h0.shape, F32),
                 jax.ShapeDtypeStruct(cb0.shape, F32)]
    out_specs = [x_spec, state_spec(A_CONV - 1, A_WIDTH), h_spec, state_spec(B_CONV - 1, B_WIDTH)]
    if emit_vc:
        out_shape.append(jax.ShapeDtypeStruct((bsz, seq, C_WIDTH), F32))
        out_specs.append(pl.BlockSpec((nb, tt, C_WIDTH), lambda b, t: (b, t, 0)))
    gb = nb if seq == tt else MIXER_GROUP
    kern = functools.partial(_mixer_kernel, nb=nb, gb=gb, tt=tt, chunk=chunk, emit_vc=emit_vc,
                             gather_x=gather_x)
    gather_scratch = [pltpu.VMEM((2, nb, tt // SUBLANES, SUBLANES, D_MODEL), F32),
                      pltpu.SemaphoreType.DMA((2,))] if gather_x else []
    return pl.pallas_call(
        kern,
        grid=(bsz // nb, seq // tt),
        in_specs=([pl.BlockSpec(memory_space=pl.ANY) if gather_x else x_spec,
                   state_spec(A_CONV - 1, A_WIDTH), h_spec, state_spec(B_CONV - 1, B_WIDTH)]
                  + [_of_layer(a, l) for a in layer_params] + [_whole()]
                  + [_of_layer(a, l) for a in mix_params]),
        out_specs=out_specs,
        out_shape=out_shape,
        scratch_shapes=[pltpu.VMEM((nb, A_CONV - 1, SUBLANES, A_WIDTH), F32),
                        pltpu.VMEM((nb, B_CONV - 1, SUBLANES, B_WIDTH), F32),
                        pltpu.VMEM((nb, SUBLANES, A_WIDTH), F32),
                        pltpu.VMEM((C_HEADS, tt, tt), BF16),
                        pltpu.VMEM((nb * tt, D_MODEL), BF16),
                        pltpu.VMEM((Z_SLOTS, gb * tt, IN_COLS), F32)] + gather_scratch,
        compiler_params=_params(),
        name="mixer",
    )(x, ca0, h0, cb0, *layer_params, p["group_mean"], *mix_params)


def _attn_call(x, kb, vb, p, l, nb, tt):
    bsz, seq, _ = x.shape
    x_spec, _ = _tile_specs(nb, tt)
    kv_spec = pl.BlockSpec((None, nb, N_MEM, D_MODEL), lambda b, t: (l, b, 0, 0))
    weights = [p["g_x"], p["w_q"], p["w_o"]]
    kern = functools.partial(_attn_kernel, nb=nb, tt=tt)
    return pl.pallas_call(
        kern,
        grid=(bsz // nb, seq // tt),
        in_specs=[x_spec, kv_spec, kv_spec] + [_of_layer(a, l) for a in weights],
        out_specs=x_spec,
        out_shape=jax.ShapeDtypeStruct(x.shape, F32),
        scratch_shapes=[pltpu.VMEM((nb * tt, D_MODEL), BF16)],
        compiler_params=_params(),
        name="attn",
    )(x, kb, vb, *weights)


def _ffn_call(x, cf0, p, l, nb, tt, final_norm, scatter_out):
    bsz, seq, _ = x.shape
    x_spec, state_spec = _tile_specs(nb, tt)
    weights = [p["g_ffn"], p["w_up"], p["conv_f_w"], p["w_down"]]
    kern = functools.partial(_ffn_kernel, nb=nb, tt=tt, final_norm=final_norm, scatter_out=scatter_out)
    scatter_scratch = [pltpu.VMEM((2, nb, tt // SUBLANES, SUBLANES, D_MODEL), F32),
                       pltpu.SemaphoreType.DMA((2,))] if scatter_out else []
    return pl.pallas_call(
        kern,
        grid=(bsz // nb, seq // tt),
        in_specs=[x_spec, state_spec(FFN_CONV - 1, D_FF)] + [_of_layer(a, l) for a in weights] + [_whole()],
        out_specs=[pl.BlockSpec(memory_space=pl.ANY) if scatter_out else x_spec,
                   state_spec(FFN_CONV - 1, D_FF)],
        out_shape=[jax.ShapeDtypeStruct(x.shape, F32), jax.ShapeDtypeStruct(cf0.shape, F32)],
        scratch_shapes=[pltpu.VMEM((nb, FFN_CONV - 1, SUBLANES, D_FF), F32)] + scatter_scratch,
        compiler_params=_params(),
        name="ffn",
    )(x, cf0, *weights, p["g_final"])


def _kv_call(mem, w_k, w_v):
    bsz = mem.shape[0]
    depth = w_k.shape[0]
    n_seq = min(bsz, KV_BATCH_TILE)
    w_spec = pl.BlockSpec((None, D_MODEL, D_MODEL), lambda l, b: (l, 0, 0))
    out5 = pl.BlockSpec(memory_space=pl.ANY)
    out4 = pl.BlockSpec((None, n_seq, N_MEM, D_MODEL), lambda l, b: (l, b, 0, 0))
    shape5 = jax.ShapeDtypeStruct((depth, bsz, N_MEM, X_HEADS, X_HD), F32)
    shape4 = jax.ShapeDtypeStruct((depth, bsz, N_MEM, D_MODEL), BF16)
    return pl.pallas_call(
        _kv_kernel,
        grid=(depth, bsz // n_seq),
        in_specs=[pl.BlockSpec((n_seq, N_MEM, D_MODEL), lambda l, b: (b, 0, 0)), w_spec, w_spec],
        out_specs=[out5, out5, out4, out4],
        out_shape=[shape5, shape5, shape4, shape4],
        scratch_shapes=[pltpu.VMEM((2, 2, n_seq, N_MEM, D_MODEL), F32), pltpu.SemaphoreType.DMA((2,))],
        compiler_params=_params(),
        name="memory_kv",
    )(mem, w_k, w_v)


def _kv_cast_call(cache_k, cache_v):
    depth, bsz = cache_k.shape[:2]
    n_seq = min(bsz, KV_BATCH_TILE)
    in_spec = pl.BlockSpec(memory_space=pl.ANY)
    out_spec = pl.BlockSpec((None, n_seq, N_MEM, D_MODEL), lambda l, b: (l, b, 0, 0))
    shape = jax.ShapeDtypeStruct((depth, bsz, N_MEM, D_MODEL), BF16)
    return pl.pallas_call(
        _kv_cast_kernel,
        grid=(depth, bsz // n_seq),
        in_specs=[in_spec, in_spec],
        out_specs=[out_spec, out_spec],
        out_shape=[shape, shape],
        scratch_shapes=[pltpu.VMEM((2, 2, n_seq, N_MEM, D_MODEL), F32), pltpu.SemaphoreType.DMA((2,))],
        compiler_params=_params(),
        name="cache_kv_cast",
    )(cache_k, cache_v)


def _block_diag(blocks):
    n, r, c = blocks.shape[-3:]
    eye = jnp.eye(n, dtype=blocks.dtype)
    return (blocks[..., :, :, None, :] * eye[:, None, :, None]).reshape(blocks.shape[:-3] + (n * r, n * c))


def _tile_times(tt):
    i = jnp.arange(tt)
    return (i % SUBLANES) * (tt // SUBLANES) + i // SUBLANES


def _to_segments(x, tt):
    bsz, seq, c = x.shape
    return x.reshape(bsz, seq // tt, SUBLANES, tt // SUBLANES, c).swapaxes(2, 3).reshape(bsz, seq, c)


def _from_segments(x, tt):
    bsz, seq, c = x.shape
    return x.reshape(bsz, seq // tt, tt // SUBLANES, SUBLANES, c).swapaxes(2, 3).reshape(bsz, seq, c)


def _on_sublanes(state):
    return jnp.broadcast_to(state[..., None, :], state.shape[:-1] + (SUBLANES, state.shape[-1]))


def _shared_params(g_mix, w_in, conv_a_w, conv_a_b, w_rg, b_rg, w_ig, b_ig, lam, conv_b_w, g_v, w_out,
                   g_x, w_q, w_o, g_ffn, w_up, conv_f_w, w_down, g_final):
    per_half = A_HEADS // 2
    gates = [jnp.concatenate([_block_diag(w_rg[:, j * per_half:(j + 1) * per_half]),
                              _block_diag(w_ig[:, j * per_half:(j + 1) * per_half])], axis=-1)
             for j in range(2)]
    group = jnp.arange(C_WIDTH) // C_HD

    def row(a):
        return a[:, None, :]

    def taps(a):
        return a[:, :, None, :]

    return dict(
        g_mix=row(g_mix), w_in=w_in.astype(BF16), conv_a_w=taps(conv_a_w), conv_a_b=row(conv_a_b),
        w_gates=jnp.stack(gates, axis=1).astype(BF16), b_rg=row(b_rg), b_ig=row(b_ig), lam=row(lam),
        conv_b_w=taps(conv_b_w), g_v=row(g_v),
        group_mean=((group[:, None] == group[None, :]).astype(F32) / C_HD).astype(BF16),
        w_out=w_out.astype(BF16), g_x=row(g_x), w_q=w_q.astype(BF16), w_o=w_o.astype(BF16),
        g_ffn=row(g_ffn), w_up=w_up.astype(BF16), conv_f_w=taps(conv_f_w), w_down=w_down.astype(BF16),
        g_final=g_final[None])


def _tile_params(tt, chunk, w_s, b_s):
    pos = _tile_times(tt) % chunk
    return dict(w_s=w_s[:, :, pos][:, :, :, pos],
                b_s_full=jnp.repeat(jnp.swapaxes(b_s[:, :, pos], 1, 2), C_HD, axis=2))


def _tiles(bsz, seq):
    if seq <= MLP_CHUNK:
        return bsz, seq
    return PROMPT_BATCH_TILE, PROMPT_TIME_TILE


def _trunk(x, kb, vb, states, p, emit_vc):
    bsz, seq, _ = x.shape
    nb, tt = _tiles(bsz, seq)
    chunk = min(seq, MLP_CHUNK)
    by_dma = seq > tt
    if not by_dma:
        x = _to_segments(x, tt)
    new_states = [[] for _ in range(4)]
    vcs = []
    for l in range(DEPTH):
        last = l == DEPTH - 1
        ca0, h0, cb0, cf0 = (s[l] for s in states)
        mix = _mixer_call(x, ca0, h0, cb0, p, l, nb, tt, chunk, emit_vc, by_dma and l == 0)
        x = _attn_call(mix[0], kb, vb, p, l, *((1, ATTN_TIME_TILE) if by_dma else (nb, tt)))
        x, cf = _ffn_call(x, cf0, p, l, nb, tt, last, by_dma and last)
        for acc, val in zip(new_states, (mix[1], mix[2], mix[3], cf)):
            acc.append(val)
        if emit_vc:
            vcs.append(mix[4])
    new_states = tuple(jnp.stack(s)[..., SUBLANES - 1, :] for s in new_states)
    vc = _from_segments(jnp.concatenate(vcs, axis=0), tt).reshape(DEPTH, bsz, seq, C_WIDTH) if emit_vc else None
    return (x if by_dma else _from_segments(x, tt)), new_states, vc


def kernel(x_prompt, x_sample, mem_prompt, cache_mem_k, cache_mem_v, state_conv_a, state_h_a, state_conv_b, state_conv_ffn, g_mix, w_in, conv_a_w, conv_a_b, w_rg, b_rg, w_ig, b_ig, lam, conv_b_w, g_v, w_s, b_s, w_out, g_x, w_q, w_k, w_v, w_o, g_ffn, w_up, conv_f_w, w_down, g_final):
    bp, seq_p, _ = x_prompt.shape
    bs, seq_s, _ = x_sample.shape
    shared = _shared_params(g_mix, w_in, conv_a_w, conv_a_b, w_rg, b_rg, w_ig, b_ig, lam, conv_b_w, g_v,
                            w_out, g_x, w_q, w_o, g_ffn, w_up, conv_f_w, w_down, g_final)

    def tile_params(bsz, seq):
        return dict(shared, **_tile_params(_tiles(bsz, seq)[1], min(seq, MLP_CHUNK), w_s, b_s))

    mem_k, mem_v, kb, vb = _kv_call(mem_prompt, w_k.astype(BF16), w_v.astype(BF16))
    zero_states = tuple(jnp.zeros((DEPTH, bp) + shape + (SUBLANES, width), F32)
                        for shape, width in (((A_CONV - 1,), A_WIDTH), ((), A_WIDTH),
                                             ((B_CONV - 1,), B_WIDTH), ((FFN_CONV - 1,), D_FF)))
    y_p, states_p, _ = _trunk(x_prompt, kb, vb, zero_states, tile_params(bp, seq_p), False)
    carried = tuple(_on_sublanes(s) for s in (state_conv_a, state_h_a, state_conv_b, state_conv_ffn))
    cache_kb, cache_vb = _kv_cast_call(cache_mem_k, cache_mem_v)
    y_s, states_s, vc_s = _trunk(x_sample, cache_kb, cache_vb, carried, tile_params(bs, seq_s), True)
    return (y_p, y_s) + states_p + (mem_k, mem_v) + states_s + (vc_s,)
```

```python
import functools
import math

import jax
import jax.numpy as jnp
from jax import lax
from jax.experimental import pallas as pl
from jax.experimental.pallas import tpu as pltpu

D_MODEL = 1024
DEPTH = 2
A_WIDTH = 512
A_HEADS = 8
A_HD = 64
A_CONV = 4
RG_C = 8.0
B_WIDTH = 256
B_CONV = 3
C_WIDTH = 256
C_HEADS = 4
C_HD = 64
MLP_CHUNK = 128
N_MEM = 256
X_HEADS = 4
X_HD = 256
D_FF = 2816
FFN_CONV = 3
EPS = 1e-6
IN_COLS = 2 * A_WIDTH + 3 * B_WIDTH + 2 * C_WIDTH

SUBLANES = 8
LANES = 128
MXU_DIM = 256
V7X_VMEM_BYTES = 64 * 1024 * 1024
VMEM_LIMIT_BYTES = V7X_VMEM_BYTES * 7 // 8
PROMPT_TIME_TILE = 256
PROMPT_BATCH_TILE = 4
ATTN_TIME_TILE = 1024
KV_BATCH_TILE = 4
Z_SLOTS = 3
PROJ_PARTS = IN_COLS // MXU_DIM

LOG2_E = math.log2(math.e)
GELU_C0 = math.sqrt(2.0 / math.pi)
GELU_C1 = 0.044715

F32 = jnp.float32
BF16 = jnp.bfloat16


def _dot(a, b):
    return jnp.dot(a, b, preferred_element_type=F32)


def _rms(x, g):
    ms = jnp.mean(x * x, axis=-1, keepdims=True)
    return x * lax.rsqrt(ms + EPS) * g


def _gelu(x):
    scale = -2.0 * LOG2_E * GELU_C0
    return x / (1.0 + jnp.exp2((x * x * (scale * GELU_C1) + scale) * x))


def _sublane_iota(shape):
    return lax.broadcasted_iota(jnp.int32, shape, len(shape) - 2)


def _causal_taps(cur, tail_ref, w):
    n_prev = tail_ref.shape[1]
    seg = cur.shape[1]
    last = cur[:, seg - n_prev:]
    wrapped = pltpu.roll(jnp.where(_sublane_iota(last.shape) == SUBLANES - 1, tail_ref[...], last), 1, 2)
    tail_ref[...] = last
    xp = jnp.concatenate([wrapped, cur], axis=1)
    acc = xp[:, 0:seg] * w[0]
    for j in range(1, n_prev + 1):
        acc = acc + xp[:, j:j + seg] * w[j]
    return acc


def _segment_scan(a, b, h_ref):
    seg = a.shape[1]
    local, decay = b[:, 0], a[:, 0]
    locals_, decays = [local], [decay]
    for s in range(1, seg):
        local = a[:, s] * local + b[:, s]
        decay = a[:, s] * decay
        locals_.append(local)
        decays.append(decay)
    h_in = h_ref[:, SUBLANES - 1:SUBLANES, :]
    row = _sublane_iota(local.shape)
    seg_a, seg_b = decay, local
    d = 1
    while d < SUBLANES:
        a_prev = pltpu.roll(seg_a, d, 1)
        b_prev = pltpu.roll(seg_b, d, 1)
        keep = row >= d
        seg_b = jnp.where(keep, seg_a * b_prev + seg_b, seg_b)
        seg_a = jnp.where(keep, seg_a * a_prev, seg_a)
        d *= 2
    h_end = seg_a * h_in + seg_b
    h_ref[...] = h_end
    h_start = jnp.where(row == 0, h_in, pltpu.roll(h_end, 1, 1))
    return jnp.stack([locals_[s] + decays[s] * h_start for s in range(seg)], axis=1)


def _grid_step():
    return (pl.program_id(0) * pl.num_programs(1) + pl.program_id(1),
            pl.num_programs(0) * pl.num_programs(1))


def _tile_copies(hbm_ref, buf_ref, sem, b, t, nb, tt, to_vmem):
    seg = tt // SUBLANES
    copies = []
    for r in range(SUBLANES):
        hbm = hbm_ref.at[pl.ds(b * nb, nb), pl.ds(t * tt + r * seg, seg), :]
        vmem = buf_ref.at[:, :, r, :]
        copies.append(pltpu.make_async_copy(hbm, vmem, sem) if to_vmem
                      else pltpu.make_async_copy(vmem, hbm, sem))
    return copies


def _head_copies(hbm_refs, buf_ref, sem, l, b, to_vmem):
    n_seq = buf_ref.shape[1]
    copies = []
    for j, hbm_ref in enumerate(hbm_refs):
        for hd in range(X_HEADS):
            hbm = hbm_ref.at[l, pl.ds(b * n_seq, n_seq), :, hd, :]
            vmem = buf_ref.at[j, :, :, pl.ds(hd * X_HD, X_HD)]
            copies.append(pltpu.make_async_copy(hbm, vmem, sem) if to_vmem
                          else pltpu.make_async_copy(vmem, hbm, sem))
    return copies


def _next_grid_index():
    wraps = pl.program_id(1) + 1 == pl.num_programs(1)
    return pl.program_id(0) + wraps.astype(jnp.int32), jnp.where(wraps, 0, pl.program_id(1) + 1)


def _prefetched(fetch):
    step, n_steps = _grid_step()
    slot = lax.rem(step, 2)

    @pl.when(step == 0)
    def _():
        for c in fetch(pl.program_id(0), pl.program_id(1), slot):
            c.start()

    @pl.when(step + 1 < n_steps)
    def _():
        for c in fetch(*_next_grid_index(), 1 - slot):
            c.start()

    for c in fetch(pl.program_id(0), pl.program_id(1), slot):
        c.wait()
    return slot


def _flushed(flush, fill):
    step, n_steps = _grid_step()
    slot = lax.rem(step, 2)

    @pl.when(step >= 2)
    def _():
        for c in flush(slot):
            c.wait()

    fill(slot)
    for c in flush(slot):
        c.start()

    @pl.when(step == n_steps - 1)
    def _():
        @pl.when(n_steps > 1)
        def _():
            for c in flush(1 - slot):
                c.wait()
        for c in flush(slot):
            c.wait()


def _mixer_kernel(x_ref, ca0_ref, h0_ref, cb0_ref, gmix_ref, win_ref, caw_ref, cab_ref, wg_ref,
                  brg_ref, big_ref, lam_ref, cbw_ref, gv_ref, gsum_ref, ws_ref, bsf_ref, wout_ref,
                  xo_ref, cao_ref, ho_ref, cbo_ref, *rest, nb, gb, tt, chunk, emit_vc, gather_x):
    if gather_x:
        *rest, x_buf, x_sem = rest
    if emit_vc:
        vco_ref, ca_tail, cb_tail, h_state, wmix, ycat, zbuf = rest
    else:
        ca_tail, cb_tail, h_state, wmix, ycat, zbuf = rest
    rows = gb * tt
    seg = tt // SUBLANES
    groups = nb // gb

    if gather_x:
        slot = _prefetched(lambda b, t, into: _tile_copies(x_ref, x_buf.at[into], x_sem.at[into],
                                                           b, t, nb, tt, to_vmem=True))

    def read_x(g):
        if gather_x:
            return x_buf[slot, g * gb:(g + 1) * gb].reshape(rows, D_MODEL)
        return x_ref[g * gb:(g + 1) * gb].reshape(rows, D_MODEL)

    @pl.when(pl.program_id(1) == 0)
    def _():
        ca_tail[...] = ca0_ref[...]
        cb_tail[...] = cb0_ref[...]
        h_state[...] = h0_ref[...]
        ti = lax.broadcasted_iota(jnp.int32, (tt, tt), 0)
        tj = lax.broadcasted_iota(jnp.int32, (tt, tt), 1)
        time_i = (ti % SUBLANES) * seg + ti // SUBLANES
        time_j = (tj % SUBLANES) * seg + tj // SUBLANES
        allowed = (time_i >= time_j) & (time_i // chunk == time_j // chunk)
        for hd in range(C_HEADS):
            wmix[hd] = jnp.where(allowed, ws_ref[hd], 0.0).astype(BF16)

    def project_in(g):
        xn = _rms(read_x(g), gmix_ref[...]).astype(BF16)
        z = zbuf.at[g % Z_SLOTS]
        width = IN_COLS // PROJ_PARTS

        def part(c):
            z[:, c * width:(c + 1) * width] = _dot(xn, win_ref[:, c * width:(c + 1) * width])

        return z, [functools.partial(part, c) for c in range(PROJ_PARTS)]

    col = {}
    o = 0
    for name, width in (("xa", A_WIDTH), ("ga", A_WIDTH), ("xb", B_WIDTH), ("gb", B_WIDTH),
                        ("gc", B_WIDTH), ("uc", C_WIDTH), ("vc", C_WIDTH)):
        col[name] = slice(o, o + width)
        o += width
    half = A_WIDTH // 2

    def mix_matmuls(g, z):
        seqs = slice(g * gb, (g + 1) * gb)
        out = {}

        def gates_and_norm():
            xc = _causal_taps(z[:, col["xa"]].reshape(gb, seg, SUBLANES, A_WIDTH), ca_tail.at[seqs],
                              caw_ref[...])
            xc2 = (xc + cab_ref[...]).reshape(rows, A_WIDTH)
            xcb = xc2.astype(BF16)
            out["xc2"] = xc2
            out["gz0"] = _dot(xcb[:, :half], wg_ref[0])
            out["gz1"] = _dot(xcb[:, half:], wg_ref[1])
            v = _gelu(z[:, col["vc"]])
            sq = v * v
            sq_hi = sq.astype(BF16)
            sq_lo = (sq - sq_hi.astype(F32)).astype(BF16)
            out["v"] = v
            out["ms"] = _dot(sq_hi, gsum_ref[...]) + _dot(sq_lo, gsum_ref[...])

        def spatial_mix():
            vn = out.pop("v") * lax.rsqrt(out.pop("ms") + EPS) * gv_ref[...]
            if emit_vc:
                vco_ref[seqs] = vn.reshape(gb, tt, C_WIDTH)
            vnb = vn.astype(BF16)
            first_head = lax.broadcasted_iota(jnp.int32, (tt, LANES), 1) < C_HD
            mixed = []
            for b in range(gb):
                pieces = []
                for pair in range(C_WIDTH // LANES):
                    vp = vnb[b * tt:(b + 1) * tt, pair * LANES:(pair + 1) * LANES]
                    pieces.append(jnp.where(first_head, _dot(wmix[2 * pair], vp), _dot(wmix[2 * pair + 1], vp)))
                mixed.append(jnp.concatenate(pieces, axis=1) + bsf_ref[...])
            out["mixed"] = jnp.concatenate(mixed, axis=0)

        return out, [gates_and_norm, spatial_mix]

    def mix_elementwise(g, z, xc2, gz0, gz1, mixed):
        seqs = slice(g * gb, (g + 1) * gb)
        y = ycat.at[g * rows:(g + 1) * rows]
        r = jax.nn.sigmoid(jnp.concatenate([gz0[:, :half], gz1[:, :half]], axis=1) + brg_ref[...])
        gi = jax.nn.sigmoid(jnp.concatenate([gz0[:, half:], gz1[:, half:]], axis=1) + big_ref[...])
        neg_lam = -lam_ref[...]
        softplus = jnp.maximum(neg_lam, 0.0) + jnp.log1p(jnp.exp(-jnp.abs(neg_lam)))
        decay_rate = RG_C * softplus
        a = jnp.exp2((-LOG2_E) * decay_rate * r)
        one_minus_a2 = (1.0 + a * a) * jnp.tanh(decay_rate * r)
        root = jnp.where(one_minus_a2 > 0.0, one_minus_a2 * lax.rsqrt(one_minus_a2), 0.0)
        bterm = root * (gi * xc2)
        h = _segment_scan(a.reshape(gb, seg, SUBLANES, A_WIDTH),
                          bterm.reshape(gb, seg, SUBLANES, A_WIDTH), h_state.at[seqs])
        y[:, 0:A_WIDTH] = (_gelu(z[:, col["ga"]]) * h.reshape(rows, A_WIDTH)).astype(BF16)
        zb = _causal_taps((z[:, col["gc"]] * z[:, col["xb"]]).reshape(gb, seg, SUBLANES, B_WIDTH),
                          cb_tail.at[seqs], cbw_ref[...])
        y[:, A_WIDTH:A_WIDTH + B_WIDTH] = (z[:, col["gb"]] * zb.reshape(rows, B_WIDTH)).astype(BF16)
        y[:, A_WIDTH + B_WIDTH:] = (_gelu(z[:, col["uc"]]) * mixed).astype(BF16)

    def project_out(g):
        out = read_x(g) + _dot(ycat[g * rows:(g + 1) * rows], wout_ref[...])
        xo_ref[g * gb:(g + 1) * gb] = out.reshape(gb, tt, D_MODEL)

    def interleave(big_steps, small_steps):
        stride = max(1, len(big_steps) // max(1, len(small_steps)))
        pending = list(small_steps)
        for i, step in enumerate(big_steps):
            step()
            if pending and (i + 1) % stride == 0:
                pending.pop(0)()
        for step in pending:
            step()

    zs, small = {}, {}
    zs[0], parts = project_in(0)
    interleave(parts, [])
    small[0], steps = mix_matmuls(0, zs[0])
    if groups > 1:
        zs[1], parts = project_in(1)
        interleave(parts, steps)
    else:
        interleave([], steps)
    for g in range(groups):
        parts, steps = [], []
        if g + 2 < groups:
            zs[g + 2], parts = project_in(g + 2)
        if g + 1 < groups:
            small[g + 1], steps = mix_matmuls(g + 1, zs[g + 1])
        interleave(parts, steps)
        res = small.pop(g)
        mix_elementwise(g, zs.pop(g), res["xc2"], res["gz0"], res["gz1"], res["mixed"])
        project_out(g)
    cao_ref[...] = ca_tail[...]
    ho_ref[...] = h_state[...]
    cbo_ref[...] = cb_tail[...]


def _attn_kernel(x_ref, kb, vb, gx_ref, wq_ref, wo_ref, xo_ref, ocat, *, nb, tt):
    rows = nb * tt
    x = x_ref[...].reshape(rows, D_MODEL)
    xn = _rms(x, gx_ref[...]).astype(BF16)
    q = (_dot(xn, wq_ref[...]) * (X_HD ** -0.5)).astype(BF16)
    for seq in range(nb):
        r0 = seq * tt
        for hd in range(X_HEADS):
            cols = slice(hd * X_HD, (hd + 1) * X_HD)
            s = lax.dot_general(q[r0:r0 + tt, cols], kb[seq, :, cols], (((1,), (1,)), ((), ())),
                                preferred_element_type=F32)
            e = jnp.exp(s - jnp.max(s, axis=-1, keepdims=True))
            denom = jnp.sum(e, axis=-1, keepdims=True)
            oh = _dot(e.astype(BF16), vb[seq, :, cols]) / denom
            ocat[r0:r0 + tt, cols] = oh.astype(BF16)
    out = x + _dot(ocat[...], wo_ref[...])
    xo_ref[...] = out.reshape(nb, tt, D_MODEL)


def _ffn_kernel(x_ref, cf0_ref, gffn_ref, wup_ref, cfw_ref, wdown_ref, gfin_ref, xo_ref, cfo_ref,
                cf_tail, *rest, nb, tt, final_norm, scatter_out):
    rows = nb * tt
    seg = tt // SUBLANES

    @pl.when(pl.program_id(1) == 0)
    def _():
        cf_tail[...] = cf0_ref[...]

    x = x_ref[...].reshape(rows, D_MODEL)
    xn = _rms(x, gffn_ref[...]).astype(BF16)
    gu = _dot(xn, wup_ref[...])
    gconv = _causal_taps(gu[:, :D_FF].reshape(nb, seg, SUBLANES, D_FF), cf_tail, cfw_ref[...])
    cfo_ref[...] = cf_tail[...]
    act = (jax.nn.silu(gconv.reshape(rows, D_FF)) * gu[:, D_FF:]).astype(BF16)
    out = x + _dot(act, wdown_ref[...])
    if final_norm:
        out = _rms(out, gfin_ref[...])
    if not scatter_out:
        xo_ref[...] = out.reshape(nb, tt, D_MODEL)
        return

    out_buf, out_sem = rest

    def flush(from_slot):
        return _tile_copies(xo_ref, out_buf.at[from_slot], out_sem.at[from_slot],
                            pl.program_id(0), pl.program_id(1), nb, tt, to_vmem=False)

    def fill(slot):
        out_buf[slot] = out.reshape(nb, seg, SUBLANES, D_MODEL)

    _flushed(flush, fill)


def _kv_kernel(mem_ref, wk_ref, wv_ref, ko_hbm, vo_hbm, kb_ref, vb_ref, kv_buf, sem):
    n_seq = mem_ref.shape[0]
    m = mem_ref[...].reshape(n_seq * N_MEM, D_MODEL).astype(BF16)
    k = _dot(m, wk_ref[...]).reshape(n_seq, N_MEM, D_MODEL)
    v = _dot(m, wv_ref[...]).reshape(n_seq, N_MEM, D_MODEL)
    kb_ref[...] = k.astype(BF16)
    vb_ref[...] = v.astype(BF16)

    def flush(from_slot):
        return _head_copies((ko_hbm, vo_hbm), kv_buf.at[from_slot], sem.at[from_slot],
                            pl.program_id(0), pl.program_id(1), to_vmem=False)

    def fill(slot):
        kv_buf[slot, 0] = k
        kv_buf[slot, 1] = v

    _flushed(flush, fill)


def _kv_cast_kernel(k_hbm, v_hbm, kb_ref, vb_ref, kv_buf, sem):
    slot = _prefetched(lambda l, b, into: _head_copies((k_hbm, v_hbm), kv_buf.at[into], sem.at[into],
                                                       l, b, to_vmem=True))
    kb_ref[...] = kv_buf[slot, 0].astype(BF16)
    vb_ref[...] = kv_buf[slot, 1].astype(BF16)


def _whole():
    return pl.BlockSpec(memory_space=pltpu.MemorySpace.VMEM)


def _of_layer(arr, l):
    zeros = (0,) * (arr.ndim - 1)
    return pl.BlockSpec((None,) + arr.shape[1:], lambda b, t: (l,) + zeros, pipeline_mode=pl.Buffered(1))


def _params():
    return pltpu.CompilerParams(dimension_semantics=("arbitrary", "arbitrary"),
                                vmem_limit_bytes=VMEM_LIMIT_BYTES)


def _tile_specs(nb, tt):
    x_spec = pl.BlockSpec((nb, tt, D_MODEL), lambda b, t: (b, t, 0))

    def state_spec(n, width):
        return pl.BlockSpec((nb, n, SUBLANES, width), lambda b, t: (b, 0, 0, 0))

    return x_spec, state_spec


def _mixer_call(x, ca0, h0, cb0, p, l, nb, tt, chunk, emit_vc, gather_x):
    bsz, seq, _ = x.shape
    layer_params = [p[name] for name in ("g_mix", "w_in", "conv_a_w", "conv_a_b", "w_gates", "b_rg", "b_ig",
                                         "lam", "conv_b_w", "g_v")]
    mix_params = [p[name] for name in ("w_s", "b_s_full", "w_out")]
    x_spec, state_spec = _tile_specs(nb, tt)
    h_spec = pl.BlockSpec((nb, SUBLANES, A_WIDTH), lambda b, t: (b, 0, 0))
    out_shape = [jax.ShapeDtypeStruct(x.shape, F32),
                 jax.ShapeDtypeStruct(ca0.shape, F32),
                 jax.ShapeDtypeStruct(h0.shape, F32),
                 jax.ShapeDtypeStruct(cb0.shape, F32)]
    out_specs = [x_spec, state_spec(A_CONV - 1, A_WIDTH), h_spec, state_spec(B_CONV - 1, B_WIDTH)]
    if emit_vc:
        out_shape.append(jax.ShapeDtypeStruct((bsz, seq, C_WIDTH), F32))
        out_specs.append(pl.BlockSpec((nb, tt, C_WIDTH), lambda b, t: (b, t, 0)))
    gb = nb if seq == tt else 1
    kern = functools.partial(_mixer_kernel, nb=nb, gb=gb, tt=tt, chunk=chunk, emit_vc=emit_vc,
                             gather_x=gather_x)
    gather_scratch = [pltpu.VMEM((2, nb, tt // SUBLANES, SUBLANES, D_MODEL), F32),
                      pltpu.SemaphoreType.DMA((2,))] if gather_x else []
    return pl.pallas_call(
        kern,
        grid=(bsz // nb, seq // tt),
        in_specs=([pl.BlockSpec(memory_space=pl.ANY) if gather_x else x_spec,
                   state_spec(A_CONV - 1, A_WIDTH), h_spec, state_spec(B_CONV - 1, B_WIDTH)]
                  + [_of_layer(a, l) for a in layer_params] + [_whole()]
                  + [_of_layer(a, l) for a in mix_params]),
        out_specs=out_specs,
        out_shape=out_shape,
        scratch_shapes=[pltpu.VMEM((nb, A_CONV - 1, SUBLANES, A_WIDTH), F32),
                        pltpu.VMEM((nb, B_CONV - 1, SUBLANES, B_WIDTH), F32),
                        pltpu.VMEM((nb, SUBLANES, A_WIDTH), F32),
                        pltpu.VMEM((C_HEADS, tt, tt), BF16),
                        pltpu.VMEM((nb * tt, D_MODEL), BF16),
                        pltpu.VMEM((Z_SLOTS, gb * tt, IN_COLS), F32)] + gather_scratch,
        compiler_params=_params(),
        name="mixer",
    )(x, ca0, h0, cb0, *layer_params, p["group_mean"], *mix_params)


def _attn_call(x, kb, vb, p, l, nb, tt):
    bsz, seq, _ = x.shape
    x_spec, _ = _tile_specs(nb, tt)
    kv_spec = pl.BlockSpec((None, nb, N_MEM, D_MODEL), lambda b, t: (l, b, 0, 0))
    weights = [p["g_x"], p["w_q"], p["w_o"]]
    kern = functools.partial(_attn_kernel, nb=nb, tt=tt)
    return pl.pallas_call(
        kern,
        grid=(bsz // nb, seq // tt),
        in_specs=[x_spec, kv_spec, kv_spec] + [_of_layer(a, l) for a in weights],
        out_specs=x_spec,
        out_shape=jax.ShapeDtypeStruct(x.shape, F32),
        scratch_shapes=[pltpu.VMEM((nb * tt, D_MODEL), BF16)],
        compiler_params=_params(),
        name="attn",
    )(x, kb, vb, *weights)


def _ffn_call(x, cf0, p, l, nb, tt, final_norm, scatter_out):
    bsz, seq, _ = x.shape
    x_spec, state_spec = _tile_specs(nb, tt)
    weights = [p["g_ffn"], p["w_up"], p["conv_f_w"], p["w_down"]]
    kern = functools.partial(_ffn_kernel, nb=nb, tt=tt, final_norm=final_norm, scatter_out=scatter_out)
    scatter_scratch = [pltpu.VMEM((2, nb, tt // SUBLANES, SUBLANES, D_MODEL), F32),
                       pltpu.SemaphoreType.DMA((2,))] if scatter_out else []
    return pl.pallas_call(
        kern,
        grid=(bsz // nb, seq // tt),
        in_specs=[x_spec, state_spec(FFN_CONV - 1, D_FF)] + [_of_layer(a, l) for a in weights] + [_whole()],
        out_specs=[pl.BlockSpec(memory_space=pl.ANY) if scatter_out else x_spec,
                   state_spec(FFN_CONV - 1, D_FF)],
        out_shape=[jax.ShapeDtypeStruct(x.shape, F32), jax.ShapeDtypeStruct(cf0.shape, F32)],
        scratch_shapes=[pltpu.VMEM((nb, FFN_CONV - 1, SUBLANES, D_FF), F32)] + scatter_scratch,
        compiler_params=_params(),
        name="ffn",
    )(x, cf0, *weights, p["g_final"])


def _kv_call(mem, w_k, w_v):
    bsz = mem.shape[0]
    depth = w_k.shape[0]
    n_seq = min(bsz, KV_BATCH_TILE)
    w_spec = pl.BlockSpec((None, D_MODEL, D_MODEL), lambda l, b: (l, 0, 0))
    out5 = pl.BlockSpec(memory_space=pl.ANY)
    out4 = pl.BlockSpec((None, n_seq, N_MEM, D_MODEL), lambda l, b: (l, b, 0, 0))
    shape5 = jax.ShapeDtypeStruct((depth, bsz, N_MEM, X_HEADS, X_HD), F32)
    shape4 = jax.ShapeDtypeStruct((depth, bsz, N_MEM, D_MODEL), BF16)
    return pl.pallas_call(
        _kv_kernel,
        grid=(depth, bsz // n_seq),
        in_specs=[pl.BlockSpec((n_seq, N_MEM, D_MODEL), lambda l, b: (b, 0, 0)), w_spec, w_spec],
        out_specs=[out5, out5, out4, out4],
        out_shape=[shape5, shape5, shape4, shape4],
        scratch_shapes=[pltpu.VMEM((2, 2, n_seq, N_MEM, D_MODEL), F32), pltpu.SemaphoreType.DMA((2,))],
        compiler_params=_params(),
        name="memory_kv",
    )(mem, w_k, w_v)


def _kv_cast_call(cache_k, cache_v):
    depth, bsz = cache_k.shape[:2]
    n_seq = min(bsz, KV_BATCH_TILE)
    in_spec = pl.BlockSpec(memory_space=pl.ANY)
    out_spec = pl.BlockSpec((None, n_seq, N_MEM, D_MODEL), lambda l, b: (l, b, 0, 0))
    shape = jax.ShapeDtypeStruct((depth, bsz, N_MEM, D_MODEL), BF16)
    return pl.pallas_call(
        _kv_cast_kernel,
        grid=(depth, bsz // n_seq),
        in_specs=[in_spec, in_spec],
        out_specs=[out_spec, out_spec],
        out_shape=[shape, shape],
        scratch_shapes=[pltpu.VMEM((2, 2, n_seq, N_MEM, D_MODEL), F32), pltpu.SemaphoreType.DMA((2,))],
        compiler_params=_params(),
        name="cache_kv_cast",
    )(cache_k, cache_v)


def _block_diag(blocks):
    n, r, c = blocks.shape[-3:]
    eye = jnp.eye(n, dtype=blocks.dtype)
    return (blocks[..., :, :, None, :] * eye[:, None, :, None]).reshape(blocks.shape[:-3] + (n * r, n * c))


def _tile_times(tt):
    i = jnp.arange(tt)
    return (i % SUBLANES) * (tt // SUBLANES) + i // SUBLANES


def _to_segments(x, tt):
    bsz, seq, c = x.shape
    return x.reshape(bsz, seq // tt, SUBLANES, tt // SUBLANES, c).swapaxes(2, 3).reshape(bsz, seq, c)


def _from_segments(x, tt):
    bsz, seq, c = x.shape
    return x.reshape(bsz, seq // tt, tt // SUBLANES, SUBLANES, c).swapaxes(2, 3).reshape(bsz, seq, c)


def _on_sublanes(state):
    return jnp.broadcast_to(state[..., None, :], state.shape[:-1] + (SUBLANES, state.shape[-1]))


def _shared_params(g_mix, w_in, conv_a_w, conv_a_b, w_rg, b_rg, w_ig, b_ig, lam, conv_b_w, g_v, w_out,
                   g_x, w_q, w_o, g_ffn, w_up, conv_f_w, w_down, g_final):
    per_half = A_HEADS // 2
    gates = [jnp.concatenate([_block_diag(w_rg[:, j * per_half:(j + 1) * per_half]),
                              _block_diag(w_ig[:, j * per_half:(j + 1) * per_half])], axis=-1)
             for j in range(2)]
    group = jnp.arange(C_WIDTH) // C_HD

    def row(a):
        return a[:, None, :]

    def taps(a):
        return a[:, :, None, :]

    return dict(
        g_mix=row(g_mix), w_in=w_in.astype(BF16), conv_a_w=taps(conv_a_w), conv_a_b=row(conv_a_b),
        w_gates=jnp.stack(gates, axis=1).astype(BF16), b_rg=row(b_rg), b_ig=row(b_ig), lam=row(lam),
        conv_b_w=taps(conv_b_w), g_v=row(g_v),
        group_mean=((group[:, None] == group[None, :]).astype(F32) / C_HD).astype(BF16),
        w_out=w_out.astype(BF16), g_x=row(g_x), w_q=w_q.astype(BF16), w_o=w_o.astype(BF16),
        g_ffn=row(g_ffn), w_up=w_up.astype(BF16), conv_f_w=taps(conv_f_w), w_down=w_down.astype(BF16),
        g_final=g_final[None])


def _tile_params(tt, chunk, w_s, b_s):
    pos = _tile_times(tt) % chunk
    return dict(w_s=w_s[:, :, pos][:, :, :, pos],
                b_s_full=jnp.repeat(jnp.swapaxes(b_s[:, :, pos], 1, 2), C_HD, axis=2))


def _tiles(bsz, seq):
    if seq <= MLP_CHUNK:
        return bsz, seq
    return PROMPT_BATCH_TILE, PROMPT_TIME_TILE


def _trunk(x, kb, vb, states, p, emit_vc):
    bsz, seq, _ = x.shape
    nb, tt = _tiles(bsz, seq)
    chunk = min(seq, MLP_CHUNK)
    by_dma = seq > tt
    if not by_dma:
        x = _to_segments(x, tt)
    new_states = [[] for _ in range(4)]
    vcs = []
    for l in range(DEPTH):
        last = l == DEPTH - 1
        ca0, h0, cb0, cf0 = (s[l] for s in states)
        mix = _mixer_call(x, ca0, h0, cb0, p, l, nb, tt, chunk, emit_vc, by_dma and l == 0)
        x = _attn_call(mix[0], kb, vb, p, l, *((1, ATTN_TIME_TILE) if by_dma else (nb, tt)))
        x, cf = _ffn_call(x, cf0, p, l, nb, tt, last, by_dma and last)
        for acc, val in zip(new_states, (mix[1], mix[2], mix[3], cf)):
            acc.append(val)
        if emit_vc:
            vcs.append(mix[4])
    new_states = tuple(jnp.stack(s)[..., SUBLANES - 1, :] for s in new_states)
    vc = _from_segments(jnp.concatenate(vcs, axis=0), tt).reshape(DEPTH, bsz, seq, C_WIDTH) if emit_vc else None
    return (x if by_dma else _from_segments(x, tt)), new_states, vc


def kernel(x_prompt, x_sample, mem_prompt, cache_mem_k, cache_mem_v, state_conv_a, state_h_a, state_conv_b, state_conv_ffn, g_mix, w_in, conv_a_w, conv_a_b, w_rg, b_rg, w_ig, b_ig, lam, conv_b_w, g_v, w_s, b_s, w_out, g_x, w_q, w_k, w_v, w_o, g_ffn, w_up, conv_f_w, w_down, g_final):
    bp, seq_p, _ = x_prompt.shape
    bs, seq_s, _ = x_sample.shape
    shared = _shared_params(g_mix, w_in, conv_a_w, conv_a_b, w_rg, b_rg, w_ig, b_ig, lam, conv_b_w, g_v,
                            w_out, g_x, w_q, w_o, g_ffn, w_up, conv_f_w, w_down, g_final)

    def tile_params(bsz, seq):
        return dict(shared, **_tile_params(_tiles(bsz, seq)[1], min(seq, MLP_CHUNK), w_s, b_s))

    mem_k, mem_v, kb, vb = _kv_call(mem_prompt, w_k.astype(BF16), w_v.astype(BF16))
    zero_states = tuple(jnp.zeros((DEPTH, bp) + shape + (SUBLANES, width), F32)
                        for shape, width in (((A_CONV - 1,), A_WIDTH), ((), A_WIDTH),
                                             ((B_CONV - 1,), B_WIDTH), ((FFN_CONV - 1,), D_FF)))
    y_p, states_p, _ = _trunk(x_prompt, kb, vb, zero_states, tile_params(bp, seq_p), False)
    carried = tuple(_on_sublanes(s) for s in (state_conv_a, state_h_a, state_conv_b, state_conv_ffn))
    cache_kb, cache_vb = _kv_cast_call(cache_mem_k, cache_mem_v)
    y_s, states_s, vc_s = _trunk(x_sample, cache_kb, cache_vb, carried, tile_params(bs, seq_s), True)
    return (y_p, y_s) + states_p + (mem_k, mem_v) + states_s + (vc_s,)
```

```python
import functools
import math

import jax
import jax.numpy as jnp
from jax import lax
from jax.experimental import pallas as pl
from jax.experimental.pallas import tpu as pltpu

D_MODEL = 1024
DEPTH = 2
A_WIDTH = 512
A_HEADS = 8
A_HD = 64
A_CONV = 4
RG_C = 8.0
B_WIDTH = 256
B_CONV = 3
C_WIDTH = 256
C_HEADS = 4
C_HD = 64
MLP_CHUNK = 128
N_MEM = 256
X_HEADS = 4
X_HD = 256
D_FF = 2816
FFN_CONV = 3
EPS = 1e-6
IN_COLS = 2 * A_WIDTH + 3 * B_WIDTH + 2 * C_WIDTH

SUBLANES = 8
LANES = 128
MXU_DIM = 256
V7X_VMEM_BYTES = 64 * 1024 * 1024
VMEM_LIMIT_BYTES = V7X_VMEM_BYTES * 7 // 8
PROMPT_TIME_TILE = 256
PROMPT_BATCH_TILE = 4
ATTN_TIME_TILE = 1024
KV_BATCH_TILE = 4
Z_SLOTS = 3
PROJ_PARTS = IN_COLS // MXU_DIM

LOG2_E = math.log2(math.e)
GELU_C0 = math.sqrt(2.0 / math.pi)
GELU_C1 = 0.044715

F32 = jnp.float32
BF16 = jnp.bfloat16


def _dot(a, b):
    return jnp.dot(a, b, preferred_element_type=F32)


def _rms(x, g):
    ms = jnp.mean(x * x, axis=-1, keepdims=True)
    return x * lax.rsqrt(ms + EPS) * g


def _gelu(x):
    scale = -2.0 * LOG2_E * GELU_C0
    return x / (1.0 + jnp.exp2((x * x * (scale * GELU_C1) + scale) * x))


def _sublane_iota(shape):
    return lax.broadcasted_iota(jnp.int32, shape, len(shape) - 2)


def _causal_taps(cur, tail_ref, w):
    n_prev = tail_ref.shape[1]
    seg = cur.shape[1]
    last = cur[:, seg - n_prev:]
    wrapped = pltpu.roll(jnp.where(_sublane_iota(last.shape) == SUBLANES - 1, tail_ref[...], last), 1, 2)
    tail_ref[...] = last
    xp = jnp.concatenate([wrapped, cur], axis=1)
    acc = xp[:, 0:seg] * w[0]
    for j in range(1, n_prev + 1):
        acc = acc + xp[:, j:j + seg] * w[j]
    return acc


def _segment_scan(a, b, h_ref):
    seg = a.shape[1]
    local, decay = b[:, 0], a[:, 0]
    locals_, decays = [local], [decay]
    for s in range(1, seg):
        local = a[:, s] * local + b[:, s]
        decay = a[:, s] * decay
        locals_.append(local)
        decays.append(decay)
    h_in = h_ref[:, SUBLANES - 1:SUBLANES, :]
    row = _sublane_iota(local.shape)
    seg_a, seg_b = decay, local
    d = 1
    while d < SUBLANES:
        a_prev = pltpu.roll(seg_a, d, 1)
        b_prev = pltpu.roll(seg_b, d, 1)
        keep = row >= d
        seg_b = jnp.where(keep, seg_a * b_prev + seg_b, seg_b)
        seg_a = jnp.where(keep, seg_a * a_prev, seg_a)
        d *= 2
    h_end = seg_a * h_in + seg_b
    h_ref[...] = h_end
    h_start = jnp.where(row == 0, h_in, pltpu.roll(h_end, 1, 1))
    return jnp.stack([locals_[s] + decays[s] * h_start for s in range(seg)], axis=1)


def _grid_step():
    return (pl.program_id(0) * pl.num_programs(1) + pl.program_id(1),
            pl.num_programs(0) * pl.num_programs(1))


def _tile_copies(hbm_ref, buf_ref, sem, b, t, nb, tt, to_vmem):
    seg = tt // SUBLANES
    copies = []
    for r in range(SUBLANES):
        hbm = hbm_ref.at[pl.ds(b * nb, nb), pl.ds(t * tt + r * seg, seg), :]
        vmem = buf_ref.at[:, :, r, :]
        copies.append(pltpu.make_async_copy(hbm, vmem, sem) if to_vmem
                      else pltpu.make_async_copy(vmem, hbm, sem))
    return copies


def _head_copies(hbm_refs, buf_ref, sem, l, b, to_vmem):
    n_seq = buf_ref.shape[1]
    copies = []
    for j, hbm_ref in enumerate(hbm_refs):
        for hd in range(X_HEADS):
            hbm = hbm_ref.at[l, pl.ds(b * n_seq, n_seq), :, hd, :]
            vmem = buf_ref.at[j, :, :, pl.ds(hd * X_HD, X_HD)]
            copies.append(pltpu.make_async_copy(hbm, vmem, sem) if to_vmem
                          else pltpu.make_async_copy(vmem, hbm, sem))
    return copies


def _next_grid_index():
    wraps = pl.program_id(1) + 1 == pl.num_programs(1)
    return pl.program_id(0) + wraps.astype(jnp.int32), jnp.where(wraps, 0, pl.program_id(1) + 1)


def _prefetched(fetch):
    step, n_steps = _grid_step()
    slot = lax.rem(step, 2)

    @pl.when(step == 0)
    def _():
        for c in fetch(pl.program_id(0), pl.program_id(1), slot):
            c.start()

    @pl.when(step + 1 < n_steps)
    def _():
        for c in fetch(*_next_grid_index(), 1 - slot):
            c.start()

    for c in fetch(pl.program_id(0), pl.program_id(1), slot):
        c.wait()
    return slot


def _flushed(flush, fill):
    step, n_steps = _grid_step()
    slot = lax.rem(step, 2)

    @pl.when(step >= 2)
    def _():
        for c in flush(slot):
            c.wait()

    fill(slot)
    for c in flush(slot):
        c.start()

    @pl.when(step == n_steps - 1)
    def _():
        @pl.when(n_steps > 1)
        def _():
            for c in flush(1 - slot):
                c.wait()
        for c in flush(slot):
            c.wait()


def _mixer_kernel(x_ref, ca0_ref, h0_ref, cb0_ref, gmix_ref, win_ref, caw_ref, cab_ref, wg_ref,
                  brg_ref, big_ref, lam_ref, cbw_ref, gv_ref, gsum_ref, ws_ref, bsf_ref, wout_ref,
                  xo_ref, cao_ref, ho_ref, cbo_ref, *rest, nb, gb, tt, chunk, emit_vc, gather_x):
    if gather_x:
        *rest, x_buf, x_sem = rest
    if emit_vc:
        vco_ref, ca_tail, cb_tail, h_state, wmix, ycat, zbuf = rest
    else:
        ca_tail, cb_tail, h_state, wmix, ycat, zbuf = rest
    rows = gb * tt
    seg = tt // SUBLANES
    groups = nb // gb

    if gather_x:
        slot = _prefetched(lambda b, t, into: _tile_copies(x_ref, x_buf.at[into], x_sem.at[into],
                                                           b, t, nb, tt, to_vmem=True))

    def read_x(g):
        if gather_x:
            return x_buf[slot, g * gb:(g + 1) * gb].reshape(rows, D_MODEL)
        return x_ref[g * gb:(g + 1) * gb].reshape(rows, D_MODEL)

    @pl.when(pl.program_id(1) == 0)
    def _():
        ca_tail[...] = ca0_ref[...]
        cb_tail[...] = cb0_ref[...]
        h_state[...] = h0_ref[...]
        ti = lax.broadcasted_iota(jnp.int32, (tt, tt), 0)
        tj = lax.broadcasted_iota(jnp.int32, (tt, tt), 1)
        time_i = (ti % SUBLANES) * seg + ti // SUBLANES
        time_j = (tj % SUBLANES) * seg + tj // SUBLANES
        allowed = (time_i >= time_j) & (time_i // chunk == time_j // chunk)
        for hd in range(C_HEADS):
            wmix[hd] = jnp.where(allowed, ws_ref[hd], 0.0).astype(BF16)

    def project_in(g):
        xn = _rms(read_x(g), gmix_ref[...]).astype(BF16)
        z = zbuf.at[g % Z_SLOTS]
        width = IN_COLS // PROJ_PARTS

        def part(c):
            z[:, c * width:(c + 1) * width] = _dot(xn, win_ref[:, c * width:(c + 1) * width])

        return z, [functools.partial(part, c) for c in range(PROJ_PARTS)]

    col = {}
    o = 0
    for name, width in (("xa", A_WIDTH), ("ga", A_WIDTH), ("xb", B_WIDTH), ("gb", B_WIDTH),
                        ("gc", B_WIDTH), ("uc", C_WIDTH), ("vc", C_WIDTH)):
        col[name] = slice(o, o + width)
        o += width
    half = A_WIDTH // 2

    def mix_matmuls(g, z):
        seqs = slice(g * gb, (g + 1) * gb)
        out = {}

        def gates_and_norm():
            xc = _causal_taps(z[:, col["xa"]].reshape(gb, seg, SUBLANES, A_WIDTH), ca_tail.at[seqs],
                              caw_ref[...])
            xc2 = (xc + cab_ref[...]).reshape(rows, A_WIDTH)
            xcb = xc2.astype(BF16)
            out["xc2"] = xc2
            out["gz0"] = _dot(xcb[:, :half], wg_ref[0])
            out["gz1"] = _dot(xcb[:, half:], wg_ref[1])
            v = _gelu(z[:, col["vc"]])
            sq = v * v
            sq_hi = sq.astype(BF16)
            sq_lo = (sq - sq_hi.astype(F32)).astype(BF16)
            out["v"] = v
            out["ms"] = _dot(sq_hi, gsum_ref[...]) + _dot(sq_lo, gsum_ref[...])

        def spatial_mix():
            vn = out.pop("v") * lax.rsqrt(out.pop("ms") + EPS) * gv_ref[...]
            if emit_vc:
                vco_ref[seqs] = vn.reshape(gb, tt, C_WIDTH)
            vnb = vn.astype(BF16)
            first_head = lax.broadcasted_iota(jnp.int32, (tt, LANES), 1) < C_HD
            mixed = []
            for b in range(gb):
                pieces = []
                for pair in range(C_WIDTH // LANES):
                    vp = vnb[b * tt:(b + 1) * tt, pair * LANES:(pair + 1) * LANES]
                    pieces.append(jnp.where(first_head, _dot(wmix[2 * pair], vp), _dot(wmix[2 * pair + 1], vp)))
                mixed.append(jnp.concatenate(pieces, axis=1) + bsf_ref[...])
            out["mixed"] = jnp.concatenate(mixed, axis=0)

        return out, [gates_and_norm, spatial_mix]

    def mix_elementwise(g, z, xc2, gz0, gz1, mixed):
        seqs = slice(g * gb, (g + 1) * gb)
        y = ycat.at[g * rows:(g + 1) * rows]
        r = jax.nn.sigmoid(jnp.concatenate([gz0[:, :half], gz1[:, :half]], axis=1) + brg_ref[...])
        gi = jax.nn.sigmoid(jnp.concatenate([gz0[:, half:], gz1[:, half:]], axis=1) + big_ref[...])
        neg_lam = -lam_ref[...]
        softplus = jnp.maximum(neg_lam, 0.0) + jnp.log1p(jnp.exp(-jnp.abs(neg_lam)))
        decay_rate = RG_C * softplus
        a = jnp.exp2((-LOG2_E) * decay_rate * r)
        one_minus_a2 = (1.0 + a * a) * jnp.tanh(decay_rate * r)
        root = jnp.where(one_minus_a2 > 0.0, one_minus_a2 * lax.rsqrt(one_minus_a2), 0.0)
        bterm = root * (gi * xc2)
        h = _segment_scan(a.reshape(gb, seg, SUBLANES, A_WIDTH),
                          bterm.reshape(gb, seg, SUBLANES, A_WIDTH), h_state.at[seqs])
        y[:, 0:A_WIDTH] = (_gelu(z[:, col["ga"]]) * h.reshape(rows, A_WIDTH)).astype(BF16)
        zb = _causal_taps((z[:, col["gc"]] * z[:, col["xb"]]).reshape(gb, seg, SUBLANES, B_WIDTH),
                          cb_tail.at[seqs], cbw_ref[...])
        y[:, A_WIDTH:A_WIDTH + B_WIDTH] = (z[:, col["gb"]] * zb.reshape(rows, B_WIDTH)).astype(BF16)
        y[:, A_WIDTH + B_WIDTH:] = (_gelu(z[:, col["uc"]]) * mixed).astype(BF16)

    def project_out(g):
        out = read_x(g) + _dot(ycat[g * rows:(g + 1) * rows], wout_ref[...])
        xo_ref[g * gb:(g + 1) * gb] = out.reshape(gb, tt, D_MODEL)

    def interleave(big_steps, small_steps):
        stride = max(1, len(big_steps) // max(1, len(small_steps)))
        pending = list(small_steps)
        for i, step in enumerate(big_steps):
            step()
            if pending and (i + 1) % stride == 0:
                pending.pop(0)()
        for step in pending:
            step()

    zs, small = {}, {}
    zs[0], parts = project_in(0)
    interleave(parts, [])
    small[0], steps = mix_matmuls(0, zs[0])
    if groups > 1:
        zs[1], parts = project_in(1)
        interleave(parts, steps)
    else:
        interleave([], steps)
    for g in range(groups):
        parts, steps = [], []
        if g + 2 < groups:
            zs[g + 2], parts = project_in(g + 2)
        if g + 1 < groups:
            small[g + 1], steps = mix_matmuls(g + 1, zs[g + 1])
        interleave(parts, steps)
        res = small.pop(g)
        mix_elementwise(g, zs.pop(g), res["xc2"], res["gz0"], res["gz1"], res["mixed"])
        project_out(g)
    cao_ref[...] = ca_tail[...]
    ho_ref[...] = h_state[...]
    cbo_ref[...] = cb_tail[...]


def _attn_kernel(x_ref, kb, vb, gx_ref, wq_ref, wo_ref, xo_ref, ocat, *, nb, tt):
    rows = nb * tt
    x = x_ref[...].reshape(rows, D_MODEL)
    xn = _rms(x, gx_ref[...]).astype(BF16)
    q = (_dot(xn, wq_ref[...]) * (X_HD ** -0.5)).astype(BF16)
    for seq in range(nb):
        r0 = seq * tt
        for hd in range(X_HEADS):
            cols = slice(hd * X_HD, (hd + 1) * X_HD)
            s = lax.dot_general(q[r0:r0 + tt, cols], kb[seq, :, cols], (((1,), (1,)), ((), ())),
                                preferred_element_type=F32)
            e = jnp.exp(s - jnp.max(s, axis=-1, keepdims=True))
            denom = jnp.sum(e, axis=-1, keepdims=True)
            oh = _dot(e.astype(BF16), vb[seq, :, cols]) / denom
            ocat[r0:r0 + tt, cols] = oh.astype(BF16)
    out = x + _dot(ocat[...], wo_ref[...])
    xo_ref[...] = out.reshape(nb, tt, D_MODEL)


def _ffn_kernel(x_ref, cf0_ref, gffn_ref, wup_ref, cfw_ref, wdown_ref, gfin_ref, xo_ref, cfo_ref,
                cf_tail, act_buf, *rest, nb, tt, final_norm, scatter_out):
    rows = nb * tt
    seg = tt // SUBLANES

    @pl.when(pl.program_id(1) == 0)
    def _():
        cf_tail[...] = cf0_ref[...]

    x = x_ref[...].reshape(rows, D_MODEL)
    xn = _rms(x, gffn_ref[...]).astype(BF16)
    for c0 in range(0, D_FF, MXU_DIM):
        cols = slice(c0, c0 + MXU_DIM)
        gate = _dot(xn, wup_ref[:, cols])
        up = _dot(xn, wup_ref[:, D_FF + c0:D_FF + c0 + MXU_DIM])
        gconv = _causal_taps(gate.reshape(nb, seg, SUBLANES, MXU_DIM), cf_tail.at[:, :, :, cols],
                             cfw_ref[:, :, cols])
        act_buf[:, cols] = (jax.nn.silu(gconv.reshape(rows, MXU_DIM)) * up).astype(BF16)
    cfo_ref[...] = cf_tail[...]
    out = x + _dot(act_buf[...], wdown_ref[...])
    if final_norm:
        out = _rms(out, gfin_ref[...])
    if not scatter_out:
        xo_ref[...] = out.reshape(nb, tt, D_MODEL)
        return

    out_buf, out_sem = rest

    def flush(from_slot):
        return _tile_copies(xo_ref, out_buf.at[from_slot], out_sem.at[from_slot],
                            pl.program_id(0), pl.program_id(1), nb, tt, to_vmem=False)

    def fill(slot):
        out_buf[slot] = out.reshape(nb, seg, SUBLANES, D_MODEL)

    _flushed(flush, fill)


def _kv_kernel(mem_ref, wk_ref, wv_ref, ko_hbm, vo_hbm, kb_ref, vb_ref, kv_buf, sem):
    n_seq = mem_ref.shape[0]
    m = mem_ref[...].reshape(n_seq * N_MEM, D_MODEL).astype(BF16)
    k = _dot(m, wk_ref[...]).reshape(n_seq, N_MEM, D_MODEL)
    v = _dot(m, wv_ref[...]).reshape(n_seq, N_MEM, D_MODEL)
    kb_ref[...] = k.astype(BF16)
    vb_ref[...] = v.astype(BF16)

    def flush(from_slot):
        return _head_copies((ko_hbm, vo_hbm), kv_buf.at[from_slot], sem.at[from_slot],
                            pl.program_id(0), pl.program_id(1), to_vmem=False)

    def fill(slot):
        kv_buf[slot, 0] = k
        kv_buf[slot, 1] = v

    _flushed(flush, fill)


def _kv_cast_kernel(k_hbm, v_hbm, kb_ref, vb_ref, kv_buf, sem):
    slot = _prefetched(lambda l, b, into: _head_copies((k_hbm, v_hbm), kv_buf.at[into], sem.at[into],
                                                       l, b, to_vmem=True))
    kb_ref[...] = kv_buf[slot, 0].astype(BF16)
    vb_ref[...] = kv_buf[slot, 1].astype(BF16)


def _whole():
    return pl.BlockSpec(memory_space=pltpu.MemorySpace.VMEM)


def _of_layer(arr, l):
    zeros = (0,) * (arr.ndim - 1)
    return pl.BlockSpec((None,) + arr.shape[1:], lambda b, t: (l,) + zeros, pipeline_mode=pl.Buffered(1))


def _params():
    return pltpu.CompilerParams(dimension_semantics=("arbitrary", "arbitrary"),
                                vmem_limit_bytes=VMEM_LIMIT_BYTES)


def _tile_specs(nb, tt):
    x_spec = pl.BlockSpec((nb, tt, D_MODEL), lambda b, t: (b, t, 0))

    def state_spec(n, width):
        return pl.BlockSpec((nb, n, SUBLANES, width), lambda b, t: (b, 0, 0, 0))

    return x_spec, state_spec


def _mixer_call(x, ca0, h0, cb0, p, l, nb, tt, chunk, emit_vc, gather_x):
    bsz, seq, _ = x.shape
    layer_params = [p[name] for name in ("g_mix", "w_in", "conv_a_w", "conv_a_b", "w_gates", "b_rg", "b_ig",
                                         "lam", "conv_b_w", "g_v")]
    mix_params = [p[name] for name in ("w_s", "b_s_full", "w_out")]
    x_spec, state_spec = _tile_specs(nb, tt)
    h_spec = pl.BlockSpec((nb, SUBLANES, A_WIDTH), lambda b, t: (b, 0, 0))
    out_shape = [jax.ShapeDtypeStruct(x.shape, F32),
                 jax.ShapeDtypeStruct(ca0.shape, F32),
                 jax.ShapeDtypeStruct(h0.shape, F32),
                 jax.ShapeDtypeStruct(cb0.shape, F32)]
    out_specs = [x_spec, state_spec(A_CONV - 1, A_WIDTH), h_spec, state_spec(B_CONV - 1, B_WIDTH)]
    if emit_vc:
        out_shape.append(jax.ShapeDtypeStruct((bsz, seq, C_WIDTH), F32))
        out_specs.append(pl.BlockSpec((nb, tt, C_WIDTH), lambda b, t: (b, t, 0)))
    gb = nb if seq == tt else 1
    kern = functools.partial(_mixer_kernel, nb=nb, gb=gb, tt=tt, chunk=chunk, emit_vc=emit_vc,
                             gather_x=gather_x)
    gather_scratch = [pltpu.VMEM((2, nb, tt // SUBLANES, SUBLANES, D_MODEL), F32),
                      pltpu.SemaphoreType.DMA((2,))] if gather_x else []
    return pl.pallas_call(
        kern,
        grid=(bsz // nb, seq // tt),
        in_specs=([pl.BlockSpec(memory_space=pl.ANY) if gather_x else x_spec,
                   state_spec(A_CONV - 1, A_WIDTH), h_spec, state_spec(B_CONV - 1, B_WIDTH)]
                  + [_of_layer(a, l) for a in layer_params] + [_whole()]
                  + [_of_layer(a, l) for a in mix_params]),
        out_specs=out_specs,
        out_shape=out_shape,
        scratch_shapes=[pltpu.VMEM((nb, A_CONV - 1, SUBLANES, A_WIDTH), F32),
                        pltpu.VMEM((nb, B_CONV - 1, SUBLANES, B_WIDTH), F32),
                        pltpu.VMEM((nb, SUBLANES, A_WIDTH), F32),
                        pltpu.VMEM((C_HEADS, tt, tt), BF16),
                        pltpu.VMEM((nb * tt, D_MODEL), BF16),
                        pltpu.VMEM((Z_SLOTS, gb * tt, IN_COLS), F32)] + gather_scratch,
        compiler_params=_params(),
        name="mixer",
    )(x, ca0, h0, cb0, *layer_params, p["group_mean"], *mix_params)


def _attn_call(x, kb, vb, p, l, nb, tt):
    bsz, seq, _ = x.shape
    x_spec, _ = _tile_specs(nb, tt)
    kv_spec = pl.BlockSpec((None, nb, N_MEM, D_MODEL), lambda b, t: (l, b, 0, 0))
    weights = [p["g_x"], p["w_q"], p["w_o"]]
    kern = functools.partial(_attn_kernel, nb=nb, tt=tt)
    return pl.pallas_call(
        kern,
        grid=(bsz // nb, seq // tt),
        in_specs=[x_spec, kv_spec, kv_spec] + [_of_layer(a, l) for a in weights],
        out_specs=x_spec,
        out_shape=jax.ShapeDtypeStruct(x.shape, F32),
        scratch_shapes=[pltpu.VMEM((nb * tt, D_MODEL), BF16)],
        compiler_params=_params(),
        name="attn",
    )(x, kb, vb, *weights)


def _ffn_call(x, cf0, p, l, nb, tt, final_norm, scatter_out):
    bsz, seq, _ = x.shape
    x_spec, state_spec = _tile_specs(nb, tt)
    weights = [p["g_ffn"], p["w_up"], p["conv_f_w"], p["w_down"]]
    kern = functools.partial(_ffn_kernel, nb=nb, tt=tt, final_norm=final_norm, scatter_out=scatter_out)
    scatter_scratch = [pltpu.VMEM((2, nb, tt // SUBLANES, SUBLANES, D_MODEL), F32),
                       pltpu.SemaphoreType.DMA((2,))] if scatter_out else []
    return pl.pallas_call(
        kern,
        grid=(bsz // nb, seq // tt),
        in_specs=[x_spec, state_spec(FFN_CONV - 1, D_FF)] + [_of_layer(a, l) for a in weights] + [_whole()],
        out_specs=[pl.BlockSpec(memory_space=pl.ANY) if scatter_out else x_spec,
                   state_spec(FFN_CONV - 1, D_FF)],
        out_shape=[jax.ShapeDtypeStruct(x.shape, F32), jax.ShapeDtypeStruct(cf0.shape, F32)],
        scratch_shapes=[pltpu.VMEM((nb, FFN_CONV - 1, SUBLANES, D_FF), F32),
                        pltpu.VMEM((nb * tt, D_FF), BF16)] + scatter_scratch,
        compiler_params=_params(),
        name="ffn",
    )(x, cf0, *weights, p["g_final"])


def _kv_call(mem, w_k, w_v):
    bsz = mem.shape[0]
    depth = w_k.shape[0]
    n_seq = min(bsz, KV_BATCH_TILE)
    assert bsz % n_seq == 0, bsz
    w_spec = pl.BlockSpec((None, D_MODEL, D_MODEL), lambda l, b: (l, 0, 0))
    out5 = pl.BlockSpec(memory_space=pl.ANY)
    out4 = pl.BlockSpec((None, n_seq, N_MEM, D_MODEL), lambda l, b: (l, b, 0, 0))
    shape5 = jax.ShapeDtypeStruct((depth, bsz, N_MEM, X_HEADS, X_HD), F32)
    shape4 = jax.ShapeDtypeStruct((depth, bsz, N_MEM, D_MODEL), BF16)
    return pl.pallas_call(
        _kv_kernel,
        grid=(depth, bsz // n_seq),
        in_specs=[pl.BlockSpec((n_seq, N_MEM, D_MODEL), lambda l, b: (b, 0, 0)), w_spec, w_spec],
        out_specs=[out5, out5, out4, out4],
        out_shape=[shape5, shape5, shape4, shape4],
        scratch_shapes=[pltpu.VMEM((2, 2, n_seq, N_MEM, D_MODEL), F32), pltpu.SemaphoreType.DMA((2,))],
        compiler_params=_params(),
        name="memory_kv",
    )(mem, w_k, w_v)


def _kv_cast_call(cache_k, cache_v):
    depth, bsz = cache_k.shape[:2]
    n_seq = min(bsz, KV_BATCH_TILE)
    assert bsz % n_seq == 0, bsz
    in_spec = pl.BlockSpec(memory_space=pl.ANY)
    out_spec = pl.BlockSpec((None, n_seq, N_MEM, D_MODEL), lambda l, b: (l, b, 0, 0))
    shape = jax.ShapeDtypeStruct((depth, bsz, N_MEM, D_MODEL), BF16)
    return pl.pallas_call(
        _kv_cast_kernel,
        grid=(depth, bsz // n_seq),
        in_specs=[in_spec, in_spec],
        out_specs=[out_spec, out_spec],
        out_shape=[shape, shape],
        scratch_shapes=[pltpu.VMEM((2, 2, n_seq, N_MEM, D_MODEL), F32), pltpu.SemaphoreType.DMA((2,))],
        compiler_params=_params(),
        name="cache_kv_cast",
    )(cache_k, cache_v)


def _block_diag(blocks):
    n, r, c = blocks.shape[-3:]
    eye = jnp.eye(n, dtype=blocks.dtype)
    return (blocks[..., :, :, None, :] * eye[:, None, :, None]).reshape(blocks.shape[:-3] + (n * r, n * c))


def _tile_times(tt):
    i = jnp.arange(tt)
    return (i % SUBLANES) * (tt // SUBLANES) + i // SUBLANES


def _to_segments(x, tt):
    bsz, seq, c = x.shape
    return x.reshape(bsz, seq // tt, SUBLANES, tt // SUBLANES, c).swapaxes(2, 3).reshape(bsz, seq, c)


def _from_segments(x, tt):
    bsz, seq, c = x.shape
    return x.reshape(bsz, seq // tt, tt // SUBLANES, SUBLANES, c).swapaxes(2, 3).reshape(bsz, seq, c)


def _on_sublanes(state):
    return jnp.broadcast_to(state[..., None, :], state.shape[:-1] + (SUBLANES, state.shape[-1]))


def _shared_params(g_mix, w_in, conv_a_w, conv_a_b, w_rg, b_rg, w_ig, b_ig, lam, conv_b_w, g_v, w_out,
                   g_x, w_q, w_o, g_ffn, w_up, conv_f_w, w_down, g_final):
    per_half = A_HEADS // 2
    gates = [jnp.concatenate([_block_diag(w_rg[:, j * per_half:(j + 1) * per_half]),
                              _block_diag(w_ig[:, j * per_half:(j + 1) * per_half])], axis=-1)
             for j in range(2)]
    group = jnp.arange(C_WIDTH) // C_HD

    def row(a):
        return a[:, None, :]

    def taps(a):
        return a[:, :, None, :]

    return dict(
        g_mix=row(g_mix), w_in=w_in.astype(BF16), conv_a_w=taps(conv_a_w), conv_a_b=row(conv_a_b),
        w_gates=jnp.stack(gates, axis=1).astype(BF16), b_rg=row(b_rg), b_ig=row(b_ig), lam=row(lam),
        conv_b_w=taps(conv_b_w), g_v=row(g_v),
        group_mean=((group[:, None] == group[None, :]).astype(F32) / C_HD).astype(BF16),
        w_out=w_out.astype(BF16), g_x=row(g_x), w_q=w_q.astype(BF16), w_o=w_o.astype(BF16),
        g_ffn=row(g_ffn), w_up=w_up.astype(BF16), conv_f_w=taps(conv_f_w), w_down=w_down.astype(BF16),
        g_final=g_final[None])


def _tile_params(tt, chunk, w_s, b_s):
    pos = _tile_times(tt) % chunk
    return dict(w_s=w_s[:, :, pos][:, :, :, pos],
                b_s_full=jnp.repeat(jnp.swapaxes(b_s[:, :, pos], 1, 2), C_HD, axis=2))


def _tiles(bsz, seq):
    if seq <= MLP_CHUNK:
        return bsz, seq
    return PROMPT_BATCH_TILE, PROMPT_TIME_TILE


def _trunk(x, kb, vb, states, p, emit_vc):
    bsz, seq, d_model = x.shape
    nb, tt = _tiles(bsz, seq)
    chunk = min(seq, MLP_CHUNK)
    by_dma = seq > tt
    assert d_model == D_MODEL and bsz % nb == 0 and seq % tt == 0 and tt % chunk == 0, x.shape
    assert tt % SUBLANES == 0 and tt // SUBLANES >= A_CONV - 1, tt
    assert not by_dma or seq % ATTN_TIME_TILE == 0, seq
    assert kb.shape == vb.shape == (DEPTH, bsz, N_MEM, D_MODEL), kb.shape
    if not by_dma:
        x = _to_segments(x, tt)
    new_states = [[] for _ in range(4)]
    vcs = []
    for l in range(DEPTH):
        last = l == DEPTH - 1
        ca0, h0, cb0, cf0 = (s[l] for s in states)
        mix = _mixer_call(x, ca0, h0, cb0, p, l, nb, tt, chunk, emit_vc, by_dma and l == 0)
        x = _attn_call(mix[0], kb, vb, p, l, *((1, ATTN_TIME_TILE) if by_dma else (nb, tt)))
        x, cf = _ffn_call(x, cf0, p, l, nb, tt, last, by_dma and last)
        for acc, val in zip(new_states, (mix[1], mix[2], mix[3], cf)):
            acc.append(val)
        if emit_vc:
            vcs.append(mix[4])
    new_states = tuple(jnp.stack(s)[..., SUBLANES - 1, :] for s in new_states)
    vc = _from_segments(jnp.concatenate(vcs, axis=0), tt).reshape(DEPTH, bsz, seq, C_WIDTH) if emit_vc else None
    return (x if by_dma else _from_segments(x, tt)), new_states, vc


def kernel(x_prompt, x_sample, mem_prompt, cache_mem_k, cache_mem_v, state_conv_a, state_h_a, state_conv_b, state_conv_ffn, g_mix, w_in, conv_a_w, conv_a_b, w_rg, b_rg, w_ig, b_ig, lam, conv_b_w, g_v, w_s, b_s, w_out, g_x, w_q, w_k, w_v, w_o, g_ffn, w_up, conv_f_w, w_down, g_final):
    bp, seq_p, _ = x_prompt.shape
    bs, seq_s, _ = x_sample.shape
    assert mem_prompt.shape == (bp, N_MEM, D_MODEL), mem_prompt.shape
    assert cache_mem_k.shape == cache_mem_v.shape == (DEPTH, bs, N_MEM, X_HEADS, X_HD), cache_mem_k.shape
    assert w_in.shape == (DEPTH, D_MODEL, IN_COLS) and w_up.shape == (DEPTH, D_MODEL, 2 * D_FF), w_in.shape
    shared = _shared_params(g_mix, w_in, conv_a_w, conv_a_b, w_rg, b_rg, w_ig, b_ig, lam, conv_b_w, g_v,
                            w_out, g_x, w_q, w_o, g_ffn, w_up, conv_f_w, w_down, g_final)

    def tile_params(bsz, seq):
        return dict(shared, **_tile_params(_tiles(bsz, seq)[1], min(seq, MLP_CHUNK), w_s, b_s))

    mem_k, mem_v, kb, vb = _kv_call(mem_prompt, w_k.astype(BF16), w_v.astype(BF16))
    zero_states = tuple(jnp.zeros((DEPTH, bp) + shape + (SUBLANES, width), F32)
                        for shape, width in (((A_CONV - 1,), A_WIDTH), ((), A_WIDTH),
                                             ((B_CONV - 1,), B_WIDTH), ((FFN_CONV - 1,), D_FF)))
    y_p, states_p, _ = _trunk(x_prompt, kb, vb, zero_states, tile_params(bp, seq_p), False)
    carried = tuple(_on_sublanes(s) for s in (state_conv_a, state_h_a, state_conv_b, state_conv_ffn))
    cache_kb, cache_vb = _kv_cast_call(cache_mem_k, cache_mem_v)
    y_s, states_s, vc_s = _trunk(x_sample, cache_kb, cache_vb, carried, tile_params(bs, seq_s), True)
    return (y_p, y_s) + states_p + (mem_k, mem_v) + states_s + (vc_s,)
```

```python
import functools
import math

import jax
import jax.numpy as jnp
from jax import lax
from jax.experimental import pallas as pl
from jax.experimental.pallas import tpu as pltpu

D_MODEL = 1024
DEPTH = 2
A_WIDTH = 512
A_HEADS = 8
A_HD = 64
A_CONV = 4
RG_C = 8.0
B_WIDTH = 256
B_CONV = 3
C_WIDTH = 256
C_HEADS = 4
C_HD = 64
MLP_CHUNK = 128
N_MEM = 256
X_HEADS = 4
X_HD = 256
D_FF = 2816
FFN_CONV = 3
EPS = 1e-6
IN_COLS = 2 * A_WIDTH + 3 * B_WIDTH + 2 * C_WIDTH

SUBLANES = 8
LANES = 128
MXU_DIM = 256
V7X_VMEM_BYTES = 64 * 1024 * 1024
VMEM_LIMIT_BYTES = V7X_VMEM_BYTES * 7 // 8
PROMPT_TIME_TILE = 256
PROMPT_BATCH_TILE = 4
ATTN_TIME_TILE = 1024
KV_BATCH_TILE = 4
Z_SLOTS = 3
PROJ_PARTS = IN_COLS // MXU_DIM

LOG2_E = math.log2(math.e)
GELU_C0 = math.sqrt(2.0 / math.pi)
GELU_C1 = 0.044715

F32 = jnp.float32
BF16 = jnp.bfloat16


def _dot(a, b):
    return jnp.dot(a, b, preferred_element_type=F32)


def _rms(x, g):
    ms = jnp.mean(x * x, axis=-1, keepdims=True)
    return x * lax.rsqrt(ms + EPS) * g


def _gelu(x):
    scale = -2.0 * LOG2_E * GELU_C0
    return x / (1.0 + jnp.exp2((x * x * (scale * GELU_C1) + scale) * x))


def _sublane_iota(shape):
    return lax.broadcasted_iota(jnp.int32, shape, len(shape) - 2)


def _causal_taps(cur, tail_ref, w):
    n_prev = tail_ref.shape[1]
    seg = cur.shape[1]
    last = cur[:, seg - n_prev:]
    wrapped = pltpu.roll(jnp.where(_sublane_iota(last.shape) == SUBLANES - 1, tail_ref[...], last), 1, 2)
    tail_ref[...] = last
    xp = jnp.concatenate([wrapped, cur], axis=1)
    acc = xp[:, 0:seg] * w[0]
    for j in range(1, n_prev + 1):
        acc = acc + xp[:, j:j + seg] * w[j]
    return acc


def _segment_scan(a, b, h_ref):
    seg = a.shape[1]
    local, decay = b[:, 0], a[:, 0]
    locals_, decays = [local], [decay]
    for s in range(1, seg):
        local = a[:, s] * local + b[:, s]
        decay = a[:, s] * decay
        locals_.append(local)
        decays.append(decay)
    h_in = h_ref[:, SUBLANES - 1:SUBLANES, :]
    row = _sublane_iota(local.shape)
    seg_a, seg_b = decay, local
    d = 1
    while d < SUBLANES:
        a_prev = pltpu.roll(seg_a, d, 1)
        b_prev = pltpu.roll(seg_b, d, 1)
        keep = row >= d
        seg_b = jnp.where(keep, seg_a * b_prev + seg_b, seg_b)
        seg_a = jnp.where(keep, seg_a * a_prev, seg_a)
        d *= 2
    h_end = seg_a * h_in + seg_b
    h_ref[...] = h_end
    h_start = jnp.where(row == 0, h_in, pltpu.roll(h_end, 1, 1))
    return jnp.stack([locals_[s] + decays[s] * h_start for s in range(seg)], axis=1)


def _grid_step():
    return (pl.program_id(0) * pl.num_programs(1) + pl.program_id(1),
            pl.num_programs(0) * pl.num_programs(1))


def _tile_copies(hbm_ref, buf_ref, sem, b, t, nb, tt, to_vmem):
    seg = tt // SUBLANES
    copies = []
    for r in range(SUBLANES):
        hbm = hbm_ref.at[pl.ds(b * nb, nb), pl.ds(t * tt + r * seg, seg), :]
        vmem = buf_ref.at[:, :, r, :]
        copies.append(pltpu.make_async_copy(hbm, vmem, sem) if to_vmem
                      else pltpu.make_async_copy(vmem, hbm, sem))
    return copies


def _head_copies(hbm_refs, buf_ref, sem, l, b, to_vmem):
    n_seq = buf_ref.shape[1]
    copies = []
    for j, hbm_ref in enumerate(hbm_refs):
        for hd in range(X_HEADS):
            hbm = hbm_ref.at[l, pl.ds(b * n_seq, n_seq), :, hd, :]
            vmem = buf_ref.at[j, :, :, pl.ds(hd * X_HD, X_HD)]
            copies.append(pltpu.make_async_copy(hbm, vmem, sem) if to_vmem
                          else pltpu.make_async_copy(vmem, hbm, sem))
    return copies


def _next_grid_index():
    wraps = pl.program_id(1) + 1 == pl.num_programs(1)
    return pl.program_id(0) + wraps.astype(jnp.int32), jnp.where(wraps, 0, pl.program_id(1) + 1)


def _prefetched(fetch):
    step, n_steps = _grid_step()
    slot = lax.rem(step, 2)

    @pl.when(step == 0)
    def _():
        for c in fetch(pl.program_id(0), pl.program_id(1), slot):
            c.start()

    @pl.when(step + 1 < n_steps)
    def _():
        for c in fetch(*_next_grid_index(), 1 - slot):
            c.start()

    for c in fetch(pl.program_id(0), pl.program_id(1), slot):
        c.wait()
    return slot


def _flushed(flush, fill):
    step, n_steps = _grid_step()
    slot = lax.rem(step, 2)

    @pl.when(step >= 2)
    def _():
        for c in flush(slot):
            c.wait()

    fill(slot)
    for c in flush(slot):
        c.start()

    @pl.when(step == n_steps - 1)
    def _():
        @pl.when(n_steps > 1)
        def _():
            for c in flush(1 - slot):
                c.wait()
        for c in flush(slot):
            c.wait()


def _mixer_kernel(x_ref, ca0_ref, h0_ref, cb0_ref, gmix_ref, win_ref, caw_ref, cab_ref, wg_ref,
                  brg_ref, big_ref, lam_ref, cbw_ref, gv_ref, gsum_ref, ws_ref, bsf_ref, wout_ref,
                  xo_ref, cao_ref, ho_ref, cbo_ref, *rest, nb, gb, tt, chunk, emit_vc, gather_x):
    if gather_x:
        *rest, x_buf, x_sem = rest
    if emit_vc:
        vco_ref, ca_tail, cb_tail, h_state, wmix, ycat, zbuf = rest
    else:
        ca_tail, cb_tail, h_state, wmix, ycat, zbuf = rest
    rows = gb * tt
    seg = tt // SUBLANES
    groups = nb // gb

    if gather_x:
        slot = _prefetched(lambda b, t, into: _tile_copies(x_ref, x_buf.at[into], x_sem.at[into],
                                                           b, t, nb, tt, to_vmem=True))

    def read_x(g):
        if gather_x:
            return x_buf[slot, g * gb:(g + 1) * gb].reshape(rows, D_MODEL)
        return x_ref[g * gb:(g + 1) * gb].reshape(rows, D_MODEL)

    @pl.when(pl.program_id(1) == 0)
    def _():
        ca_tail[...] = ca0_ref[...]
        cb_tail[...] = cb0_ref[...]
        h_state[...] = h0_ref[...]
        ti = lax.broadcasted_iota(jnp.int32, (tt, tt), 0)
        tj = lax.broadcasted_iota(jnp.int32, (tt, tt), 1)
        time_i = (ti % SUBLANES) * seg + ti // SUBLANES
        time_j = (tj % SUBLANES) * seg + tj // SUBLANES
        allowed = (time_i >= time_j) & (time_i // chunk == time_j // chunk)
        for hd in range(C_HEADS):
            wmix[hd] = jnp.where(allowed, ws_ref[hd], 0.0).astype(BF16)

    def project_in(g):
        xn = _rms(read_x(g), gmix_ref[...]).astype(BF16)
        z = zbuf.at[g % Z_SLOTS]
        width = IN_COLS // PROJ_PARTS

        def part(c):
            z[:, c * width:(c + 1) * width] = _dot(xn, win_ref[:, c * width:(c + 1) * width])

        return z, [functools.partial(part, c) for c in range(PROJ_PARTS)]

    col = {}
    o = 0
    for name, width in (("xa", A_WIDTH), ("ga", A_WIDTH), ("xb", B_WIDTH), ("gb", B_WIDTH),
                        ("gc", B_WIDTH), ("uc", C_WIDTH), ("vc", C_WIDTH)):
        col[name] = slice(o, o + width)
        o += width
    half = A_WIDTH // 2

    def mix_matmuls(g, z):
        seqs = slice(g * gb, (g + 1) * gb)
        out = {}

        def gates_and_norm():
            xc = _causal_taps(z[:, col["xa"]].reshape(gb, seg, SUBLANES, A_WIDTH), ca_tail.at[seqs],
                              caw_ref[...])
            xc2 = (xc + cab_ref[...]).reshape(rows, A_WIDTH)
            xcb = xc2.astype(BF16)
            out["xc2"] = xc2
            out["gz0"] = _dot(xcb[:, :half], wg_ref[0])
            out["gz1"] = _dot(xcb[:, half:], wg_ref[1])
            v = _gelu(z[:, col["vc"]])
            sq = v * v
            sq_hi = sq.astype(BF16)
            sq_lo = (sq - sq_hi.astype(F32)).astype(BF16)
            out["v"] = v
            out["ms"] = _dot(sq_hi, gsum_ref[...]) + _dot(sq_lo, gsum_ref[...])

        def spatial_mix():
            vn = out.pop("v") * lax.rsqrt(out.pop("ms") + EPS) * gv_ref[...]
            if emit_vc:
                vco_ref[seqs] = vn.reshape(gb, tt, C_WIDTH)
            vnb = vn.astype(BF16)
            first_head = lax.broadcasted_iota(jnp.int32, (tt, LANES), 1) < C_HD
            mixed = []
            for b in range(gb):
                pieces = []
                for pair in range(C_WIDTH // LANES):
                    vp = vnb[b * tt:(b + 1) * tt, pair * LANES:(pair + 1) * LANES]
                    pieces.append(jnp.where(first_head, _dot(wmix[2 * pair], vp), _dot(wmix[2 * pair + 1], vp)))
                mixed.append(jnp.concatenate(pieces, axis=1) + bsf_ref[...])
            out["mixed"] = jnp.concatenate(mixed, axis=0)

        return out, [gates_and_norm, spatial_mix]

    def mix_elementwise(g, z, xc2, gz0, gz1, mixed):
        seqs = slice(g * gb, (g + 1) * gb)
        y = ycat.at[g * rows:(g + 1) * rows]
        r = jax.nn.sigmoid(jnp.concatenate([gz0[:, :half], gz1[:, :half]], axis=1) + brg_ref[...])
        gi = jax.nn.sigmoid(jnp.concatenate([gz0[:, half:], gz1[:, half:]], axis=1) + big_ref[...])
        neg_lam = -lam_ref[...]
        softplus = jnp.maximum(neg_lam, 0.0) + jnp.log1p(jnp.exp(-jnp.abs(neg_lam)))
        decay_rate = RG_C * softplus
        a = jnp.exp2((-LOG2_E) * decay_rate * r)
        one_minus_a2 = (1.0 + a * a) * jnp.tanh(decay_rate * r)
        root = jnp.where(one_minus_a2 > 0.0, one_minus_a2 * lax.rsqrt(one_minus_a2), 0.0)
        bterm = root * (gi * xc2)
        h = _segment_scan(a.reshape(gb, seg, SUBLANES, A_WIDTH),
                          bterm.reshape(gb, seg, SUBLANES, A_WIDTH), h_state.at[seqs])
        y[:, 0:A_WIDTH] = (_gelu(z[:, col["ga"]]) * h.reshape(rows, A_WIDTH)).astype(BF16)
        zb = _causal_taps((z[:, col["gc"]] * z[:, col["xb"]]).reshape(gb, seg, SUBLANES, B_WIDTH),
                          cb_tail.at[seqs], cbw_ref[...])
        y[:, A_WIDTH:A_WIDTH + B_WIDTH] = (z[:, col["gb"]] * zb.reshape(rows, B_WIDTH)).astype(BF16)
        y[:, A_WIDTH + B_WIDTH:] = (_gelu(z[:, col["uc"]]) * mixed).astype(BF16)

    def project_out(g):
        out = read_x(g) + _dot(ycat[g * rows:(g + 1) * rows], wout_ref[...])
        xo_ref[g * gb:(g + 1) * gb] = out.reshape(gb, tt, D_MODEL)

    def interleave(big_steps, small_steps):
        stride = max(1, len(big_steps) // max(1, len(small_steps)))
        pending = list(small_steps)
        for i, step in enumerate(big_steps):
            step()
            if pending and (i + 1) % stride == 0:
                pending.pop(0)()
        for step in pending:
            step()

    zs, small = {}, {}
    zs[0], parts = project_in(0)
    interleave(parts, [])
    small[0], steps = mix_matmuls(0, zs[0])
    if groups > 1:
        zs[1], parts = project_in(1)
        interleave(parts, steps)
    else:
        interleave([], steps)
    for g in range(groups):
        parts, steps = [], []
        if g + 2 < groups:
            zs[g + 2], parts = project_in(g + 2)
        if g + 1 < groups:
            small[g + 1], steps = mix_matmuls(g + 1, zs[g + 1])
        interleave(parts, steps)
        res = small.pop(g)
        mix_elementwise(g, zs.pop(g), res["xc2"], res["gz0"], res["gz1"], res["mixed"])
        project_out(g)
    cao_ref[...] = ca_tail[...]
    ho_ref[...] = h_state[...]
    cbo_ref[...] = cb_tail[...]


def _attn_kernel(x_ref, kb, vb, gx_ref, wq_ref, wo_ref, xo_ref, ocat, *, nb, tt):
    rows = nb * tt
    x = x_ref[...].reshape(rows, D_MODEL)
    xn = _rms(x, gx_ref[...]).astype(BF16)
    q = (_dot(xn, wq_ref[...]) * (X_HD ** -0.5)).astype(BF16)
    for seq in range(nb):
        r0 = seq * tt
        for hd in range(X_HEADS):
            cols = slice(hd * X_HD, (hd + 1) * X_HD)
            s = lax.dot_general(q[r0:r0 + tt, cols], kb[seq, :, cols], (((1,), (1,)), ((), ())),
                                preferred_element_type=F32)
            e = jnp.exp(s - jnp.max(s, axis=-1, keepdims=True))
            denom = jnp.sum(e, axis=-1, keepdims=True)
            oh = _dot(e.astype(BF16), vb[seq, :, cols]) / denom
            ocat[r0:r0 + tt, cols] = oh.astype(BF16)
    out = x + _dot(ocat[...], wo_ref[...])
    xo_ref[...] = out.reshape(nb, tt, D_MODEL)


def _ffn_kernel(x_ref, cf0_ref, gffn_ref, wup_ref, cfw_ref, wdown_ref, gfin_ref, xo_ref, cfo_ref,
                cf_tail, act_buf, *rest, nb, tt, final_norm, scatter_out):
    rows = nb * tt
    seg = tt // SUBLANES

    @pl.when(pl.program_id(1) == 0)
    def _():
        cf_tail[...] = cf0_ref[...]

    deferred = final_norm and scatter_out
    if deferred:
        out_buf, out_sem = rest
        step, n_steps = _grid_step()
        slot = lax.rem(step, 2)
        NORM_ROWS = min(tt, LANES)
        pieces = rows // NORM_ROWS
        assert tt % NORM_ROWS == 0 and pieces <= D_FF // MXU_DIM

        @pl.when(step == 0)
        def _():
            out_buf[1 - slot] = jnp.zeros((nb, seg, SUBLANES, D_MODEL), F32)

    x = x_ref[...].reshape(rows, D_MODEL)
    xn = _rms(x, gffn_ref[...]).astype(BF16)
    for c0 in range(0, D_FF, MXU_DIM):
        cols = slice(c0, c0 + MXU_DIM)
        gate = _dot(xn, wup_ref[:, cols])
        up = _dot(xn, wup_ref[:, D_FF + c0:D_FF + c0 + MXU_DIM])
        gconv = _causal_taps(gate.reshape(nb, seg, SUBLANES, MXU_DIM), cf_tail.at[:, :, :, cols],
                             cfw_ref[:, :, cols])
        act_buf[:, cols] = (jax.nn.silu(gconv.reshape(rows, MXU_DIM)) * up).astype(BF16)
        if deferred and c0 // MXU_DIM < pieces:
            seq_i, part = divmod(c0 // MXU_DIM, tt // NORM_ROWS)
            regs = slice(part * (NORM_ROWS // SUBLANES), (part + 1) * (NORM_ROWS // SUBLANES))
            piece = out_buf[1 - slot, seq_i, regs].reshape(NORM_ROWS, D_MODEL)
            out_buf[1 - slot, seq_i, regs] = _rms(piece, gfin_ref[...]).reshape(-1, SUBLANES, D_MODEL)
    cfo_ref[...] = cf_tail[...]
    out = x + _dot(act_buf[...], wdown_ref[...])
    if deferred:
        def copies(b, t, from_slot):
            return _tile_copies(xo_ref, out_buf.at[from_slot], out_sem.at[from_slot], b, t, nb, tt,
                                to_vmem=False)

        b, t = pl.program_id(0), pl.program_id(1)
        first_t = t == 0
        prev_b, prev_t = b - first_t.astype(jnp.int32), jnp.where(first_t, pl.num_programs(1) - 1, t - 1)

        @pl.when(step >= 1)
        def _():
            for c in copies(prev_b, prev_t, 1 - slot):
                c.start()

        @pl.when(step >= 2)
        def _():
            for c in copies(b, t, slot):
                c.wait()

        out_buf[slot] = out.reshape(nb, seg, SUBLANES, D_MODEL)

        @pl.when(step == n_steps - 1)
        def _():
            @pl.when(step >= 1)
            def _():
                for c in copies(b, t, 1 - slot):
                    c.wait()
            out_buf[slot] = _rms(out, gfin_ref[...]).reshape(nb, seg, SUBLANES, D_MODEL)
            for c in copies(b, t, slot):
                c.start()
            for c in copies(b, t, slot):
                c.wait()
        return

    if final_norm:
        out = _rms(out, gfin_ref[...])
    if not scatter_out:
        xo_ref[...] = out.reshape(nb, tt, D_MODEL)
        return

    out_buf, out_sem = rest

    def flush(from_slot):
        return _tile_copies(xo_ref, out_buf.at[from_slot], out_sem.at[from_slot],
                            pl.program_id(0), pl.program_id(1), nb, tt, to_vmem=False)

    def fill(slot):
        out_buf[slot] = out.reshape(nb, seg, SUBLANES, D_MODEL)

    _flushed(flush, fill)


def _kv_kernel(mem_ref, wk_ref, wv_ref, ko_hbm, vo_hbm, kb_ref, vb_ref, kv_buf, sem):
    n_seq = mem_ref.shape[0]
    m = mem_ref[...].reshape(n_seq * N_MEM, D_MODEL).astype(BF16)
    k = _dot(m, wk_ref[...]).reshape(n_seq, N_MEM, D_MODEL)
    v = _dot(m, wv_ref[...]).reshape(n_seq, N_MEM, D_MODEL)
    kb_ref[...] = k.astype(BF16)
    vb_ref[...] = v.astype(BF16)

    def flush(from_slot):
        return _head_copies((ko_hbm, vo_hbm), kv_buf.at[from_slot], sem.at[from_slot],
                            pl.program_id(0), pl.program_id(1), to_vmem=False)

    def fill(slot):
        kv_buf[slot, 0] = k
        kv_buf[slot, 1] = v

    _flushed(flush, fill)


def _kv_cast_kernel(k_hbm, v_hbm, kb_ref, vb_ref, kv_buf, sem):
    slot = _prefetched(lambda l, b, into: _head_copies((k_hbm, v_hbm), kv_buf.at[into], sem.at[into],
                                                       l, b, to_vmem=True))
    kb_ref[...] = kv_buf[slot, 0].astype(BF16)
    vb_ref[...] = kv_buf[slot, 1].astype(BF16)


def _whole():
    return pl.BlockSpec(memory_space=pltpu.MemorySpace.VMEM)


def _of_layer(arr, l):
    zeros = (0,) * (arr.ndim - 1)
    return pl.BlockSpec((None,) + arr.shape[1:], lambda b, t: (l,) + zeros, pipeline_mode=pl.Buffered(1))


def _params():
    return pltpu.CompilerParams(dimension_semantics=("arbitrary", "arbitrary"),
                                vmem_limit_bytes=VMEM_LIMIT_BYTES)


def _tile_specs(nb, tt):
    x_spec = pl.BlockSpec((nb, tt, D_MODEL), lambda b, t: (b, t, 0))

    def state_spec(n, width):
        return pl.BlockSpec((nb, n, SUBLANES, width), lambda b, t: (b, 0, 0, 0))

    return x_spec, state_spec


def _mixer_call(x, ca0, h0, cb0, p, l, nb, tt, chunk, emit_vc, gather_x):
    bsz, seq, _ = x.shape
    layer_params = [p[name] for name in ("g_mix", "w_in", "conv_a_w", "conv_a_b", "w_gates", "b_rg", "b_ig",
                                         "lam", "conv_b_w", "g_v")]
    mix_params = [p[name] for name in ("w_s", "b_s_full", "w_out")]
    x_spec, state_spec = _tile_specs(nb, tt)
    h_spec = pl.BlockSpec((nb, SUBLANES, A_WIDTH), lambda b, t: (b, 0, 0))
    out_shape = [jax.ShapeDtypeStruct(x.shape, F32),
                 jax.ShapeDtypeStruct(ca0.shape, F32),
                 jax.ShapeDtypeStruct(h0.shape, F32),
                 jax.ShapeDtypeStruct(cb0.shape, F32)]
    out_specs = [x_spec, state_spec(A_CONV - 1, A_WIDTH), h_spec, state_spec(B_CONV - 1, B_WIDTH)]
    if emit_vc:
        out_shape.append(jax.ShapeDtypeStruct((bsz, seq, C_WIDTH), F32))
        out_specs.append(pl.BlockSpec((nb, tt, C_WIDTH), lambda b, t: (b, t, 0)))
    gb = nb if seq == tt else 1
    kern = functools.partial(_mixer_kernel, nb=nb, gb=gb, tt=tt, chunk=chunk, emit_vc=emit_vc,
                             gather_x=gather_x)
    gather_scratch = [pltpu.VMEM((2, nb, tt // SUBLANES, SUBLANES, D_MODEL), F32),
                      pltpu.SemaphoreType.DMA((2,))] if gather_x else []
    return pl.pallas_call(
        kern,
        grid=(bsz // nb, seq // tt),
        in_specs=([pl.BlockSpec(memory_space=pl.ANY) if gather_x else x_spec,
                   state_spec(A_CONV - 1, A_WIDTH), h_spec, state_spec(B_CONV - 1, B_WIDTH)]
                  + [_of_layer(a, l) for a in layer_params] + [_whole()]
                  + [_of_layer(a, l) for a in mix_params]),
        out_specs=out_specs,
        out_shape=out_shape,
        scratch_shapes=[pltpu.VMEM((nb, A_CONV - 1, SUBLANES, A_WIDTH), F32),
                        pltpu.VMEM((nb, B_CONV - 1, SUBLANES, B_WIDTH), F32),
                        pltpu.VMEM((nb, SUBLANES, A_WIDTH), F32),
                        pltpu.VMEM((C_HEADS, tt, tt), BF16),
                        pltpu.VMEM((nb * tt, D_MODEL), BF16),
                        pltpu.VMEM((Z_SLOTS, gb * tt, IN_COLS), F32)] + gather_scratch,
        compiler_params=_params(),
        name="mixer",
    )(x, ca0, h0, cb0, *layer_params, p["group_mean"], *mix_params)


def _attn_call(x, kb, vb, p, l, nb, tt):
    bsz, seq, _ = x.shape
    x_spec, _ = _tile_specs(nb, tt)
    kv_spec = pl.BlockSpec((None, nb, N_MEM, D_MODEL), lambda b, t: (l, b, 0, 0))
    weights = [p["g_x"], p["w_q"], p["w_o"]]
    kern = functools.partial(_attn_kernel, nb=nb, tt=tt)
    return pl.pallas_call(
        kern,
        grid=(bsz // nb, seq // tt),
        in_specs=[x_spec, kv_spec, kv_spec] + [_of_layer(a, l) for a in weights],
        out_specs=x_spec,
        out_shape=jax.ShapeDtypeStruct(x.shape, F32),
        scratch_shapes=[pltpu.VMEM((nb * tt, D_MODEL), BF16)],
        compiler_params=_params(),
        name="attn",
    )(x, kb, vb, *weights)


def _ffn_call(x, cf0, p, l, nb, tt, final_norm, scatter_out):
    bsz, seq, _ = x.shape
    x_spec, state_spec = _tile_specs(nb, tt)
    weights = [p["g_ffn"], p["w_up"], p["conv_f_w"], p["w_down"]]
    kern = functools.partial(_ffn_kernel, nb=nb, tt=tt, final_norm=final_norm, scatter_out=scatter_out)
    scatter_scratch = [pltpu.VMEM((2, nb, tt // SUBLANES, SUBLANES, D_MODEL), F32),
                       pltpu.SemaphoreType.DMA((2,))] if scatter_out else []
    return pl.pallas_call(
        kern,
        grid=(bsz // nb, seq // tt),
        in_specs=[x_spec, state_spec(FFN_CONV - 1, D_FF)] + [_of_layer(a, l) for a in weights] + [_whole()],
        out_specs=[pl.BlockSpec(memory_space=pl.ANY) if scatter_out else x_spec,
                   state_spec(FFN_CONV - 1, D_FF)],
        out_shape=[jax.ShapeDtypeStruct(x.shape, F32), jax.ShapeDtypeStruct(cf0.shape, F32)],
        scratch_shapes=[pltpu.VMEM((nb, FFN_CONV - 1, SUBLANES, D_FF), F32),
                        pltpu.VMEM((nb * tt, D_FF), BF16)] + scatter_scratch,
        compiler_params=_params(),
        name="ffn",
    )(x, cf0, *weights, p["g_final"])


def _kv_call(mem, w_k, w_v):
    bsz = mem.shape[0]
    depth = w_k.shape[0]
    n_seq = min(bsz, KV_BATCH_TILE)
    assert bsz % n_seq == 0, bsz
    w_spec = pl.BlockSpec((None, D_MODEL, D_MODEL), lambda l, b: (l, 0, 0))
    out5 = pl.BlockSpec(memory_space=pl.ANY)
    out4 = pl.BlockSpec((None, n_seq, N_MEM, D_MODEL), lambda l, b: (l, b, 0, 0))
    shape5 = jax.ShapeDtypeStruct((depth, bsz, N_MEM, X_HEADS, X_HD), F32)
    shape4 = jax.ShapeDtypeStruct((depth, bsz, N_MEM, D_MODEL), BF16)
    return pl.pallas_call(
        _kv_kernel,
        grid=(depth, bsz // n_seq),
        in_specs=[pl.BlockSpec((n_seq, N_MEM, D_MODEL), lambda l, b: (b, 0, 0)), w_spec, w_spec],
        out_specs=[out5, out5, out4, out4],
        out_shape=[shape5, shape5, shape4, shape4],
        scratch_shapes=[pltpu.VMEM((2, 2, n_seq, N_MEM, D_MODEL), F32), pltpu.SemaphoreType.DMA((2,))],
        compiler_params=_params(),
        name="memory_kv",
    )(mem, w_k, w_v)


def _kv_cast_call(cache_k, cache_v):
    depth, bsz = cache_k.shape[:2]
    n_seq = min(bsz, KV_BATCH_TILE)
    assert bsz % n_seq == 0, bsz
    in_spec = pl.BlockSpec(memory_space=pl.ANY)
    out_spec = pl.BlockSpec((None, n_seq, N_MEM, D_MODEL), lambda l, b: (l, b, 0, 0))
    shape = jax.ShapeDtypeStruct((depth, bsz, N_MEM, D_MODEL), BF16)
    return pl.pallas_call(
        _kv_cast_kernel,
        grid=(depth, bsz // n_seq),
        in_specs=[in_spec, in_spec],
        out_specs=[out_spec, out_spec],
        out_shape=[shape, shape],
        scratch_shapes=[pltpu.VMEM((2, 2, n_seq, N_MEM, D_MODEL), F32), pltpu.SemaphoreType.DMA((2,))],
        compiler_params=_params(),
        name="cache_kv_cast",
    )(cache_k, cache_v)


def _block_diag(blocks):
    n, r, c = blocks.shape[-3:]
    eye = jnp.eye(n, dtype=blocks.dtype)
    return (blocks[..., :, :, None, :] * eye[:, None, :, None]).reshape(blocks.shape[:-3] + (n * r, n * c))


def _tile_times(tt):
    i = jnp.arange(tt)
    return (i % SUBLANES) * (tt // SUBLANES) + i // SUBLANES


def _to_segments(x, tt):
    bsz, seq, c = x.shape
    return x.reshape(bsz, seq // tt, SUBLANES, tt // SUBLANES, c).swapaxes(2, 3).reshape(bsz, seq, c)


def _from_segments(x, tt):
    bsz, seq, c = x.shape
    return x.reshape(bsz, seq // tt, tt // SUBLANES, SUBLANES, c).swapaxes(2, 3).reshape(bsz, seq, c)


def _on_sublanes(state):
    return jnp.broadcast_to(state[..., None, :], state.shape[:-1] + (SUBLANES, state.shape[-1]))


def _shared_params(g_mix, w_in, conv_a_w, conv_a_b, w_rg, b_rg, w_ig, b_ig, lam, conv_b_w, g_v, w_out,
                   g_x, w_q, w_o, g_ffn, w_up, conv_f_w, w_down, g_final):
    per_half = A_HEADS // 2
    gates = [jnp.concatenate([_block_diag(w_rg[:, j * per_half:(j + 1) * per_half]),
                              _block_diag(w_ig[:, j * per_half:(j + 1) * per_half])], axis=-1)
             for j in range(2)]
    group = jnp.arange(C_WIDTH) // C_HD

    def row(a):
        return a[:, None, :]

    def taps(a):
        return a[:, :, None, :]

    return dict(
        g_mix=row(g_mix), w_in=w_in.astype(BF16), conv_a_w=taps(conv_a_w), conv_a_b=row(conv_a_b),
        w_gates=jnp.stack(gates, axis=1).astype(BF16), b_rg=row(b_rg), b_ig=row(b_ig), lam=row(lam),
        conv_b_w=taps(conv_b_w), g_v=row(g_v),
        group_mean=((group[:, None] == group[None, :]).astype(F32) / C_HD).astype(BF16),
        w_out=w_out.astype(BF16), g_x=row(g_x), w_q=w_q.astype(BF16), w_o=w_o.astype(BF16),
        g_ffn=row(g_ffn), w_up=w_up.astype(BF16), conv_f_w=taps(conv_f_w), w_down=w_down.astype(BF16),
        g_final=g_final[None])


def _tile_params(tt, chunk, w_s, b_s):
    pos = _tile_times(tt) % chunk
    return dict(w_s=w_s[:, :, pos][:, :, :, pos],
                b_s_full=jnp.repeat(jnp.swapaxes(b_s[:, :, pos], 1, 2), C_HD, axis=2))


def _tiles(bsz, seq):
    if seq <= MLP_CHUNK:
        return bsz, seq
    return PROMPT_BATCH_TILE, PROMPT_TIME_TILE


def _trunk(x, kb, vb, states, p, emit_vc):
    bsz, seq, d_model = x.shape
    nb, tt = _tiles(bsz, seq)
    chunk = min(seq, MLP_CHUNK)
    by_dma = seq > tt
    assert d_model == D_MODEL and bsz % nb == 0 and seq % tt == 0 and tt % chunk == 0, x.shape
    assert tt % SUBLANES == 0 and tt // SUBLANES >= A_CONV - 1, tt
    assert not by_dma or seq % ATTN_TIME_TILE == 0, seq
    assert kb.shape == vb.shape == (DEPTH, bsz, N_MEM, D_MODEL), kb.shape
    if not by_dma:
        x = _to_segments(x, tt)
    new_states = [[] for _ in range(4)]
    vcs = []
    for l in range(DEPTH):
        last = l == DEPTH - 1
        ca0, h0, cb0, cf0 = (s[l] for s in states)
        mix = _mixer_call(x, ca0, h0, cb0, p, l, nb, tt, chunk, emit_vc, by_dma and l == 0)
        x = _attn_call(mix[0], kb, vb, p, l, *((1, ATTN_TIME_TILE) if by_dma else (nb, tt)))
        x, cf = _ffn_call(x, cf0, p, l, nb, tt, last, by_dma and last)
        for acc, val in zip(new_states, (mix[1], mix[2], mix[3], cf)):
            acc.append(val)
        if emit_vc:
            vcs.append(mix[4])
    new_states = tuple(jnp.stack(s)[..., SUBLANES - 1, :] for s in new_states)
    vc = _from_segments(jnp.concatenate(vcs, axis=0), tt).reshape(DEPTH, bsz, seq, C_WIDTH) if emit_vc else None
    return (x if by_dma else _from_segments(x, tt)), new_states, vc


def kernel(x_prompt, x_sample, mem_prompt, cache_mem_k, cache_mem_v, state_conv_a, state_h_a, state_conv_b, state_conv_ffn, g_mix, w_in, conv_a_w, conv_a_b, w_rg, b_rg, w_ig, b_ig, lam, conv_b_w, g_v, w_s, b_s, w_out, g_x, w_q, w_k, w_v, w_o, g_ffn, w_up, conv_f_w, w_down, g_final):
    bp, seq_p, _ = x_prompt.shape
    bs, seq_s, _ = x_sample.shape
    assert mem_prompt.shape == (bp, N_MEM, D_MODEL), mem_prompt.shape
    assert cache_mem_k.shape == cache_mem_v.shape == (DEPTH, bs, N_MEM, X_HEADS, X_HD), cache_mem_k.shape
    assert w_in.shape == (DEPTH, D_MODEL, IN_COLS) and w_up.shape == (DEPTH, D_MODEL, 2 * D_FF), w_in.shape
    shared = _shared_params(g_mix, w_in, conv_a_w, conv_a_b, w_rg, b_rg, w_ig, b_ig, lam, conv_b_w, g_v,
                            w_out, g_x, w_q, w_o, g_ffn, w_up, conv_f_w, w_down, g_final)

    def tile_params(bsz, seq):
        return dict(shared, **_tile_params(_tiles(bsz, seq)[1], min(seq, MLP_CHUNK), w_s, b_s))

    mem_k, mem_v, kb, vb = _kv_call(mem_prompt, w_k.astype(BF16), w_v.astype(BF16))
    zero_states = tuple(jnp.zeros((DEPTH, bp) + shape + (SUBLANES, width), F32)
                        for shape, width in (((A_CONV - 1,), A_WIDTH), ((), A_WIDTH),
                                             ((B_CONV - 1,), B_WIDTH), ((FFN_CONV - 1,), D_FF)))
    y_p, states_p, _ = _trunk(x_prompt, kb, vb, zero_states, tile_params(bp, seq_p), False)
    carried = tuple(_on_sublanes(s) for s in (state_conv_a, state_h_a, state_conv_b, state_conv_ffn))
    cache_kb, cache_vb = _kv_cast_call(cache_mem_k, cache_mem_v)
    y_s, states_s, vc_s = _trunk(x_sample, cache_kb, cache_vb, carried, tile_params(bs, seq_s), True)
    return (y_p, y_s) + states_p + (mem_k, mem_v) + states_s + (vc_s,)
```

```python
import functools
import math

import jax
import jax.numpy as jnp
from jax import lax
from jax.experimental import pallas as pl
from jax.experimental.pallas import tpu as pltpu

D_MODEL = 1024
DEPTH = 2
A_WIDTH = 512
A_HEADS = 8
A_HD = 64
A_CONV = 4
RG_C = 8.0
B_WIDTH = 256
B_CONV = 3
C_WIDTH = 256
C_HEADS = 4
C_HD = 64
MLP_CHUNK = 128
N_MEM = 256
X_HEADS = 4
X_HD = 256
D_FF = 2816
FFN_CONV = 3
EPS = 1e-6
IN_COLS = 2 * A_WIDTH + 3 * B_WIDTH + 2 * C_WIDTH

SUBLANES = 8
LANES = 128
MXU_DIM = 256
V7X_VMEM_BYTES = 64 * 1024 * 1024
VMEM_LIMIT_BYTES = V7X_VMEM_BYTES * 7 // 8
PROMPT_TIME_TILE = 256
PROMPT_BATCH_TILE = 4
ATTN_TIME_TILE = 1024
KV_BATCH_TILE = 4
Z_SLOTS = 3
PROJ_PARTS = IN_COLS // MXU_DIM

LOG2_E = math.log2(math.e)
GELU_C0 = math.sqrt(2.0 / math.pi)
GELU_C1 = 0.044715

F32 = jnp.float32
BF16 = jnp.bfloat16


def _dot(a, b):
    return jnp.dot(a, b, preferred_element_type=F32)


def _rms(x, g):
    ms = jnp.mean(x * x, axis=-1, keepdims=True)
    return x * lax.rsqrt(ms + EPS) * g


def _gelu(x):
    scale = -2.0 * LOG2_E * GELU_C0
    return x / (1.0 + jnp.exp2((x * x * (scale * GELU_C1) + scale) * x))


def _sublane_iota(shape):
    return lax.broadcasted_iota(jnp.int32, shape, len(shape) - 2)


def _causal_taps(cur, tail_ref, w):
    n_prev = tail_ref.shape[1]
    seg = cur.shape[1]
    last = cur[:, seg - n_prev:]
    wrapped = pltpu.roll(jnp.where(_sublane_iota(last.shape) == SUBLANES - 1, tail_ref[...], last), 1, 2)
    tail_ref[...] = last
    xp = jnp.concatenate([wrapped, cur], axis=1)
    acc = xp[:, 0:seg] * w[0]
    for j in range(1, n_prev + 1):
        acc = acc + xp[:, j:j + seg] * w[j]
    return acc


def _segment_scan(a, b, h_ref):
    seg = a.shape[1]
    local, decay = b[:, 0], a[:, 0]
    locals_, decays = [local], [decay]
    for s in range(1, seg):
        local = a[:, s] * local + b[:, s]
        decay = a[:, s] * decay
        locals_.append(local)
        decays.append(decay)
    h_in = h_ref[:, SUBLANES - 1:SUBLANES, :]
    row = _sublane_iota(local.shape)
    seg_a, seg_b = decay, local
    d = 1
    while d < SUBLANES:
        a_prev = pltpu.roll(seg_a, d, 1)
        b_prev = pltpu.roll(seg_b, d, 1)
        keep = row >= d
        seg_b = jnp.where(keep, seg_a * b_prev + seg_b, seg_b)
        seg_a = jnp.where(keep, seg_a * a_prev, seg_a)
        d *= 2
    h_end = seg_a * h_in + seg_b
    h_ref[...] = h_end
    h_start = jnp.where(row == 0, h_in, pltpu.roll(h_end, 1, 1))
    return jnp.stack([locals_[s] + decays[s] * h_start for s in range(seg)], axis=1)


def _grid_step():
    return (pl.program_id(0) * pl.num_programs(1) + pl.program_id(1),
            pl.num_programs(0) * pl.num_programs(1))


def _tile_copies(hbm_ref, buf_ref, sem, b, t, nb, tt, to_vmem):
    seg = tt // SUBLANES
    copies = []
    for r in range(SUBLANES):
        hbm = hbm_ref.at[pl.ds(b * nb, nb), pl.ds(t * tt + r * seg, seg), :]
        vmem = buf_ref.at[:, :, r, :]
        copies.append(pltpu.make_async_copy(hbm, vmem, sem) if to_vmem
                      else pltpu.make_async_copy(vmem, hbm, sem))
    return copies


def _head_copies(hbm_refs, buf_ref, sem, l, b, to_vmem):
    n_seq = buf_ref.shape[1]
    copies = []
    for j, hbm_ref in enumerate(hbm_refs):
        for hd in range(X_HEADS):
            hbm = hbm_ref.at[l, pl.ds(b * n_seq, n_seq), :, hd, :]
            vmem = buf_ref.at[j, :, :, pl.ds(hd * X_HD, X_HD)]
            copies.append(pltpu.make_async_copy(hbm, vmem, sem) if to_vmem
                          else pltpu.make_async_copy(vmem, hbm, sem))
    return copies


def _next_grid_index():
    wraps = pl.program_id(1) + 1 == pl.num_programs(1)
    return pl.program_id(0) + wraps.astype(jnp.int32), jnp.where(wraps, 0, pl.program_id(1) + 1)


def _prefetched(fetch):
    step, n_steps = _grid_step()
    slot = lax.rem(step, 2)

    @pl.when(step == 0)
    def _():
        for i, c in enumerate(fetch(pl.program_id(0), pl.program_id(1), slot)):
            c.start(priority=i % 2)

    @pl.when(step + 1 < n_steps)
    def _():
        for i, c in enumerate(fetch(*_next_grid_index(), 1 - slot)):
            c.start(priority=i % 2)

    for c in fetch(pl.program_id(0), pl.program_id(1), slot):
        c.wait()
    return slot


def _flushed(flush, fill):
    step, n_steps = _grid_step()
    slot = lax.rem(step, 2)

    @pl.when(step >= 2)
    def _():
        for c in flush(slot):
            c.wait()

    fill(slot)
    for i, c in enumerate(flush(slot)):
        c.start(priority=i % 2)

    @pl.when(step == n_steps - 1)
    def _():
        @pl.when(n_steps > 1)
        def _():
            for c in flush(1 - slot):
                c.wait()
        for c in flush(slot):
            c.wait()


def _mixer_kernel(x_ref, ca0_ref, h0_ref, cb0_ref, gmix_ref, win_ref, caw_ref, cab_ref, wg_ref,
                  brg_ref, big_ref, lam_ref, cbw_ref, gv_ref, gsum_ref, ws_ref, bsf_ref, wout_ref,
                  xo_ref, cao_ref, ho_ref, cbo_ref, *rest, nb, gb, tt, chunk, emit_vc, gather_x):
    if gather_x:
        *rest, x_buf, x_sem = rest
    if emit_vc:
        vco_ref, ca_tail, cb_tail, h_state, wmix, ycat, zbuf = rest
    else:
        ca_tail, cb_tail, h_state, wmix, ycat, zbuf = rest
    rows = gb * tt
    seg = tt // SUBLANES
    groups = nb // gb

    if gather_x:
        slot = _prefetched(lambda b, t, into: _tile_copies(x_ref, x_buf.at[into], x_sem.at[into],
                                                           b, t, nb, tt, to_vmem=True))

    def read_x(g):
        if gather_x:
            return x_buf[slot, g * gb:(g + 1) * gb].reshape(rows, D_MODEL)
        return x_ref[g * gb:(g + 1) * gb].reshape(rows, D_MODEL)

    @pl.when(pl.program_id(1) == 0)
    def _():
        ca_tail[...] = ca0_ref[...]
        cb_tail[...] = cb0_ref[...]
        h_state[...] = h0_ref[...]
        ti = lax.broadcasted_iota(jnp.int32, (tt, tt), 0)
        tj = lax.broadcasted_iota(jnp.int32, (tt, tt), 1)
        time_i = (ti % SUBLANES) * seg + ti // SUBLANES
        time_j = (tj % SUBLANES) * seg + tj // SUBLANES
        allowed = (time_i >= time_j) & (time_i // chunk == time_j // chunk)
        for hd in range(C_HEADS):
            wmix[hd] = jnp.where(allowed, ws_ref[hd], 0.0).astype(BF16)

    def project_in(g):
        xn = _rms(read_x(g), gmix_ref[...]).astype(BF16)
        z = zbuf.at[g % Z_SLOTS]
        width = IN_COLS // PROJ_PARTS

        def part(c):
            z[:, c * width:(c + 1) * width] = _dot(xn, win_ref[:, c * width:(c + 1) * width])

        return z, [functools.partial(part, c) for c in range(PROJ_PARTS)]

    col = {}
    o = 0
    for name, width in (("xa", A_WIDTH), ("ga", A_WIDTH), ("xb", B_WIDTH), ("gb", B_WIDTH),
                        ("gc", B_WIDTH), ("uc", C_WIDTH), ("vc", C_WIDTH)):
        col[name] = slice(o, o + width)
        o += width
    half = A_WIDTH // 2

    def mix_matmuls(g, z):
        seqs = slice(g * gb, (g + 1) * gb)
        out = {}

        def gates_and_norm():
            xc = _causal_taps(z[:, col["xa"]].reshape(gb, seg, SUBLANES, A_WIDTH), ca_tail.at[seqs],
                              caw_ref[...])
            xc2 = (xc + cab_ref[...]).reshape(rows, A_WIDTH)
            xcb = xc2.astype(BF16)
            out["xc2"] = xc2
            out["gz0"] = _dot(xcb[:, :half], wg_ref[0])
            out["gz1"] = _dot(xcb[:, half:], wg_ref[1])
            v = _gelu(z[:, col["vc"]])
            sq = v * v
            sq_hi = sq.astype(BF16)
            sq_lo = (sq - sq_hi.astype(F32)).astype(BF16)
            out["v"] = v
            out["ms"] = _dot(sq_hi, gsum_ref[...]) + _dot(sq_lo, gsum_ref[...])

        def spatial_mix():
            vn = out.pop("v") * lax.rsqrt(out.pop("ms") + EPS) * gv_ref[...]
            if emit_vc:
                vco_ref[seqs] = vn.reshape(gb, tt, C_WIDTH)
            vnb = vn.astype(BF16)
            first_head = lax.broadcasted_iota(jnp.int32, (tt, LANES), 1) < C_HD
            mixed = []
            for b in range(gb):
                pieces = []
                for pair in range(C_WIDTH // LANES):
                    vp = vnb[b * tt:(b + 1) * tt, pair * LANES:(pair + 1) * LANES]
                    pieces.append(jnp.where(first_head, _dot(wmix[2 * pair], vp), _dot(wmix[2 * pair + 1], vp)))
                mixed.append(jnp.concatenate(pieces, axis=1) + bsf_ref[...])
            out["mixed"] = jnp.concatenate(mixed, axis=0)

        return out, [gates_and_norm, spatial_mix]

    def mix_elementwise(g, z, xc2, gz0, gz1, mixed):
        seqs = slice(g * gb, (g + 1) * gb)
        y = ycat.at[g * rows:(g + 1) * rows]
        r = jax.nn.sigmoid(jnp.concatenate([gz0[:, :half], gz1[:, :half]], axis=1) + brg_ref[...])
        gi = jax.nn.sigmoid(jnp.concatenate([gz0[:, half:], gz1[:, half:]], axis=1) + big_ref[...])
        neg_lam = -lam_ref[...]
        softplus = jnp.maximum(neg_lam, 0.0) + jnp.log1p(jnp.exp(-jnp.abs(neg_lam)))
        decay_rate = RG_C * softplus
        a = jnp.exp2((-LOG2_E) * decay_rate * r)
        one_minus_a2 = (1.0 + a * a) * jnp.tanh(decay_rate * r)
        root = jnp.where(one_minus_a2 > 0.0, one_minus_a2 * lax.rsqrt(one_minus_a2), 0.0)
        bterm = root * (gi * xc2)
        h = _segment_scan(a.reshape(gb, seg, SUBLANES, A_WIDTH),
                          bterm.reshape(gb, seg, SUBLANES, A_WIDTH), h_state.at[seqs])
        y[:, 0:A_WIDTH] = (_gelu(z[:, col["ga"]]) * h.reshape(rows, A_WIDTH)).astype(BF16)
        zb = _causal_taps((z[:, col["gc"]] * z[:, col["xb"]]).reshape(gb, seg, SUBLANES, B_WIDTH),
                          cb_tail.at[seqs], cbw_ref[...])
        y[:, A_WIDTH:A_WIDTH + B_WIDTH] = (z[:, col["gb"]] * zb.reshape(rows, B_WIDTH)).astype(BF16)
        y[:, A_WIDTH + B_WIDTH:] = (_gelu(z[:, col["uc"]]) * mixed).astype(BF16)

    def project_out(g):
        out = read_x(g) + _dot(ycat[g * rows:(g + 1) * rows], wout_ref[...])
        xo_ref[g * gb:(g + 1) * gb] = out.reshape(gb, tt, D_MODEL)

    def interleave(big_steps, small_steps):
        stride = max(1, len(big_steps) // max(1, len(small_steps)))
        pending = list(small_steps)
        for i, step in enumerate(big_steps):
            step()
            if pending and (i + 1) % stride == 0:
                pending.pop(0)()
        for step in pending:
            step()

    zs, small = {}, {}
    zs[0], parts = project_in(0)
    interleave(parts, [])
    small[0], steps = mix_matmuls(0, zs[0])
    if groups > 1:
        zs[1], parts = project_in(1)
        interleave(parts, steps)
    else:
        interleave([], steps)
    for g in range(groups):
        parts, steps = [], []
        if g + 2 < groups:
            zs[g + 2], parts = project_in(g + 2)
        if g + 1 < groups:
            small[g + 1], steps = mix_matmuls(g + 1, zs[g + 1])
        interleave(parts, steps)
        res = small.pop(g)
        mix_elementwise(g, zs.pop(g), res["xc2"], res["gz0"], res["gz1"], res["mixed"])
        project_out(g)
    cao_ref[...] = ca_tail[...]
    ho_ref[...] = h_state[...]
    cbo_ref[...] = cb_tail[...]


def _attn_kernel(x_ref, kb, vb, gx_ref, wq_ref, wo_ref, xo_ref, ocat, *, nb, tt):
    rows = nb * tt
    x = x_ref[...].reshape(rows, D_MODEL)
    xn = _rms(x, gx_ref[...]).astype(BF16)
    q = (_dot(xn, wq_ref[...]) * (X_HD ** -0.5)).astype(BF16)
    for seq in range(nb):
        r0 = seq * tt
        for hd in range(X_HEADS):
            cols = slice(hd * X_HD, (hd + 1) * X_HD)
            s = lax.dot_general(q[r0:r0 + tt, cols], kb[seq, :, cols], (((1,), (1,)), ((), ())),
                                preferred_element_type=F32)
            e = jnp.exp(s - jnp.max(s, axis=-1, keepdims=True))
            denom = jnp.sum(e, axis=-1, keepdims=True)
            oh = _dot(e.astype(BF16), vb[seq, :, cols]) / denom
            ocat[r0:r0 + tt, cols] = oh.astype(BF16)
    out = x + _dot(ocat[...], wo_ref[...])
    xo_ref[...] = out.reshape(nb, tt, D_MODEL)


def _ffn_kernel(x_ref, cf0_ref, gffn_ref, wup_ref, cfw_ref, wdown_ref, gfin_ref, xo_ref, cfo_ref,
                cf_tail, act_buf, *rest, nb, tt, final_norm, scatter_out):
    rows = nb * tt
    seg = tt // SUBLANES

    @pl.when(pl.program_id(1) == 0)
    def _():
        cf_tail[...] = cf0_ref[...]

    x = x_ref[...].reshape(rows, D_MODEL)
    xn = _rms(x, gffn_ref[...]).astype(BF16)
    for c0 in range(0, D_FF, MXU_DIM):
        cols = slice(c0, c0 + MXU_DIM)
        gate = _dot(xn, wup_ref[:, cols])
        up = _dot(xn, wup_ref[:, D_FF + c0:D_FF + c0 + MXU_DIM])
        gconv = _causal_taps(gate.reshape(nb, seg, SUBLANES, MXU_DIM), cf_tail.at[:, :, :, cols],
                             cfw_ref[:, :, cols])
        act_buf[:, cols] = (jax.nn.silu(gconv.reshape(rows, MXU_DIM)) * up).astype(BF16)
    cfo_ref[...] = cf_tail[...]
    out = x + _dot(act_buf[...], wdown_ref[...])
    if final_norm:
        out = _rms(out, gfin_ref[...])
    if not scatter_out:
        xo_ref[...] = out.reshape(nb, tt, D_MODEL)
        return

    out_buf, out_sem = rest

    def flush(from_slot):
        return _tile_copies(xo_ref, out_buf.at[from_slot], out_sem.at[from_slot],
                            pl.program_id(0), pl.program_id(1), nb, tt, to_vmem=False)

    def fill(slot):
        out_buf[slot] = out.reshape(nb, seg, SUBLANES, D_MODEL)

    _flushed(flush, fill)


def _kv_kernel(mem_ref, wk_ref, wv_ref, ko_hbm, vo_hbm, kb_ref, vb_ref, kv_buf, sem):
    n_seq = mem_ref.shape[0]
    m = mem_ref[...].reshape(n_seq * N_MEM, D_MODEL).astype(BF16)
    k = _dot(m, wk_ref[...]).reshape(n_seq, N_MEM, D_MODEL)
    v = _dot(m, wv_ref[...]).reshape(n_seq, N_MEM, D_MODEL)
    kb_ref[...] = k.astype(BF16)
    vb_ref[...] = v.astype(BF16)

    def flush(from_slot):
        return _head_copies((ko_hbm, vo_hbm), kv_buf.at[from_slot], sem.at[from_slot],
                            pl.program_id(0), pl.program_id(1), to_vmem=False)

    def fill(slot):
        kv_buf[slot, 0] = k
        kv_buf[slot, 1] = v

    _flushed(flush, fill)


def _kv_cast_kernel(k_hbm, v_hbm, kb_ref, vb_ref, kv_buf, sem):
    slot = _prefetched(lambda l, b, into: _head_copies((k_hbm, v_hbm), kv_buf.at[into], sem.at[into],
                                                       l, b, to_vmem=True))
    kb_ref[...] = kv_buf[slot, 0].astype(BF16)
    vb_ref[...] = kv_buf[slot, 1].astype(BF16)


def _whole():
    return pl.BlockSpec(memory_space=pltpu.MemorySpace.VMEM)


def _of_layer(arr, l):
    zeros = (0,) * (arr.ndim - 1)
    return pl.BlockSpec((None,) + arr.shape[1:], lambda b, t: (l,) + zeros, pipeline_mode=pl.Buffered(1))


def _params():
    return pltpu.CompilerParams(dimension_semantics=("arbitrary", "arbitrary"),
                                vmem_limit_bytes=VMEM_LIMIT_BYTES)


def _tile_specs(nb, tt):
    x_spec = pl.BlockSpec((nb, tt, D_MODEL), lambda b, t: (b, t, 0))

    def state_spec(n, width):
        return pl.BlockSpec((nb, n, SUBLANES, width), lambda b, t: (b, 0, 0, 0))

    return x_spec, state_spec


def _mixer_call(x, ca0, h0, cb0, p, l, nb, tt, chunk, emit_vc, gather_x):
    bsz, seq, _ = x.shape
    layer_params = [p[name] for name in ("g_mix", "w_in", "conv_a_w", "conv_a_b", "w_gates", "b_rg", "b_ig",
                                         "lam", "conv_b_w", "g_v")]
    mix_params = [p[name] for name in ("w_s", "b_s_full", "w_out")]
    x_spec, state_spec = _tile_specs(nb, tt)
    h_spec = pl.BlockSpec((nb, SUBLANES, A_WIDTH), lambda b, t: (b, 0, 0))
    out_shape = [jax.ShapeDtypeStruct(x.shape, F32),
                 jax.ShapeDtypeStruct(ca0.shape, F32),
                 jax.ShapeDtypeStruct(h0.shape, F32),
                 jax.ShapeDtypeStruct(cb0.shape, F32)]
    out_specs = [x_spec, state_spec(A_CONV - 1, A_WIDTH), h_spec, state_spec(B_CONV - 1, B_WIDTH)]
    if emit_vc:
        out_shape.append(jax.ShapeDtypeStruct((bsz, seq, C_WIDTH), F32))
        out_specs.append(pl.BlockSpec((nb, tt, C_WIDTH), lambda b, t: (b, t, 0)))
    gb = nb if seq == tt else 1
    kern = functools.partial(_mixer_kernel, nb=nb, gb=gb, tt=tt, chunk=chunk, emit_vc=emit_vc,
                             gather_x=gather_x)
    gather_scratch = [pltpu.VMEM((2, nb, tt // SUBLANES, SUBLANES, D_MODEL), F32),
                      pltpu.SemaphoreType.DMA((2,))] if gather_x else []
    return pl.pallas_call(
        kern,
        grid=(bsz // nb, seq // tt),
        in_specs=([pl.BlockSpec(memory_space=pl.ANY) if gather_x else x_spec,
                   state_spec(A_CONV - 1, A_WIDTH), h_spec, state_spec(B_CONV - 1, B_WIDTH)]
                  + [_of_layer(a, l) for a in layer_params] + [_whole()]
                  + [_of_layer(a, l) for a in mix_params]),
        out_specs=out_specs,
        out_shape=out_shape,
        scratch_shapes=[pltpu.VMEM((nb, A_CONV - 1, SUBLANES, A_WIDTH), F32),
                        pltpu.VMEM((nb, B_CONV - 1, SUBLANES, B_WIDTH), F32),
                        pltpu.VMEM((nb, SUBLANES, A_WIDTH), F32),
                        pltpu.VMEM((C_HEADS, tt, tt), BF16),
                        pltpu.VMEM((nb * tt, D_MODEL), BF16),
                        pltpu.VMEM((Z_SLOTS, gb * tt, IN_COLS), F32)] + gather_scratch,
        compiler_params=_params(),
        name="mixer",
    )(x, ca0, h0, cb0, *layer_params, p["group_mean"], *mix_params)


def _attn_call(x, kb, vb, p, l, nb, tt):
    bsz, seq, _ = x.shape
    x_spec, _ = _tile_specs(nb, tt)
    kv_spec = pl.BlockSpec((None, nb, N_MEM, D_MODEL), lambda b, t: (l, b, 0, 0))
    weights = [p["g_x"], p["w_q"], p["w_o"]]
    kern = functools.partial(_attn_kernel, nb=nb, tt=tt)
    return pl.pallas_call(
        kern,
        grid=(bsz // nb, seq // tt),
        in_specs=[x_spec, kv_spec, kv_spec] + [_of_layer(a, l) for a in weights],
        out_specs=x_spec,
        out_shape=jax.ShapeDtypeStruct(x.shape, F32),
        scratch_shapes=[pltpu.VMEM((nb * tt, D_MODEL), BF16)],
        compiler_params=_params(),
        name="attn",
    )(x, kb, vb, *weights)


def _ffn_call(x, cf0, p, l, nb, tt, final_norm, scatter_out):
    bsz, seq, _ = x.shape
    x_spec, state_spec = _tile_specs(nb, tt)
    weights = [p["g_ffn"], p["w_up"], p["conv_f_w"], p["w_down"]]
    kern = functools.partial(_ffn_kernel, nb=nb, tt=tt, final_norm=final_norm, scatter_out=scatter_out)
    scatter_scratch = [pltpu.VMEM((2, nb, tt // SUBLANES, SUBLANES, D_MODEL), F32),
                       pltpu.SemaphoreType.DMA((2,))] if scatter_out else []
    return pl.pallas_call(
        kern,
        grid=(bsz // nb, seq // tt),
        in_specs=[x_spec, state_spec(FFN_CONV - 1, D_FF)] + [_of_layer(a, l) for a in weights] + [_whole()],
        out_specs=[pl.BlockSpec(memory_space=pl.ANY) if scatter_out else x_spec,
                   state_spec(FFN_CONV - 1, D_FF)],
        out_shape=[jax.ShapeDtypeStruct(x.shape, F32), jax.ShapeDtypeStruct(cf0.shape, F32)],
        scratch_shapes=[pltpu.VMEM((nb, FFN_CONV - 1, SUBLANES, D_FF), F32),
                        pltpu.VMEM((nb * tt, D_FF), BF16)] + scatter_scratch,
        compiler_params=_params(),
        name="ffn",
    )(x, cf0, *weights, p["g_final"])


def _kv_call(mem, w_k, w_v):
    bsz = mem.shape[0]
    depth = w_k.shape[0]
    n_seq = min(bsz, KV_BATCH_TILE)
    assert bsz % n_seq == 0, bsz
    w_spec = pl.BlockSpec((None, D_MODEL, D_MODEL), lambda l, b: (l, 0, 0))
    out5 = pl.BlockSpec(memory_space=pl.ANY)
    out4 = pl.BlockSpec((None, n_seq, N_MEM, D_MODEL), lambda l, b: (l, b, 0, 0))
    shape5 = jax.ShapeDtypeStruct((depth, bsz, N_MEM, X_HEADS, X_HD), F32)
    shape4 = jax.ShapeDtypeStruct((depth, bsz, N_MEM, D_MODEL), BF16)
    return pl.pallas_call(
        _kv_kernel,
        grid=(depth, bsz // n_seq),
        in_specs=[pl.BlockSpec((n_seq, N_MEM, D_MODEL), lambda l, b: (b, 0, 0)), w_spec, w_spec],
        out_specs=[out5, out5, out4, out4],
        out_shape=[shape5, shape5, shape4, shape4],
        scratch_shapes=[pltpu.VMEM((2, 2, n_seq, N_MEM, D_MODEL), F32), pltpu.SemaphoreType.DMA((2,))],
        compiler_params=_params(),
        name="memory_kv",
    )(mem, w_k, w_v)


def _kv_cast_call(cache_k, cache_v):
    depth, bsz = cache_k.shape[:2]
    n_seq = min(bsz, KV_BATCH_TILE)
    assert bsz % n_seq == 0, bsz
    in_spec = pl.BlockSpec(memory_space=pl.ANY)
    out_spec = pl.BlockSpec((None, n_seq, N_MEM, D_MODEL), lambda l, b: (l, b, 0, 0))
    shape = jax.ShapeDtypeStruct((depth, bsz, N_MEM, D_MODEL), BF16)
    return pl.pallas_call(
        _kv_cast_kernel,
        grid=(depth, bsz // n_seq),
        in_specs=[in_spec, in_spec],
        out_specs=[out_spec, out_spec],
        out_shape=[shape, shape],
        scratch_shapes=[pltpu.VMEM((2, 2, n_seq, N_MEM, D_MODEL), F32), pltpu.SemaphoreType.DMA((2,))],
        compiler_params=_params(),
        name="cache_kv_cast",
    )(cache_k, cache_v)


def _block_diag(blocks):
    n, r, c = blocks.shape[-3:]
    eye = jnp.eye(n, dtype=blocks.dtype)
    return (blocks[..., :, :, None, :] * eye[:, None, :, None]).reshape(blocks.shape[:-3] + (n * r, n * c))


def _tile_times(tt):
    i = jnp.arange(tt)
    return (i % SUBLANES) * (tt // SUBLANES) + i // SUBLANES


def _to_segments(x, tt):
    bsz, seq, c = x.shape
    return x.reshape(bsz, seq // tt, SUBLANES, tt // SUBLANES, c).swapaxes(2, 3).reshape(bsz, seq, c)


def _from_segments(x, tt):
    bsz, seq, c = x.shape
    return x.reshape(bsz, seq // tt, tt // SUBLANES, SUBLANES, c).swapaxes(2, 3).reshape(bsz, seq, c)


def _on_sublanes(state):
    return jnp.broadcast_to(state[..., None, :], state.shape[:-1] + (SUBLANES, state.shape[-1]))


def _shared_params(g_mix, w_in, conv_a_w, conv_a_b, w_rg, b_rg, w_ig, b_ig, lam, conv_b_w, g_v, w_out,
                   g_x, w_q, w_o, g_ffn, w_up, conv_f_w, w_down, g_final):
    per_half = A_HEADS // 2
    gates = [jnp.concatenate([_block_diag(w_rg[:, j * per_half:(j + 1) * per_half]),
                              _block_diag(w_ig[:, j * per_half:(j + 1) * per_half])], axis=-1)
             for j in range(2)]
    group = jnp.arange(C_WIDTH) // C_HD

    def row(a):
        return a[:, None, :]

    def taps(a):
        return a[:, :, None, :]

    return dict(
        g_mix=row(g_mix), w_in=w_in.astype(BF16), conv_a_w=taps(conv_a_w), conv_a_b=row(conv_a_b),
        w_gates=jnp.stack(gates, axis=1).astype(BF16), b_rg=row(b_rg), b_ig=row(b_ig), lam=row(lam),
        conv_b_w=taps(conv_b_w), g_v=row(g_v),
        group_mean=((group[:, None] == group[None, :]).astype(F32) / C_HD).astype(BF16),
        w_out=w_out.astype(BF16), g_x=row(g_x), w_q=w_q.astype(BF16), w_o=w_o.astype(BF16),
        g_ffn=row(g_ffn), w_up=w_up.astype(BF16), conv_f_w=taps(conv_f_w), w_down=w_down.astype(BF16),
        g_final=g_final[None])


def _tile_params(tt, chunk, w_s, b_s):
    pos = _tile_times(tt) % chunk
    return dict(w_s=w_s[:, :, pos][:, :, :, pos],
                b_s_full=jnp.repeat(jnp.swapaxes(b_s[:, :, pos], 1, 2), C_HD, axis=2))


def _tiles(bsz, seq):
    if seq <= MLP_CHUNK:
        return bsz, seq
    return PROMPT_BATCH_TILE, PROMPT_TIME_TILE


def _trunk(x, kb, vb, states, p, emit_vc):
    bsz, seq, d_model = x.shape
    nb, tt = _tiles(bsz, seq)
    chunk = min(seq, MLP_CHUNK)
    by_dma = seq > tt
    assert d_model == D_MODEL and bsz % nb == 0 and seq % tt == 0 and tt % chunk == 0, x.shape
    assert tt % SUBLANES == 0 and tt // SUBLANES >= A_CONV - 1, tt
    assert not by_dma or seq % ATTN_TIME_TILE == 0, seq
    assert kb.shape == vb.shape == (DEPTH, bsz, N_MEM, D_MODEL), kb.shape
    if not by_dma:
        x = _to_segments(x, tt)
    new_states = [[] for _ in range(4)]
    vcs = []
    for l in range(DEPTH):
        last = l == DEPTH - 1
        ca0, h0, cb0, cf0 = (s[l] for s in states)
        mix = _mixer_call(x, ca0, h0, cb0, p, l, nb, tt, chunk, emit_vc, by_dma and l == 0)
        x = _attn_call(mix[0], kb, vb, p, l, *((1, ATTN_TIME_TILE) if by_dma else (nb, tt)))
        x, cf = _ffn_call(x, cf0, p, l, nb, tt, last, by_dma and last)
        for acc, val in zip(new_states, (mix[1], mix[2], mix[3], cf)):
            acc.append(val)
        if emit_vc:
            vcs.append(mix[4])
    new_states = tuple(jnp.stack(s)[..., SUBLANES - 1, :] for s in new_states)
    vc = _from_segments(jnp.concatenate(vcs, axis=0), tt).reshape(DEPTH, bsz, seq, C_WIDTH) if emit_vc else None
    return (x if by_dma else _from_segments(x, tt)), new_states, vc


def kernel(x_prompt, x_sample, mem_prompt, cache_mem_k, cache_mem_v, state_conv_a, state_h_a, state_conv_b, state_conv_ffn, g_mix, w_in, conv_a_w, conv_a_b, w_rg, b_rg, w_ig, b_ig, lam, conv_b_w, g_v, w_s, b_s, w_out, g_x, w_q, w_k, w_v, w_o, g_ffn, w_up, conv_f_w, w_down, g_final):
    bp, seq_p, _ = x_prompt.shape
    bs, seq_s, _ = x_sample.shape
    assert mem_prompt.shape == (bp, N_MEM, D_MODEL), mem_prompt.shape
    assert cache_mem_k.shape == cache_mem_v.shape == (DEPTH, bs, N_MEM, X_HEADS, X_HD), cache_mem_k.shape
    assert w_in.shape == (DEPTH, D_MODEL, IN_COLS) and w_up.shape == (DEPTH, D_MODEL, 2 * D_FF), w_in.shape
    shared = _shared_params(g_mix, w_in, conv_a_w, conv_a_b, w_rg, b_rg, w_ig, b_ig, lam, conv_b_w, g_v,
                            w_out, g_x, w_q, w_o, g_ffn, w_up, conv_f_w, w_down, g_final)

    def tile_params(bsz, seq):
        return dict(shared, **_tile_params(_tiles(bsz, seq)[1], min(seq, MLP_CHUNK), w_s, b_s))

    mem_k, mem_v, kb, vb = _kv_call(mem_prompt, w_k.astype(BF16), w_v.astype(BF16))
    zero_states = tuple(jnp.zeros((DEPTH, bp) + shape + (SUBLANES, width), F32)
                        for shape, width in (((A_CONV - 1,), A_WIDTH), ((), A_WIDTH),
                                             ((B_CONV - 1,), B_WIDTH), ((FFN_CONV - 1,), D_FF)))
    y_p, states_p, _ = _trunk(x_prompt, kb, vb, zero_states, tile_params(bp, seq_p), False)
    carried = tuple(_on_sublanes(s) for s in (state_conv_a, state_h_a, state_conv_b, state_conv_ffn))
    cache_kb, cache_vb = _kv_cast_call(cache_mem_k, cache_mem_v)
    y_s, states_s, vc_s = _trunk(x_sample, cache_kb, cache_vb, carried, tile_params(bs, seq_s), True)
    return (y_p, y_s) + states_p + (mem_k, mem_v) + states_s + (vc_s,)
```
